```python
import math
import jax, jax.numpy as jnp
from jax import lax
import numpy as np

D_MODEL = 1024
BATCH = 2
SEQ = 16384
DEPTH = 2

CTX_LEN = 256
GRID_W = 64
Q_BLOCK = 128
ROPE_THETA = 10000.0
HEAD_DIM = 64
GROUP_WIDTH = D_MODEL // 4
MIX_WIDTH = 4 * GROUP_WIDTH
A_HEADS = GROUP_WIDTH // HEAD_DIM
A_KV_HEADS = A_HEADS // 2
POOL_WINDOWS = (2, 4, 8, 16)
POOL_GROUP = GROUP_WIDTH // len(POOL_WINDOWS)
C_HEADS = GROUP_WIDTH // HEAD_DIM
C_QK_DIM = HEAD_DIM // 2
D_HEADS = 4
CHUNK = 128
PROJ_WIDTHS = (A_HEADS * HEAD_DIM, A_KV_HEADS * HEAD_DIM, A_KV_HEADS * HEAD_DIM,
               GROUP_WIDTH,
               C_HEADS * 2 * C_QK_DIM, C_HEADS * 2 * C_QK_DIM, C_HEADS * HEAD_DIM,
               GROUP_WIDTH, GROUP_WIDTH)
PROJ_WIDTH = sum(PROJ_WIDTHS)
N_EXPERTS = 64
TOP_K = 8
N_EXPERT_GROUPS = 8
TOPK_GROUPS = 4
EXPERT_HIDDEN = D_MODEL // 4
SHARED_HIDDEN = EXPERT_HIDDEN
ROUTED_SCALE = 2.5

kernel_name = 'hybrid_parallel_heads_dit_block'


def rmsnorm(x, g, eps=1e-6):
    xf = x.astype(jnp.float32)
    y = xf * lax.rsqrt(jnp.mean(xf * xf, axis=-1, keepdims=True) + eps)
    return y.astype(x.dtype) * g


def layernorm_plain(x, eps=1e-5):
    xf = x.astype(jnp.float32)
    mu = jnp.mean(xf, axis=-1, keepdims=True)
    var = jnp.mean(jnp.square(xf - mu), axis=-1, keepdims=True)
    return ((xf - mu) * lax.rsqrt(var + eps)).astype(x.dtype)


def modulate(h, shift, scale):
    return h * (1 + scale) + shift


def axial_rope_tables(rows, dim):
    row = jnp.repeat(jnp.arange(rows, dtype=jnp.float32), GRID_W)
    col = jnp.tile(jnp.arange(GRID_W, dtype=jnp.float32), rows)
    half = dim // 2
    inv = ROPE_THETA ** (-jnp.arange(0, half, 2, dtype=jnp.float32) / half)
    ar = row[:, None] * inv[None, :]
    ac = col[:, None] * inv[None, :]
    ang = jnp.concatenate([ar, ar, ac, ac], axis=-1)
    return jnp.cos(ang), jnp.sin(ang)


def apply_axial_rope(x, cos, sin):
    n, d = cos.shape
    shape = (n,) + (1,) * (x.ndim - 3) + (d,)
    cos = cos.reshape(shape).astype(x.dtype)
    sin = sin.reshape(shape).astype(x.dtype)
    x1, x2, x3, x4 = jnp.split(x, 4, axis=-1)
    rot = jnp.concatenate([-x2, x1, -x4, x3], axis=-1)
    return x * cos + rot * sin


def sweep_query_blocks(fn, q):
    b, s = q.shape[:2]
    nb = s // Q_BLOCK
    qb = jnp.moveaxis(q.reshape((b, nb, Q_BLOCK) + q.shape[2:]), 1, 0)
    ob = lax.map(fn, qb)
    return jnp.moveaxis(ob, 0, 1).reshape((b, s) + ob.shape[3:])


def split_proj(p):
    idx = np.cumsum(np.array(PROJ_WIDTHS))[:-1].tolist()
    return jnp.split(p, idx, axis=-1)


def gqa_attend(q, k, v):
    s = jnp.einsum('bqkgd,bskd->bkgqs', q, k).astype(jnp.float32) * (q.shape[-1] ** -0.5)
    p = jax.nn.softmax(s, axis=-1).astype(v.dtype)
    return jnp.einsum('bkgqs,bskd->bqkgd', p, v)


def mixer_gqa(qc, kc, vc, ql, kl, vl, q_gain, k_gain, rope, need_ctx):
    cos, sin = rope

    def q_heads(p):
        return rmsnorm(p.reshape(p.shape[:2] + (A_KV_HEADS, A_HEADS // A_KV_HEADS, HEAD_DIM)), q_gain)

    def kv_heads(p):
        return p.reshape(p.shape[:2] + (A_KV_HEADS, HEAD_DIM))

    kc_h = rmsnorm(kv_heads(kc), k_gain)
    vc_h = kv_heads(vc)
    kl_h = apply_axial_rope(rmsnorm(kv_heads(kl), k_gain), cos, sin)
    ql_h = apply_axial_rope(q_heads(ql), cos, sin)
    k_all = jnp.concatenate([kc_h, kl_h], axis=1)
    v_all = jnp.concatenate([vc_h, kv_heads(vl)], axis=1)
    yl = sweep_query_blocks(lambda qb: gqa_attend(qb, k_all, v_all), ql_h)
    yl = yl.reshape(yl.shape[:2] + (-1,))
    yc = None
    if need_ctx:
        yc = gqa_attend(q_heads(qc), kc_h, vc_h)
        yc = yc.reshape(yc.shape[:2] + (-1,))
    return yc, yl


def diff_attend(q, k, v, lam):
    s = jnp.einsum('bqhcd,bshcd->bhcqs', q, k).astype(jnp.float32) * (q.shape[-1] ** -0.5)
    p = jax.nn.softmax(s, axis=-1)
    a = (p[:, :, 0] - lam * p[:, :, 1]).astype(v.dtype)
    return jnp.einsum('bhqs,bshd->bqhd', a, v)


def mixer_diff(qc, kc, vc, ql, kl, vl, lam_qk, subln_gain, lam_init, rope, need_ctx):
    cos, sin = rope
    lq = lam_qk.astype(jnp.float32)
    lam = jnp.exp(jnp.sum(lq[0] * lq[1])) - jnp.exp(jnp.sum(lq[2] * lq[3])) + lam_init

    def qk_heads(p):
        return p.reshape(p.shape[:2] + (C_HEADS, 2, C_QK_DIM))

    def v_heads(p):
        return p.reshape(p.shape[:2] + (C_HEADS, HEAD_DIM))

    def finish(o):
        o = rmsnorm(o, subln_gain) * (1.0 - lam_init)
        return o.reshape(o.shape[:2] + (-1,))

    kc_h = qk_heads(kc)
    vc_h = v_heads(vc)
    k_all = jnp.concatenate([kc_h, apply_axial_rope(qk_heads(kl), cos, sin)], axis=1)
    v_all = jnp.concatenate([vc_h, v_heads(vl)], axis=1)
    ql_h = apply_axial_rope(qk_heads(ql), cos, sin)
    yl = finish(sweep_query_blocks(lambda qb: diff_attend(qb, k_all, v_all, lam), ql_h))
    yc = finish(diff_attend(qk_heads(qc), kc_h, vc_h, lam)) if need_ctx else None
    return yc, yl


def mixer_pool(p, pool_w, pool_scale):
    b, n, ch = p.shape
    pf = p.astype(jnp.float32)
    cs = jnp.concatenate([jnp.zeros((b, 1, ch), jnp.float32), jnp.cumsum(pf, axis=1)], axis=1)
    t = np.arange(n)
    outs = []
    for g, w in enumerate(POOL_WINDOWS):
        lo = np.clip(t - w // 2, 0, n)
        hi = np.clip(t - w // 2 + w, 0, n)
        sl = slice(g * POOL_GROUP, (g + 1) * POOL_GROUP)
        cnt = (hi - lo).astype(np.float32)[None, :, None]
        win_mean = (cs[:, hi, sl] - cs[:, lo, sl]) / cnt
        outs.append((win_mean - pf[:, :, sl]).astype(p.dtype) @ pool_w[g])
    return jnp.concatenate(outs, axis=-1) * pool_scale


def mixer_sgu(u, v, sgu_w, sgu_b):
    b, n, w = v.shape
    vc = layernorm_plain(v).reshape(b, n // CHUNK, CHUNK, D_HEADS, w // D_HEADS)
    sv = jnp.einsum('hpq,bnqhc->bnphc', sgu_w, vc) + jnp.transpose(sgu_b)[:, :, None]
    return u * sv.reshape(b, n, w)


def token_mix(pc, pl, a_q_gain, a_k_gain, pool_w, pool_scale, lam_qk, c_subln_gain, lam_init,
              sgu_w, sgu_b, rope_a, rope_c, need_ctx):
    aqc, akc, avc, bc, cqc, ckc, cvc, duc, dvc = split_proj(pc)
    aql, akl, avl, bl, cql, ckl, cvl, dul, dvl = split_proj(pl)
    ya_c, ya_l = mixer_gqa(aqc, akc, avc, aql, akl, avl, a_q_gain, a_k_gain, rope_a, need_ctx)
    yc_c, yc_l = mixer_diff(cqc, ckc, cvc, cql, ckl, cvl, lam_qk, c_subln_gain, lam_init, rope_c, need_ctx)
    yl = jnp.concatenate([ya_l, mixer_pool(bl, pool_w, pool_scale), yc_l,
                          mixer_sgu(dul, dvl, sgu_w, sgu_b)], axis=-1)
    yc = None
    if need_ctx:
        yc = jnp.concatenate([ya_c, mixer_pool(bc, pool_w, pool_scale), yc_c,
                              mixer_sgu(duc, dvc, sgu_w, sgu_b)], axis=-1)
    return yc, yl


def swiglu(h, w1, w3, w2):
    return (jax.nn.silu(h @ w1) * (h @ w3)) @ w2


def moe(h, router_w, router_bias, exp_w1, exp_w3, exp_w2, sh_w1, sh_w3, sh_w2):
    n = h.shape[0]
    scores = jax.nn.sigmoid((h @ router_w).astype(jnp.float32))
    sel = scores + router_bias.astype(jnp.float32)
    per_group = N_EXPERTS // N_EXPERT_GROUPS
    gscore = jnp.sum(lax.top_k(sel.reshape(n, N_EXPERT_GROUPS, per_group), 2)[0], axis=-1)
    _, gidx = lax.top_k(gscore, TOPK_GROUPS)
    gmask = jnp.sum(jax.nn.one_hot(gidx, N_EXPERT_GROUPS, dtype=jnp.float32), axis=1)
    emask = jnp.repeat(gmask, per_group, axis=1) > 0
    _, eidx = lax.top_k(jnp.where(emask, sel, -jnp.inf), TOP_K)
    wts = jnp.take_along_axis(scores, eidx, axis=1)
    wts = wts / jnp.sum(wts, axis=-1, keepdims=True) * ROUTED_SCALE
    gates = jnp.sum(jax.nn.one_hot(eidx, N_EXPERTS, dtype=jnp.float32) * wts[..., None], axis=1)

    def body(acc, ew):
        w1, w3, w2, g = ew
        return acc + g[:, None] * swiglu(h, w1, w3, w2), None

    routed, _ = lax.scan(body, jnp.zeros_like(h), (exp_w1, exp_w3, exp_w2, gates.T.astype(h.dtype)))
    return swiglu(h, sh_w1, sh_w3, sh_w2) + routed


def setup_inputs(seed: int = 0) -> dict:
    key = jax.random.key(seed)
    ks = jax.random.split(key, 32)
    f32 = jnp.float32
    L, D = DEPTH, D_MODEL

    def nrm(k, shape, scale):
        return jax.random.normal(k, shape, f32) * scale

    return {
        'x': nrm(ks[0], (BATCH, SEQ, D), 1.0),
        'c': nrm(ks[1], (BATCH, D), 1.0),
        'ctx': nrm(ks[2], (BATCH, CTX_LEN, D), 1.0),
        'c_ctx': nrm(ks[3], (D,), 1.0),
        'ada_w': nrm(ks[4], (L, D, 6 * D), 0.5 * D ** -0.5),
        'ada_b': nrm(ks[5], (L, 6 * D), 0.02),
        'g_pre_mix': 1.0 + nrm(ks[6], (L, D), 0.05),
        'g_post_mix': 1.0 + nrm(ks[7], (L, D), 0.05),
        'g_pre_ffn': 1.0 + nrm(ks[8], (L, D), 0.05),
        'g_post_ffn': 1.0 + nrm(ks[9], (L, D), 0.05),
        'w_in': nrm(ks[10], (L, D, PROJ_WIDTH), D ** -0.5),
        'w_out': nrm(ks[11], (L, MIX_WIDTH, D), MIX_WIDTH ** -0.5),
        'a_q_gain': 1.0 + nrm(ks[12], (L, HEAD_DIM), 0.05),
        'a_k_gain': 1.0 + nrm(ks[13], (L, HEAD_DIM), 0.05),
        'pool_w': nrm(ks[14], (L, len(POOL_WINDOWS), POOL_GROUP, POOL_GROUP), POOL_GROUP ** -0.5),
        'pool_scale': 1.0 + nrm(ks[15], (L, GROUP_WIDTH), 0.1),
        'lam_qk': nrm(ks[16], (L, 4, C_QK_DIM), 0.1),
        'c_subln_gain': 1.0 + nrm(ks[17], (L, HEAD_DIM), 0.05),
        'sgu_w': nrm(ks[18], (L, D_HEADS, CHUNK, CHUNK), CHUNK ** -0.5),
        'sgu_b': 1.0 + nrm(ks[19], (L, D_HEADS, CHUNK), 0.1),
        'router_w': nrm(ks[20], (L, D, N_EXPERTS), D ** -0.5),
        'router_bias': nrm(ks[21], (L, N_EXPERTS), 0.01),
        'exp_w1': nrm(ks[22], (L, N_EXPERTS, D, EXPERT_HIDDEN), D ** -0.5),
        'exp_w3': nrm(ks[23], (L, N_EXPERTS, D, EXPERT_HIDDEN), D ** -0.5),
        'exp_w2': nrm(ks[24], (L, N_EXPERTS, EXPERT_HIDDEN, D), EXPERT_HIDDEN ** -0.5),
        'sh_w1': nrm(ks[25], (L, D, SHARED_HIDDEN), D ** -0.5),
        'sh_w3': nrm(ks[26], (L, D, SHARED_HIDDEN), D ** -0.5),
        'sh_w2': nrm(ks[27], (L, SHARED_HIDDEN, D), SHARED_HIDDEN ** -0.5),
    }


def reference(x, c, ctx, c_ctx, ada_w, ada_b, g_pre_mix, g_post_mix, g_pre_ffn, g_post_ffn,
              w_in, w_out, a_q_gain, a_k_gain, pool_w, pool_scale, lam_qk, c_subln_gain,
              sgu_w, sgu_b, router_w, router_bias, exp_w1, exp_w3, exp_w2, sh_w1, sh_w3, sh_w2):
    b, s, d = x.shape
    rows = s // GRID_W
    rope_a = axial_rope_tables(rows, HEAD_DIM)
    rope_c = axial_rope_tables(rows, C_QK_DIM)
    silu_c = jax.nn.silu(c)
    silu_cc = jax.nn.silu(c_ctx)
    xl, xc = x, ctx
    for l in range(DEPTH):
        need_ctx = l < DEPTH - 1
        lam_init = 0.8 - 0.6 * math.exp(-0.3 * l)
        mod_l = jnp.split((silu_c @ ada_w[l] + ada_b[l])[:, None, :], 6, axis=-1)
        mod_c = jnp.split(silu_cc @ ada_w[l] + ada_b[l], 6, axis=-1)
        hl = modulate(rmsnorm(xl, g_pre_mix[l]), mod_l[0], mod_l[1])
        hc = modulate(rmsnorm(xc, g_pre_mix[l]), mod_c[0], mod_c[1])
        yc, yl = token_mix(hc @ w_in[l], hl @ w_in[l], a_q_gain[l], a_k_gain[l], pool_w[l], pool_scale[l],
                           lam_qk[l], c_subln_gain[l], lam_init, sgu_w[l], sgu_b[l], rope_a, rope_c, need_ctx)
        xl = xl + mod_l[2] * rmsnorm(yl @ w_out[l], g_post_mix[l])
        if need_ctx:
            xc = xc + mod_c[2] * rmsnorm(yc @ w_out[l], g_post_mix[l])
        tokens = modulate(rmsnorm(xl, g_pre_ffn[l]), mod_l[3], mod_l[4]).reshape(b * s, d)
        if need_ctx:
            hc2 = modulate(rmsnorm(xc, g_pre_ffn[l]), mod_c[3], mod_c[4]).reshape(-1, d)
            tokens = jnp.concatenate([tokens, hc2], axis=0)
        f = moe(tokens, router_w[l], router_bias[l], exp_w1[l], exp_w3[l], exp_w2[l],
                sh_w1[l], sh_w3[l], sh_w2[l])
        xl = xl + mod_l[5] * rmsnorm(f[:b * s].reshape(b, s, d), g_post_ffn[l])
        if need_ctx:
            xc = xc + mod_c[5] * rmsnorm(f[b * s:].reshape(xc.shape), g_post_ffn[l])
    return xl
```

```python
import functools
import math

import jax
import jax.numpy as jnp
from jax import lax
from jax.experimental import pallas as pl
from jax.experimental.pallas import tpu as pltpu

F32 = jnp.float32
BF16 = jnp.bfloat16

GRID_W = 64
ROPE_THETA = 10000.0
HEAD_DIM = 64
C_QK_DIM = 32
POOL_WINDOWS = (2, 4, 8, 16)
CHUNK = 128
N_EXPERTS = 64
TOP_K = 8
N_EXPERT_GROUPS = 8
TOPK_GROUPS = 4
ROUTED_SCALE = 2.5

LANES = 128
KEY_BLOCK = 256
VMEM_LIMIT = 56 * 1024 * 1024

NEG_INF = float("-inf")


def _cparams(*sem):
    return pltpu.CompilerParams(dimension_semantics=sem, vmem_limit_bytes=VMEM_LIMIT)


def _rms(x, eps=1e-6):
    return x * lax.rsqrt(jnp.mean(x * x, axis=-1, keepdims=True) + eps)


def _segsum(sq, bd):
    hi = sq.astype(BF16)
    lo = (sq - hi.astype(F32)).astype(BF16)
    return (jnp.dot(hi, bd, preferred_element_type=F32)
            + jnp.dot(lo, bd, preferred_element_type=F32))


def _rope(x, cos, sin_signed, quarter):
    w = x.shape[1]
    lane = lax.broadcasted_iota(jnp.int32, x.shape, 1)
    first = (lane & quarter) == 0
    rot = jnp.where(first, pltpu.roll(x, w - quarter, 1), pltpu.roll(x, quarter, 1))
    return x * cos + rot * sin_signed


def _silu(x):
    return x * jax.nn.sigmoid(x)


def _ada_body(c_ref, w_ref, b_ref, o_ref):
    sc = _silu(c_ref[...])
    o_ref[0] = jnp.dot(sc, w_ref[0], precision=lax.Precision.HIGHEST,
                       preferred_element_type=F32) + b_ref[0]


def _ada(cvec, ada_w, ada_b):
    nl, d, d6 = ada_w.shape
    return pl.pallas_call(
        _ada_body,
        grid=(nl, d6 // d),
        in_specs=[pl.BlockSpec((8, d), lambda l, j: (0, 0)),
                  pl.BlockSpec((1, d, d), lambda l, j: (l, 0, j)),
                  pl.BlockSpec((1, 1, d), lambda l, j: (l, 0, j))],
        out_specs=pl.BlockSpec((1, 8, d), lambda l, j: (l, 0, j)),
        out_shape=jax.ShapeDtypeStruct((nl, 8, d6), F32),
        compiler_params=_cparams("arbitrary", "arbitrary"),
        name="ada",
    )(cvec, ada_w, ada_b.reshape(nl, 1, d6))


def _inproj_body(x_ref, mod_ref, g_ref, w_ref, qg_ref, kg_ref, ca_ref, sa_ref, cc_ref, sc_ref, bd_ref,
                 qa_ref, kat_ref, va_ref, qc_ref, kct_ref, vc_ref, pdu_ref, *, rope, tm):
    x = x_ref[0]
    mod = mod_ref[0]
    h = _rms(x) * g_ref[...] * (1.0 + mod[1:2]) + mod[0:1]
    p = jnp.dot(h.astype(BF16), w_ref[...], preferred_element_type=F32)

    lane = lax.broadcasted_iota(jnp.int32, (tm, LANES), 1)
    low = lane < HEAD_DIM
    ones_col = (lane == HEAD_DIM).astype(F32)
    nkb = tm // KEY_BLOCK

    aq = p[:, 0:256]
    qn = aq * lax.rsqrt(_segsum(aq * aq, bd_ref[...]) * (1.0 / HEAD_DIM) + 1e-6) * qg_ref[...]
    if rope:
        ca = ca_ref[...]
        sa = sa_ref[...]
        qn = _rope(qn, jnp.concatenate([ca, ca], axis=1), jnp.concatenate([sa, sa], axis=1), HEAD_DIM // 4)
    qa_ref[0] = (qn * (HEAD_DIM ** -0.5)).astype(BF16)

    ak = p[:, 256:384]
    kn = ak * lax.rsqrt(_segsum(ak * ak, bd_ref[0:LANES, 0:LANES]) * (1.0 / HEAD_DIM) + 1e-6) * kg_ref[...]
    if rope:
        kn = _rope(kn, ca_ref[...], sa_ref[...], HEAD_DIM // 4)
    ksw = pltpu.roll(kn, HEAD_DIM, 1)
    kdup = (jnp.where(low, kn, ksw), jnp.where(low, ksw, kn))
    for kv in range(2):
        for j in range(nkb):
            kat_ref[0, kv, j] = kdup[kv][j * KEY_BLOCK:(j + 1) * KEY_BLOCK].T.astype(BF16)

    av = p[:, 384:512]
    va_ref[0, 0] = jnp.where(low, av, ones_col).astype(BF16)
    va_ref[0, 1] = jnp.where(low, pltpu.roll(av, HEAD_DIM, 1), ones_col).astype(BF16)

    cq = p[:, 768:1024]
    ck = p[:, 1024:1280]
    if rope:
        cc = cc_ref[...]
        sc = sc_ref[...]
        cc2 = jnp.concatenate([cc, cc], axis=1)
        sc2 = jnp.concatenate([sc, sc], axis=1)
        cq = _rope(cq, cc2, sc2, C_QK_DIM // 4)
        ck = _rope(ck, cc2, sc2, C_QK_DIM // 4)
    qc_ref[0] = (cq * (C_QK_DIM ** -0.5)).astype(BF16)
    for pr in range(2):
        for j in range(nkb):
            kct_ref[0, pr, j] = ck[j * KEY_BLOCK:(j + 1) * KEY_BLOCK, pr * LANES:(pr + 1) * LANES].T.astype(BF16)
    cv = p[:, 1280:1536]
    for hd in range(4):
        seg = cv[:, (hd // 2) * LANES:(hd // 2 + 1) * LANES]
        if hd % 2:
            seg = pltpu.roll(seg, HEAD_DIM, 1)
        vc_ref[0, hd] = jnp.where(low, seg, ones_col).astype(BF16)

    pdu_ref[0, :, 0:256] = p[:, 512:768]
    pdu_ref[0, :, 256:768] = p[:, 1536:2048]


def _inproj(x, mod, g_pre, w_in, q_gain, k_gain, tabs, bd, *, rope, tm):
    b, s, d = x.shape
    nkb = s // KEY_BLOCK
    tkb = tm // KEY_BLOCK
    full = lambda shape: pl.BlockSpec(shape, lambda bi, i: (0,) * len(shape))
    tab = pl.BlockSpec((tm, LANES), lambda bi, i: (i, 0))
    return pl.pallas_call(
        functools.partial(_inproj_body, rope=rope, tm=tm),
        grid=(b, s // tm),
        in_specs=[pl.BlockSpec((1, tm, d), lambda bi, i: (bi, i, 0)),
                  pl.BlockSpec((1, 6, d), lambda bi, i: (bi, 0, 0)),
                  full((1, d)), full(w_in.shape), full((1, 256)), full((1, LANES)),
                  tab, tab, tab, tab, full((256, 256))],
        out_specs=[pl.BlockSpec((1, tm, 256), lambda bi, i: (bi, i, 0)),
                   pl.BlockSpec((1, 2, tkb, LANES, KEY_BLOCK), lambda bi, i: (bi, 0, i, 0, 0)),
                   pl.BlockSpec((1, 2, tm, LANES), lambda bi, i: (bi, 0, i, 0)),
                   pl.BlockSpec((1, tm, 256), lambda bi, i: (bi, i, 0)),
                   pl.BlockSpec((1, 2, tkb, LANES, KEY_BLOCK), lambda bi, i: (bi, 0, i, 0, 0)),
                   pl.BlockSpec((1, 4, tm, LANES), lambda bi, i: (bi, 0, i, 0)),
                   pl.BlockSpec((1, tm, 768), lambda bi, i: (bi, i, 0))],
        out_shape=[jax.ShapeDtypeStruct((b, s, 256), BF16),
                   jax.ShapeDtypeStruct((b, 2, nkb, LANES, KEY_BLOCK), BF16),
                   jax.ShapeDtypeStruct((b, 2, s, LANES), BF16),
                   jax.ShapeDtypeStruct((b, s, 256), BF16),
                   jax.ShapeDtypeStruct((b, 2, nkb, LANES, KEY_BLOCK), BF16),
                   jax.ShapeDtypeStruct((b, 4, s, LANES), BF16),
                   jax.ShapeDtypeStruct((b, s, 768), F32)],
        compiler_params=_cparams("arbitrary", "arbitrary"),
        name="inproj",
    )(x, mod, g_pre, w_in, q_gain, k_gain, *tabs, bd)


def _flash_body(q_ref, kt_ref, v_ref, *rest, diff, tq, nchunk, bpc, lam_init):
    if diff:
        lamqk_ref, gain_ref, o_ref, m_sc, acc_sc = rest
    else:
        o_ref, m_sc, acc_sc = rest
    lane = lax.broadcasted_iota(jnp.int32, (tq, LANES), 1)
    if diff:
        base = (pl.program_id(1) % 2) * HEAD_DIM
        mask0 = (lane >= base) & (lane < base + C_QK_DIM)
        mask1 = (lane >= base + C_QK_DIM) & (lane < base + 2 * C_QK_DIM)
    else:
        mask0 = lane < HEAD_DIM
        mask1 = lane >= HEAD_DIM
    q = q_ref[0].astype(F32)
    qs = jnp.concatenate([jnp.where(mask0, q, 0.0), jnp.where(mask1, q, 0.0)], axis=0).astype(BF16)

    m_sc[...] = jnp.full(m_sc.shape, NEG_INF, F32)
    acc_sc[...] = jnp.zeros(acc_sc.shape, F32)
    tk = bpc * KEY_BLOCK

    def chunk(c, carry):
        s = jnp.concatenate(
            [jnp.dot(qs, kt_ref[0, 0, c * bpc + j], preferred_element_type=F32) for j in range(bpc)], axis=1)
        m_prev = m_sc[...]
        m_new = jnp.maximum(m_prev, jnp.max(s, axis=1, keepdims=True))
        alpha = jnp.exp(m_prev - m_new)
        p = jnp.exp(s - m_new).astype(BF16)
        v = v_ref[0, 0, pl.ds(pl.multiple_of(c * tk, KEY_BLOCK), tk), :]
        acc_sc[...] = acc_sc[...] * alpha + jnp.dot(p, v, preferred_element_type=F32)
        m_sc[...] = m_new
        return carry

    lax.fori_loop(0, nchunk, chunk, 0)

    acc = acc_sc[...]
    o = acc / acc[:, HEAD_DIM:HEAD_DIM + 1]
    o0 = o[:tq]
    o1 = o[tq:]
    if diff:
        lq = lamqk_ref[...]
        lam = (jnp.exp(jnp.sum(lq[0:1] * lq[1:2], axis=1, keepdims=True))
               - jnp.exp(jnp.sum(lq[2:3] * lq[3:4], axis=1, keepdims=True)) + lam_init)
        dlt = jnp.where(lane < HEAD_DIM, o0 - lam * o1, 0.0)
        ms = jnp.sum(dlt * dlt, axis=1, keepdims=True) * (1.0 / HEAD_DIM)
        out = dlt * lax.rsqrt(ms + 1e-6) * gain_ref[...] * (1.0 - lam_init)
    else:
        out = jnp.where(lane < HEAD_DIM, o0, pltpu.roll(o1, HEAD_DIM, 1))
    o_ref[0] = out.astype(o_ref.dtype)


def _flash(q, kt, v, *, diff, tq, bpc, lam_qk=None, gain=None, lam_init=0.0):
    b, s, _ = q.shape
    units = v.shape[1]
    nkb = kt.shape[2]
    nk = v.shape[2]
    assert nkb * KEY_BLOCK == nk and nkb % bpc == 0
    ku = (lambda u: u // 2) if diff else (lambda u: u)
    in_specs = [pl.BlockSpec((1, tq, LANES), lambda bi, u, i: (bi, i, ku(u))),
                pl.BlockSpec((1, 1, nkb, LANES, KEY_BLOCK), lambda bi, u, i: (bi, ku(u), 0, 0, 0)),
                pl.BlockSpec((1, 1, nk, LANES), lambda bi, u, i: (bi, u, 0, 0))]
    args = [q, kt, v]
    if diff:
        in_specs += [pl.BlockSpec(lam_qk.shape, lambda bi, u, i: (0, 0)),
                     pl.BlockSpec((1, LANES), lambda bi, u, i: (0, 0))]
        args += [lam_qk, gain]
    return pl.pallas_call(
        functools.partial(_flash_body, diff=diff, tq=tq, nchunk=nkb // bpc, bpc=bpc, lam_init=lam_init),
        grid=(b, units, s // tq),
        in_specs=in_specs,
        out_specs=pl.BlockSpec((1, tq, LANES), lambda bi, u, i: (bi, i, u)),
        out_shape=jax.ShapeDtypeStruct((b, s, units * LANES), BF16),
        scratch_shapes=[pltpu.VMEM((2 * tq, 1), F32), pltpu.VMEM((2 * tq, LANES), F32)],
        compiler_params=_cparams("arbitrary", "arbitrary", "arbitrary"),
        name="flash_diff" if diff else "flash_gqa",
    )(*args)


def _mixout_body(x_ref, mod_ref, ya_ref, yc_ref, pdu_ref, prev_ref, next_ref, poolw_ref, pscale_ref,
                 sguw_ref, sgub_ref, wout_ref, gpost_ref, gpre_ref, xo_ref, tok_ref, *, tm, n):
    i = pl.program_id(1)
    nt = pl.num_programs(1)
    mod = mod_ref[0]
    pdu = pdu_ref[0]
    pb = pdu[:, 0:256]
    du = pdu[:, 256:512]
    dv = pdu[:, 512:768]

    prev = jnp.where(i > 0, prev_ref[0], 0.0)
    nxt = jnp.where(i < nt - 1, next_ref[0], 0.0)
    ext = jnp.concatenate([prev, pb, nxt], axis=0)
    rows = tm + 16
    up = lambda a, k: pltpu.roll(a, rows - k, 0)
    s2 = ext + up(ext, 1)
    s4 = s2 + up(s2, 2)
    s8 = s4 + up(s4, 4)
    s16 = s8 + up(s8, 8)
    lane = lax.broadcasted_iota(jnp.int32, (tm, 256), 1)
    grp = lane // 64
    win = jnp.where(grp == 0, up(s2, 7)[0:tm],
                    jnp.where(grp == 1, up(s4, 6)[0:tm],
                              jnp.where(grp == 2, up(s8, 4)[0:tm], s16[0:tm])))
    tok_idx = i * tm + lax.broadcasted_iota(jnp.int32, (tm, 256), 0)
    half = jnp.left_shift(1, grp)
    cnt = jnp.minimum(tok_idx + half, n) - jnp.maximum(tok_idx - half, 0)
    pooled = win / cnt.astype(F32) - pb
    pool = jnp.dot(pooled.astype(BF16), poolw_ref[...], preferred_element_type=F32) * pscale_ref[...]

    mu = jnp.mean(dv, axis=1, keepdims=True)
    dc = dv - mu
    vln = (dc * lax.rsqrt(jnp.mean(dc * dc, axis=1, keepdims=True) + 1e-5)).astype(BF16)
    head = lax.broadcasted_iota(jnp.int32, (CHUNK, 256), 1) // 64
    svs = []
    for c in range(tm // CHUNK):
        vch = vln[c * CHUNK:(c + 1) * CHUNK]
        sv = sgub_ref[...]
        for hd in range(4):
            r = jnp.dot(sguw_ref[hd], vch, preferred_element_type=F32)
            sv = sv + jnp.where(head == hd, r, 0.0)
        svs.append(sv)
    sgu = du * jnp.concatenate(svs, axis=0)

    ycat = jnp.concatenate([ya_ref[0], pool.astype(BF16), yc_ref[0], sgu.astype(BF16)], axis=1)
    o = jnp.dot(ycat, wout_ref[...], preferred_element_type=F32)
    xn = x_ref[0] + mod[2:3] * (_rms(o) * gpost_ref[...])
    xo_ref[0] = xn
    tok_ref[0] = _rms(xn) * gpre_ref[...] * (1.0 + mod[4:5]) + mod[3:4]


def _mixout(x, mod, ya, yc, pdu, poolw, pscale, sguw, sgub, wout, gpost, gpre, *, tm):
    b, s, d = x.shape
    t8 = tm // 8
    last8 = s // 8 - 1
    full = lambda shape: pl.BlockSpec(shape, lambda bi, i: (0,) * len(shape))
    row = lambda w: pl.BlockSpec((1, tm, w), lambda bi, i: (bi, i, 0))
    return pl.pallas_call(
        functools.partial(_mixout_body, tm=tm, n=s),
        grid=(b, s // tm),
        in_specs=[row(d), pl.BlockSpec((1, 6, d), lambda bi, i: (bi, 0, 0)),
                  row(256), row(512), row(768),
                  pl.BlockSpec((1, 8, 256), lambda bi, i: (bi, jnp.maximum(i * t8 - 1, 0), 0)),
                  pl.BlockSpec((1, 8, 256), lambda bi, i: (bi, jnp.minimum((i + 1) * t8, last8), 0)),
                  full((256, 256)), full((1, 256)), full(sguw.shape), full((CHUNK, 256)),
                  full(wout.shape), full((1, d)), full((1, d))],
        out_specs=[row(d), row(d)],
        out_shape=[jax.ShapeDtypeStruct((b, s, d), F32), jax.ShapeDtypeStruct((b, s, d), F32)],
        compiler_params=_cparams("arbitrary", "arbitrary"),
        name="mixout",
    )(x, mod, ya, yc, pdu, pdu, pdu, poolw, pscale, sguw, sgub, wout, gpost, gpre)


def _route(sel, scores, tm):
    per = N_EXPERTS // N_EXPERT_GROUPS
    i8 = lax.broadcasted_iota(jnp.int32, (per, tm), 0)
    gsc = []
    for g in range(N_EXPERT_GROUPS):
        blk = sel[g * per:(g + 1) * per]
        m1 = jnp.max(blk, axis=0, keepdims=True)
        i1 = jnp.min(jnp.where(blk == m1, i8, per), axis=0, keepdims=True)
        m2 = jnp.max(jnp.where(i8 == i1, NEG_INF, blk), axis=0, keepdims=True)
        gsc.append(m1 + m2)
    gs = jnp.concatenate(gsc, axis=0)
    g8 = lax.broadcasted_iota(jnp.int32, (N_EXPERT_GROUPS, tm), 0)
    gmask = jnp.zeros((N_EXPERT_GROUPS, tm), F32)
    for _ in range(TOPK_GROUPS):
        gm = jnp.max(gs, axis=0, keepdims=True)
        gi = jnp.min(jnp.where(gs == gm, g8, N_EXPERT_GROUPS), axis=0, keepdims=True)
        hit = g8 == gi
        gmask = jnp.where(hit, 1.0, gmask)
        gs = jnp.where(hit, NEG_INF, gs)
    ms = jnp.concatenate(
        [jnp.where(gmask[g:g + 1] > 0.0, sel[g * per:(g + 1) * per], NEG_INF) for g in range(N_EXPERT_GROUPS)],
        axis=0)
    e64 = lax.broadcasted_iota(jnp.int32, (N_EXPERTS, tm), 0)
    chosen = jnp.zeros((N_EXPERTS, tm), F32)
    for _ in range(TOP_K):
        m = jnp.max(ms, axis=0, keepdims=True)
        ii = jnp.min(jnp.where(ms == m, e64, N_EXPERTS), axis=0, keepdims=True)
        hit = e64 == ii
        chosen = jnp.where(hit, 1.0, chosen)
        ms = jnp.where(hit, NEG_INF, ms)
    w = chosen * scores
    return w / jnp.sum(w, axis=0, keepdims=True) * ROUTED_SCALE


def _moe_body(tok_ref, x_ref, mod_ref, rw_ref, rb_ref, w1_ref, w3_ref, w2_ref, s1_ref, s3_ref, s2_ref,
              gpost_ref, o_ref, hb_sc, gates_sc, acc_sc, *, tm):
    e = pl.program_id(2)

    @pl.when(e == 0)
    def _():
        h = tok_ref[0]
        hb = h.astype(BF16)
        hb_sc[...] = hb
        logits = jnp.dot(h, rw_ref[...], precision=lax.Precision.HIGHEST, preferred_element_type=F32)
        lt = logits.T[0:N_EXPERTS]
        scores = jax.nn.sigmoid(lt)
        gates_t = _route(scores + rb_ref[...], scores, tm)
        gates_sc[...] = jnp.concatenate([gates_t, jnp.zeros_like(gates_t)], axis=0).T
        a = jnp.dot(hb, s1_ref[...], preferred_element_type=F32)
        g = jnp.dot(hb, s3_ref[...], preferred_element_type=F32)
        acc_sc[...] = jnp.dot((_silu(a) * g).astype(BF16), s2_ref[...], preferred_element_type=F32)

    hb = hb_sc[...]
    lane = lax.broadcasted_iota(jnp.int32, (tm, LANES), 1)
    gate = jnp.sum(jnp.where(lane == e, gates_sc[...], 0.0), axis=1, keepdims=True)
    a = jnp.dot(hb, w1_ref[0, 0].astype(BF16), preferred_element_type=F32)
    g = jnp.dot(hb, w3_ref[0, 0].astype(BF16), preferred_element_type=F32)
    act = (_silu(a) * g * gate).astype(BF16)
    acc_sc[...] += jnp.dot(act, w2_ref[0, 0].astype(BF16), preferred_element_type=F32)

    @pl.when(e == pl.num_programs(2) - 1)
    def _():
        mod = mod_ref[0]
        o_ref[0] = x_ref[0] + mod[5:6] * (_rms(acc_sc[...]) * gpost_ref[...])


def _moe(tok, x, mod, rw, rb, w1, w3, w2, s1, s3, s2, gpost, *, layer, tm):
    b, s, d = x.shape
    ne, _, hid = w1.shape[1:]
    full = lambda shape: pl.BlockSpec(shape, lambda bi, i, e: (0,) * len(shape))
    row = pl.BlockSpec((1, tm, d), lambda bi, i, e: (bi, i, 0))
    return pl.pallas_call(
        functools.partial(_moe_body, tm=tm),
        grid=(b, s // tm, ne),
        in_specs=[row, row, pl.BlockSpec((1, 6, d), lambda bi, i, e: (bi, 0, 0)),
                  full((d, LANES)), full((N_EXPERTS, 1)),
                  pl.BlockSpec((1, 1, d, hid), lambda bi, i, e: (layer, e, 0, 0)),
                  pl.BlockSpec((1, 1, d, hid), lambda bi, i, e: (layer, e, 0, 0)),
                  pl.BlockSpec((1, 1, hid, d), lambda bi, i, e: (layer, e, 0, 0)),
                  full(s1.shape), full(s3.shape), full(s2.shape), full((1, d))],
        out_specs=row,
        out_shape=jax.ShapeDtypeStruct((b, s, d), F32),
        scratch_shapes=[pltpu.VMEM((tm, d), BF16), pltpu.VMEM((tm, LANES), F32), pltpu.VMEM((tm, d), F32)],
        compiler_params=_cparams("arbitrary", "arbitrary", "arbitrary"),
        name="moe",
    )(tok, x, mod, rw, rb, w1, w3, w2, s1, s3, s2, gpost)


def _rope_tables(s, dim):
    rows = s // GRID_W
    row = jnp.repeat(jnp.arange(rows, dtype=F32), GRID_W)
    col = jnp.tile(jnp.arange(GRID_W, dtype=F32), rows)
    half = dim // 2
    inv = ROPE_THETA ** (-jnp.arange(0, half, 2, dtype=F32) / half)
    ar = row[:, None] * inv[None, :]
    ac = col[:, None] * inv[None, :]
    ang = jnp.concatenate([ar, ar, ac, ac], axis=-1)
    sign = jnp.where((jnp.arange(dim) & (dim // 4)) == 0, -1.0, 1.0).astype(F32)
    reps = LANES // dim
    return jnp.tile(jnp.cos(ang), (1, reps)), jnp.tile(jnp.sin(ang) * sign, (1, reps))


def _block_diag(blocks):
    n = len(blocks)
    r, c = blocks[0].shape
    out = jnp.zeros((n * r, n * c), blocks[0].dtype)
    for i, blk in enumerate(blocks):
        out = out.at[i * r:(i + 1) * r, i * c:(i + 1) * c].set(blk)
    return out


def kernel(x, c, ctx, c_ctx, ada_w, ada_b, g_pre_mix, g_post_mix, g_pre_ffn, g_post_ffn, w_in, w_out, a_q_gain, a_k_gain, pool_w, pool_scale, lam_qk, c_subln_gain, sgu_w, sgu_b, router_w, router_bias, exp_w1, exp_w3, exp_w2, sh_w1, sh_w3, sh_w2):
    b, s, d = x.shape
    nctx = ctx.shape[1]
    depth = ada_w.shape[0]
    tm_lat = 512
    tq = 256
    tm_moe = 1024

    cvec = jnp.zeros((8, d), F32).at[0:b].set(c).at[b].set(c_ctx)
    mods = _ada(cvec, ada_w, ada_b)

    tabs = _rope_tables(s, HEAD_DIM) + _rope_tables(s, C_QK_DIM)
    bd = _block_diag([jnp.ones((HEAD_DIM, HEAD_DIM), BF16)] * 4)
    row2 = lambda v: v.reshape(1, -1)

    xl, xc = x, ctx
    for l in range(depth):
        need_ctx = l < depth - 1
        lam_init = 0.8 - 0.6 * math.exp(-0.3 * l)
        m6 = mods[l].reshape(8, 6, d)
        mod_l = m6[0:b]
        mod_c = jnp.broadcast_to(m6[b:b + 1], (b, 6, d))

        w_in_l = w_in[l].astype(BF16)
        qg = jnp.tile(a_q_gain[l], 4).reshape(1, 256)
        kg = jnp.tile(a_k_gain[l], 2).reshape(1, LANES)
        inproj = functools.partial(_inproj, g_pre=row2(g_pre_mix[l]), w_in=w_in_l, q_gain=qg, k_gain=kg,
                                   tabs=tabs, bd=bd)
        qa_l, kat_l, va_l, qc_l, kct_l, vc_l, pdu_l = inproj(xl, mod_l, rope=True, tm=tm_lat)
        qa_c, kat_c, va_c, qc_c, kct_c, vc_c, pdu_c = inproj(xc, mod_c, rope=False, tm=nctx)

        kat = jnp.concatenate([kat_c, kat_l], axis=2)
        va = jnp.concatenate([va_c, va_l], axis=2)
        kct = jnp.concatenate([kct_c, kct_l], axis=2)
        vc = jnp.concatenate([vc_c, vc_l], axis=2)
        sub_gain = jnp.concatenate([c_subln_gain[l], jnp.zeros((LANES - HEAD_DIM,), F32)]).reshape(1, LANES)
        nkb = kat.shape[2]
        bpc = 5 if nkb % 5 == 0 else 1

        ya_l = _flash(qa_l, kat, va, diff=False, tq=tq, bpc=bpc)
        yc_l = _flash(qc_l, kct, vc, diff=True, tq=tq, bpc=bpc, lam_qk=lam_qk[l], gain=sub_gain,
                      lam_init=lam_init)

        poolw = _block_diag([pool_w[l, g] for g in range(len(POOL_WINDOWS))]).astype(BF16)
        sgub = jnp.repeat(jnp.transpose(sgu_b[l]), d // 16, axis=1)
        wo = w_out[l]
        wo_c = jnp.pad(wo[512:768].reshape(4, HEAD_DIM, d), ((0, 0), (0, LANES - HEAD_DIM), (0, 0)))
        wout = jnp.concatenate([wo[0:512], wo_c.reshape(4 * LANES, d), wo[768:1024]], axis=0).astype(BF16)
        mixout = functools.partial(_mixout, poolw=poolw, pscale=row2(pool_scale[l]), sguw=sgu_w[l].astype(BF16),
                                   sgub=sgub, wout=wout, gpost=row2(g_post_mix[l]), gpre=row2(g_pre_ffn[l]))
        rw = jnp.pad(router_w[l], ((0, 0), (0, LANES - N_EXPERTS)))
        moe = functools.partial(_moe, rw=rw, rb=router_bias[l].reshape(N_EXPERTS, 1), w1=exp_w1, w3=exp_w3,
                                w2=exp_w2, s1=sh_w1[l].astype(BF16), s3=sh_w3[l].astype(BF16),
                                s2=sh_w2[l].astype(BF16), gpost=row2(g_post_ffn[l]), layer=l)

        xl_mid, tok_l = mixout(xl, mod_l, ya_l, yc_l, pdu_l, tm=tm_lat)
        xl = moe(tok_l, xl_mid, mod_l, tm=tm_moe)
        if need_ctx:
            ya_c = _flash(qa_c, kat_c, va_c, diff=False, tq=nctx, bpc=1)
            yc_c = _flash(qc_c, kct_c, vc_c, diff=True, tq=nctx, bpc=1, lam_qk=lam_qk[l], gain=sub_gain,
                          lam_init=lam_init)
            xc_mid, tok_c = mixout(xc, mod_c, ya_c, yc_c, pdu_c, tm=nctx)
            xc = moe(tok_c, xc_mid, mod_c, tm=nctx)
    return xl
```

```python
import functools
import math

import jax
import jax.numpy as jnp
from jax import lax
from jax.experimental import pallas as pl
from jax.experimental.pallas import tpu as pltpu

F32 = jnp.float32
BF16 = jnp.bfloat16

GRID_W = 64
ROPE_THETA = 10000.0
HEAD_DIM = 64
C_QK_DIM = 32
POOL_WINDOWS = (2, 4, 8, 16)
CHUNK = 128
N_EXPERTS = 64
TOP_K = 8
N_EXPERT_GROUPS = 8
TOPK_GROUPS = 4
ROUTED_SCALE = 2.5

LANES = 128
KEY_BLOCK = 256
V_ROWS = 80
VMEM_LIMIT = 56 * 1024 * 1024

NEG_INF = float("-inf")
LOG2E = math.log2(math.e)


def _cparams(*sem):
    return pltpu.CompilerParams(dimension_semantics=sem, vmem_limit_bytes=VMEM_LIMIT)


def _rms(x, eps=1e-6):
    return x * lax.rsqrt(jnp.mean(x * x, axis=-1, keepdims=True) + eps)


def _segsum(sq, bd):
    hi = sq.astype(BF16)
    lo = (sq - hi.astype(F32)).astype(BF16)
    return (jnp.dot(hi, bd, preferred_element_type=F32)
            + jnp.dot(lo, bd, preferred_element_type=F32))


def _rope(x, cos, sin_signed, quarter):
    w = x.shape[1]
    lane = lax.broadcasted_iota(jnp.int32, x.shape, 1)
    first = (lane & quarter) == 0
    rot = jnp.where(first, pltpu.roll(x, w - quarter, 1), pltpu.roll(x, quarter, 1))
    return x * cos + rot * sin_signed


def _silu(x):
    return x * jax.nn.sigmoid(x)


def _ada_body(c_ref, w_ref, b_ref, o_ref):
    sc = _silu(c_ref[...])
    o_ref[0] = jnp.dot(sc, w_ref[0], precision=lax.Precision.HIGHEST,
                       preferred_element_type=F32) + b_ref[0]


def _ada(cvec, ada_w, ada_b):
    nl, d, d6 = ada_w.shape
    return pl.pallas_call(
        _ada_body,
        grid=(nl, d6 // d),
        in_specs=[pl.BlockSpec((8, d), lambda l, j: (0, 0)),
                  pl.BlockSpec((1, d, d), lambda l, j: (l, 0, j)),
                  pl.BlockSpec((1, 1, d), lambda l, j: (l, 0, j))],
        out_specs=pl.BlockSpec((1, 8, d), lambda l, j: (l, 0, j)),
        out_shape=jax.ShapeDtypeStruct((nl, 8, d6), F32),
        compiler_params=_cparams("arbitrary", "arbitrary"),
        name="ada",
    )(cvec, ada_w, ada_b.reshape(nl, 1, d6))


def _inproj_body(x_ref, mod_ref, g_ref, w_ref, qg_ref, kg_ref, ca_ref, sa_ref, cc_ref, sc_ref, bd_ref,
                 qat_ref, ka_ref, vat_ref, qct_ref, kc_ref, vct_ref, pdu_ref, *, rope, tm):
    x = x_ref[0]
    mod = mod_ref[0]
    h = _rms(x) * g_ref[...] * (1.0 + mod[1:2]) + mod[0:1]
    p = jnp.dot(h.astype(BF16), w_ref[...], preferred_element_type=F32)

    lane = lax.broadcasted_iota(jnp.int32, (tm, LANES), 1)
    low = lane < HEAD_DIM
    ones_col = (lane == HEAD_DIM).astype(F32)
    nkb = tm // KEY_BLOCK

    aq = p[:, 0:256]
    qn = aq * lax.rsqrt(_segsum(aq * aq, bd_ref[...]) * (1.0 / HEAD_DIM) + 1e-6) * qg_ref[...]
    if rope:
        ca = ca_ref[...]
        sa = sa_ref[...]
        qn = _rope(qn, jnp.concatenate([ca, ca], axis=1), jnp.concatenate([sa, sa], axis=1), HEAD_DIM // 4)
    qn = qn * (HEAD_DIM ** -0.5 * LOG2E)
    for kv in range(2):
        qat_ref[0, kv] = qn[:, kv * LANES:(kv + 1) * LANES].T.astype(BF16)

    ak = p[:, 256:384]
    kn = ak * lax.rsqrt(_segsum(ak * ak, bd_ref[0:LANES, 0:LANES]) * (1.0 / HEAD_DIM) + 1e-6) * kg_ref[...]
    if rope:
        kn = _rope(kn, ca_ref[...], sa_ref[...], HEAD_DIM // 4)
    ksw = pltpu.roll(kn, HEAD_DIM, 1)
    ka_ref[0, 0] = jnp.where(low, kn, ksw).astype(BF16)
    ka_ref[0, 1] = jnp.where(low, ksw, kn).astype(BF16)

    def store_vt(ref, unit, vext):
        for j in range(nkb):
            ref[0, unit, j] = vext[j * KEY_BLOCK:(j + 1) * KEY_BLOCK].T[0:V_ROWS].astype(BF16)

    av = p[:, 384:512]
    store_vt(vat_ref, 0, jnp.where(low, av, ones_col))
    store_vt(vat_ref, 1, jnp.where(low, pltpu.roll(av, HEAD_DIM, 1), ones_col))

    cq = p[:, 768:1024]
    ck = p[:, 1024:1280]
    if rope:
        cc = cc_ref[...]
        sc = sc_ref[...]
        cc2 = jnp.concatenate([cc, cc], axis=1)
        sc2 = jnp.concatenate([sc, sc], axis=1)
        cq = _rope(cq, cc2, sc2, C_QK_DIM // 4)
        ck = _rope(ck, cc2, sc2, C_QK_DIM // 4)
    cq = cq * (C_QK_DIM ** -0.5 * LOG2E)
    for pr in range(2):
        qct_ref[0, pr] = cq[:, pr * LANES:(pr + 1) * LANES].T.astype(BF16)
        kc_ref[0, pr] = ck[:, pr * LANES:(pr + 1) * LANES].astype(BF16)
    cv = p[:, 1280:1536]
    for hd in range(4):
        seg = cv[:, (hd // 2) * LANES:(hd // 2 + 1) * LANES]
        if hd % 2:
            seg = pltpu.roll(seg, HEAD_DIM, 1)
        store_vt(vct_ref, hd, jnp.where(low, seg, ones_col))

    pdu_ref[0, :, 0:256] = p[:, 512:768]
    pdu_ref[0, :, 256:768] = p[:, 1536:2048]


def _inproj(x, mod, g_pre, w_in, q_gain, k_gain, tabs, bd, *, rope, tm):
    b, s, d = x.shape
    nkb = s // KEY_BLOCK
    tkb = tm // KEY_BLOCK
    full = lambda shape: pl.BlockSpec(shape, lambda bi, i: (0,) * len(shape))
    tab = pl.BlockSpec((tm, LANES), lambda bi, i: (i, 0))
    return pl.pallas_call(
        functools.partial(_inproj_body, rope=rope, tm=tm),
        grid=(b, s // tm),
        in_specs=[pl.BlockSpec((1, tm, d), lambda bi, i: (bi, i, 0)),
                  pl.BlockSpec((1, 6, d), lambda bi, i: (bi, 0, 0)),
                  full((1, d)), full(w_in.shape), full((1, 256)), full((1, LANES)),
                  tab, tab, tab, tab, full((256, 256))],
        out_specs=[pl.BlockSpec((1, 2, LANES, tm), lambda bi, i: (bi, 0, 0, i)),
                   pl.BlockSpec((1, 2, tm, LANES), lambda bi, i: (bi, 0, i, 0)),
                   pl.BlockSpec((1, 2, tkb, V_ROWS, KEY_BLOCK), lambda bi, i: (bi, 0, i, 0, 0)),
                   pl.BlockSpec((1, 2, LANES, tm), lambda bi, i: (bi, 0, 0, i)),
                   pl.BlockSpec((1, 2, tm, LANES), lambda bi, i: (bi, 0, i, 0)),
                   pl.BlockSpec((1, 4, tkb, V_ROWS, KEY_BLOCK), lambda bi, i: (bi, 0, i, 0, 0)),
                   pl.BlockSpec((1, tm, 768), lambda bi, i: (bi, i, 0))],
        out_shape=[jax.ShapeDtypeStruct((b, 2, LANES, s), BF16),
                   jax.ShapeDtypeStruct((b, 2, s, LANES), BF16),
                   jax.ShapeDtypeStruct((b, 2, nkb, V_ROWS, KEY_BLOCK), BF16),
                   jax.ShapeDtypeStruct((b, 2, LANES, s), BF16),
                   jax.ShapeDtypeStruct((b, 2, s, LANES), BF16),
                   jax.ShapeDtypeStruct((b, 4, nkb, V_ROWS, KEY_BLOCK), BF16),
                   jax.ShapeDtypeStruct((b, s, 768), F32)],
        compiler_params=_cparams("arbitrary", "arbitrary"),
        name="inproj",
    )(x, mod, g_pre, w_in, q_gain, k_gain, *tabs, bd)


def _flash_body(qt_ref, k_ref, vt_ref, *rest, diff, tq, nkb, lam_init):
    if diff:
        lamqk_ref, gain_ref, o_ref, s0_sc, s1_sc, p0_sc, p1_sc, a0_sc, a1_sc, b0_sc, b1_sc, m_sc, acc_sc = rest
    else:
        o_ref, s0_sc, s1_sc, p0_sc, p1_sc, a0_sc, a1_sc, b0_sc, b1_sc, m_sc, acc_sc = rest
    s_sc = (s0_sc, s1_sc)
    p_sc = (p0_sc, p1_sc)
    a_sc = (a0_sc, a1_sc)
    b_sc = (b0_sc, b1_sc)
    row =lax.broadcasted_iota(jnp.int32, (LANES, tq), 0)
    if diff:
        base = (pl.program_id(1) % 2) * HEAD_DIM
        mask0 = (row >= base) & (row < base + C_QK_DIM)
        mask1 = (row >= base + C_QK_DIM) & (row < base + 2 * C_QK_DIM)
    else:
        mask0 = row < HEAD_DIM
        mask1 = row >= HEAD_DIM
    qt = qt_ref[0, 0].astype(F32)
    qst = jnp.concatenate([jnp.where(mask0, qt, 0.0), jnp.where(mask1, qt, 0.0)], axis=1).astype(BF16)

    m_sc[...] = jnp.full(m_sc.shape, NEG_INF, F32)
    acc_sc[...] = jnp.zeros(acc_sc.shape, F32)
    for slot in range(2):
        p_sc[slot][...] = jnp.zeros(p_sc[slot].shape, BF16)
        a_sc[slot][...] = jnp.ones(a_sc[slot].shape, F32)
    last = nkb - 1

    def values(j, slot):
        pv = jnp.dot(vt_ref[0, 0, j], p_sc[slot][...], preferred_element_type=F32)
        acc_sc[...] = acc_sc[...] * a_sc[slot][...] + pv

    def softmax(slot):
        m_prev = m_sc[...]
        m_new = jnp.maximum(m_prev, b_sc[slot][...])
        a_sc[slot][...] = jnp.exp2(m_prev - m_new)
        p_sc[slot][...] = jnp.exp2(s_sc[slot][...] - m_new).astype(BF16)
        m_sc[...] = m_new

    def scores(j, slot):
        k = k_ref[0, 0, pl.ds(pl.multiple_of(j * KEY_BLOCK, KEY_BLOCK), KEY_BLOCK), :]
        s = jnp.dot(k, qst, preferred_element_type=F32)
        s_sc[slot][...] = s
        b_sc[slot][...] = jnp.max(s, axis=0, keepdims=True)

    def step(j, slot):
        values(jnp.maximum(j - 2, 0), slot)
        softmax(slot)
        scores(jnp.minimum(j + 2, last), slot)

    scores(0, 0)
    scores(jnp.minimum(1, last), 1)

    def pair(i, carry):
        step(2 * i, 0)
        step(2 * i + 1, 1)
        return carry

    lax.fori_loop(0, last // 2, pair, 0)
    step(last, 0)
    if last >= 1:
        values(last - 1, 1)
    values(last, 0)

    acc = acc_sc[...]
    o = acc[0:HEAD_DIM] / acc[HEAD_DIM:HEAD_DIM + 1]
    o0 = o[:, :tq]
    o1 = o[:, tq:]
    if diff:
        lq = lamqk_ref[...]
        lam = (jnp.exp(jnp.sum(lq[0:1] * lq[1:2], axis=1, keepdims=True))
               - jnp.exp(jnp.sum(lq[2:3] * lq[3:4], axis=1, keepdims=True)) + lam_init)
        dlt = o0 - lam * o1
        ms = jnp.mean(dlt * dlt, axis=0, keepdims=True)
        y = dlt * lax.rsqrt(ms + 1e-6) * gain_ref[...] * (1.0 - lam_init)
        out_t = jnp.concatenate([y, jnp.zeros_like(y)], axis=0)
    else:
        out_t = jnp.concatenate([o0, o1], axis=0)
    o_ref[0] = out_t.T.astype(o_ref.dtype)


def _flash(qt, k, vt, *, diff, tq, lam_qk=None, gain=None, lam_init=0.0):
    b, _, _, s = qt.shape
    units, nkb = vt.shape[1:3]
    nk = k.shape[2]
    assert nkb * KEY_BLOCK == nk and nkb % 2 == 1
    n = 2 * tq
    ku = (lambda u: u // 2) if diff else (lambda u: u)
    in_specs = [pl.BlockSpec((1, 1, LANES, tq), lambda bi, u, i: (bi, ku(u), 0, i)),
                pl.BlockSpec((1, 1, nk, LANES), lambda bi, u, i: (bi, ku(u), 0, 0)),
                pl.BlockSpec((1, 1, nkb, V_ROWS, KEY_BLOCK), lambda bi, u, i: (bi, u, 0, 0, 0))]
    args = [qt, k, vt]
    if diff:
        in_specs += [pl.BlockSpec(lam_qk.shape, lambda bi, u, i: (0, 0)),
                     pl.BlockSpec((HEAD_DIM, 1), lambda bi, u, i: (0, 0))]
        args += [lam_qk, gain]
    return pl.pallas_call(
        functools.partial(_flash_body, diff=diff, tq=tq, nkb=nkb, lam_init=lam_init),
        grid=(b, units, s // tq),
        in_specs=in_specs,
        out_specs=pl.BlockSpec((1, tq, LANES), lambda bi, u, i: (bi, i, u)),
        out_shape=jax.ShapeDtypeStruct((b, s, units * LANES), BF16),
        scratch_shapes=[pltpu.VMEM((KEY_BLOCK, n), F32), pltpu.VMEM((KEY_BLOCK, n), F32),
                        pltpu.VMEM((KEY_BLOCK, n), BF16), pltpu.VMEM((KEY_BLOCK, n), BF16),
                        pltpu.VMEM((1, n), F32), pltpu.VMEM((1, n), F32),
                        pltpu.VMEM((1, n), F32), pltpu.VMEM((1, n), F32),
                        pltpu.VMEM((1, n), F32), pltpu.VMEM((V_ROWS, n), F32)],
        compiler_params=_cparams("arbitrary", "arbitrary", "arbitrary"),
        name="flash_diff" if diff else "flash_gqa",
    )(*args)


def _mixout_body(x_ref, mod_ref, ya_ref, yc_ref, pdu_ref, prev_ref, next_ref, poolw_ref, pscale_ref,
                 sguw_ref, sgub_ref, wout_ref, gpost_ref, gpre_ref, xo_ref, tok_ref, *, tm, n):
    i = pl.program_id(1)
    nt = pl.num_programs(1)
    mod = mod_ref[0]
    pdu = pdu_ref[0]
    pb = pdu[:, 0:256]
    du = pdu[:, 256:512]
    dv = pdu[:, 512:768]

    prev = jnp.where(i > 0, prev_ref[0], 0.0)
    nxt = jnp.where(i < nt - 1, next_ref[0], 0.0)
    ext = jnp.concatenate([prev, pb, nxt], axis=0)
    rows = tm + 16
    up = lambda a, k: pltpu.roll(a, rows - k, 0)
    s2 = ext + up(ext, 1)
    s4 = s2 + up(s2, 2)
    s8 = s4 + up(s4, 4)
    s16 = s8 + up(s8, 8)
    lane = lax.broadcasted_iota(jnp.int32, (tm, 256), 1)
    grp = lane // 64
    win = jnp.where(grp == 0, up(s2, 7)[0:tm],
                    jnp.where(grp == 1, up(s4, 6)[0:tm],
                              jnp.where(grp == 2, up(s8, 4)[0:tm], s16[0:tm])))
    tok_idx = i * tm + lax.broadcasted_iota(jnp.int32, (tm, 256), 0)
    half = jnp.left_shift(1, grp)
    cnt = jnp.minimum(tok_idx + half, n) - jnp.maximum(tok_idx - half, 0)
    pooled = win / cnt.astype(F32) - pb
    pool = jnp.dot(pooled.astype(BF16), poolw_ref[...], preferred_element_type=F32) * pscale_ref[...]

    mu = jnp.mean(dv, axis=1, keepdims=True)
    dc = dv - mu
    vln = (dc * lax.rsqrt(jnp.mean(dc * dc, axis=1, keepdims=True) + 1e-5)).astype(BF16)
    head = lax.broadcasted_iota(jnp.int32, (CHUNK, 256), 1) // 64
    svs = []
    for c in range(tm // CHUNK):
        vch = vln[c * CHUNK:(c + 1) * CHUNK]
        sv = sgub_ref[...]
        for hd in range(4):
            r = jnp.dot(sguw_ref[hd], vch, preferred_element_type=F32)
            sv = sv + jnp.where(head == hd, r, 0.0)
        svs.append(sv)
    sgu = du * jnp.concatenate(svs, axis=0)

    ycat = jnp.concatenate([ya_ref[0], pool.astype(BF16), yc_ref[0], sgu.astype(BF16)], axis=1)
    o = jnp.dot(ycat, wout_ref[...], preferred_element_type=F32)
    xn = x_ref[0] + mod[2:3] * (_rms(o) * gpost_ref[...])
    xo_ref[0] = xn
    tok_ref[0] = _rms(xn) * gpre_ref[...] * (1.0 + mod[4:5]) + mod[3:4]


def _mixout(x, mod, ya, yc, pdu, poolw, pscale, sguw, sgub, wout, gpost, gpre, *, tm):
    b, s, d = x.shape
    t8 = tm // 8
    last8 = s // 8 - 1
    full = lambda shape: pl.BlockSpec(shape, lambda bi, i: (0,) * len(shape))
    row = lambda w: pl.BlockSpec((1, tm, w), lambda bi, i: (bi, i, 0))
    return pl.pallas_call(
        functools.partial(_mixout_body, tm=tm, n=s),
        grid=(b, s // tm),
        in_specs=[row(d), pl.BlockSpec((1, 6, d), lambda bi, i: (bi, 0, 0)),
                  row(256), row(512), row(768),
                  pl.BlockSpec((1, 8, 256), lambda bi, i: (bi, jnp.maximum(i * t8 - 1, 0), 0)),
                  pl.BlockSpec((1, 8, 256), lambda bi, i: (bi, jnp.minimum((i + 1) * t8, last8), 0)),
                  full((256, 256)), full((1, 256)), full(sguw.shape), full((CHUNK, 256)),
                  full(wout.shape), full((1, d)), full((1, d))],
        out_specs=[row(d), row(d)],
        out_shape=[jax.ShapeDtypeStruct((b, s, d), F32), jax.ShapeDtypeStruct((b, s, d), F32)],
        compiler_params=_cparams("arbitrary", "arbitrary"),
        name="mixout",
    )(x, mod, ya, yc, pdu, pdu, pdu, poolw, pscale, sguw, sgub, wout, gpost, gpre)


def _route(sel, scores, tm):
    per = N_EXPERTS // N_EXPERT_GROUPS
    i8 = lax.broadcasted_iota(jnp.int32, (per, tm), 0)
    gsc = []
    for g in range(N_EXPERT_GROUPS):
        blk = sel[g * per:(g + 1) * per]
        m1 = jnp.max(blk, axis=0, keepdims=True)
        i1 = jnp.min(jnp.where(blk == m1, i8, per), axis=0, keepdims=True)
        m2 = jnp.max(jnp.where(i8 == i1, NEG_INF, blk), axis=0, keepdims=True)
        gsc.append(m1 + m2)
    gs = jnp.concatenate(gsc, axis=0)
    g8 = lax.broadcasted_iota(jnp.int32, (N_EXPERT_GROUPS, tm), 0)
    gmask = jnp.zeros((N_EXPERT_GROUPS, tm), F32)
    for _ in range(TOPK_GROUPS):
        gm = jnp.max(gs, axis=0, keepdims=True)
        gi = jnp.min(jnp.where(gs == gm, g8, N_EXPERT_GROUPS), axis=0, keepdims=True)
        hit = g8 == gi
        gmask = jnp.where(hit, 1.0, gmask)
        gs = jnp.where(hit, NEG_INF, gs)
    ms = jnp.concatenate(
        [jnp.where(gmask[g:g + 1] > 0.0, sel[g * per:(g + 1) * per], NEG_INF) for g in range(N_EXPERT_GROUPS)],
        axis=0)
    e64 = lax.broadcasted_iota(jnp.int32, (N_EXPERTS, tm), 0)
    chosen = jnp.zeros((N_EXPERTS, tm), F32)
    for _ in range(TOP_K):
        m = jnp.max(ms, axis=0, keepdims=True)
        ii = jnp.min(jnp.where(ms == m, e64, N_EXPERTS), axis=0, keepdims=True)
        hit = e64 == ii
        chosen = jnp.where(hit, 1.0, chosen)
        ms = jnp.where(hit, NEG_INF, ms)
    w = chosen * scores
    return w / jnp.sum(w, axis=0, keepdims=True) * ROUTED_SCALE


def _moe_body(tok_ref, x_ref, mod_ref, rw_ref, rb_ref, w1_ref, w3_ref, w2_ref, s1_ref, s3_ref, s2_ref,
              gpost_ref, o_ref, hb_sc, gates_sc, acc_sc, *, tm):
    e = pl.program_id(2)

    @pl.when(e == 0)
    def _():
        h = tok_ref[0]
        hb = h.astype(BF16)
        hb_sc[...] = hb
        logits = jnp.dot(h, rw_ref[...], precision=lax.Precision.HIGHEST, preferred_element_type=F32)
        lt = logits.T[0:N_EXPERTS]
        scores = jax.nn.sigmoid(lt)
        gates_t = _route(scores + rb_ref[...], scores, tm)
        gates_sc[...] = jnp.concatenate([gates_t, jnp.zeros_like(gates_t)], axis=0).T
        a = jnp.dot(hb, s1_ref[...], preferred_element_type=F32)
        g = jnp.dot(hb, s3_ref[...], preferred_element_type=F32)
        acc_sc[...] = jnp.dot((_silu(a) * g).astype(BF16), s2_ref[...], preferred_element_type=F32)

    hb = hb_sc[...]
    lane = lax.broadcasted_iota(jnp.int32, (tm, LANES), 1)
    gate = jnp.sum(jnp.where(lane == e, gates_sc[...], 0.0), axis=1, keepdims=True)
    a = jnp.dot(hb, w1_ref[0, 0].astype(BF16), preferred_element_type=F32)
    g = jnp.dot(hb, w3_ref[0, 0].astype(BF16), preferred_element_type=F32)
    act = (_silu(a) * g * gate).astype(BF16)
    acc_sc[...] += jnp.dot(act, w2_ref[0, 0].astype(BF16), preferred_element_type=F32)

    @pl.when(e == pl.num_programs(2) - 1)
    def _():
        mod = mod_ref[0]
        o_ref[0] = x_ref[0] + mod[5:6] * (_rms(acc_sc[...]) * gpost_ref[...])


def _moe(tok, x, mod, rw, rb, w1, w3, w2, s1, s3, s2, gpost, *, layer, tm):
    b, s, d = x.shape
    ne, _, hid = w1.shape[1:]
    full = lambda shape: pl.BlockSpec(shape, lambda bi, i, e: (0,) * len(shape))
    row = pl.BlockSpec((1, tm, d), lambda bi, i, e: (bi, i, 0))
    return pl.pallas_call(
        functools.partial(_moe_body, tm=tm),
        grid=(b, s // tm, ne),
        in_specs=[row, row, pl.BlockSpec((1, 6, d), lambda bi, i, e: (bi, 0, 0)),
                  full((d, LANES)), full((N_EXPERTS, 1)),
                  pl.BlockSpec((1, 1, d, hid), lambda bi, i, e: (layer, e, 0, 0)),
                  pl.BlockSpec((1, 1, d, hid), lambda bi, i, e: (layer, e, 0, 0)),
                  pl.BlockSpec((1, 1, hid, d), lambda bi, i, e: (layer, e, 0, 0)),
                  full(s1.shape), full(s3.shape), full(s2.shape), full((1, d))],
        out_specs=row,
        out_shape=jax.ShapeDtypeStruct((b, s, d), F32),
        scratch_shapes=[pltpu.VMEM((tm, d), BF16), pltpu.VMEM((tm, LANES), F32), pltpu.VMEM((tm, d), F32)],
        compiler_params=_cparams("arbitrary", "arbitrary", "arbitrary"),
        name="moe",
    )(tok, x, mod, rw, rb, w1, w3, w2, s1, s3, s2, gpost)


def _rope_tables(s, dim):
    rows = s // GRID_W
    row = jnp.repeat(jnp.arange(rows, dtype=F32), GRID_W)
    col = jnp.tile(jnp.arange(GRID_W, dtype=F32), rows)
    half = dim // 2
    inv = ROPE_THETA ** (-jnp.arange(0, half, 2, dtype=F32) / half)
    ar = row[:, None] * inv[None, :]
    ac = col[:, None] * inv[None, :]
    ang = jnp.concatenate([ar, ar, ac, ac], axis=-1)
    sign = jnp.where((jnp.arange(dim) & (dim // 4)) == 0, -1.0, 1.0).astype(F32)
    reps = LANES // dim
    return jnp.tile(jnp.cos(ang), (1, reps)), jnp.tile(jnp.sin(ang) * sign, (1, reps))


def _block_diag(blocks):
    n = len(blocks)
    r, c = blocks[0].shape
    out = jnp.zeros((n * r, n * c), blocks[0].dtype)
    for i, blk in enumerate(blocks):
        out = out.at[i * r:(i + 1) * r, i * c:(i + 1) * c].set(blk)
    return out


def kernel(x, c, ctx, c_ctx, ada_w, ada_b, g_pre_mix, g_post_mix, g_pre_ffn, g_post_ffn, w_in, w_out, a_q_gain, a_k_gain, pool_w, pool_scale, lam_qk, c_subln_gain, sgu_w, sgu_b, router_w, router_bias, exp_w1, exp_w3, exp_w2, sh_w1, sh_w3, sh_w2):
    b, s, d = x.shape
    nctx = ctx.shape[1]
    depth = ada_w.shape[0]
    tm_lat = 512
    tq = 1024
    tm_moe = 1024

    cvec = jnp.zeros((8, d), F32).at[0:b].set(c).at[b].set(c_ctx)
    mods = _ada(cvec, ada_w, ada_b)

    tabs = _rope_tables(s, HEAD_DIM) + _rope_tables(s, C_QK_DIM)
    bd = _block_diag([jnp.ones((HEAD_DIM, HEAD_DIM), BF16)] * 4)
    row2 = lambda v: v.reshape(1, -1)

    xl, xc = x, ctx
    for l in range(depth):
        need_ctx = l < depth - 1
        lam_init = 0.8 - 0.6 * math.exp(-0.3 * l)
        m6 = mods[l].reshape(8, 6, d)
        mod_l = m6[0:b]
        mod_c = jnp.broadcast_to(m6[b:b + 1], (b, 6, d))

        w_in_l = w_in[l].astype(BF16)
        qg = jnp.tile(a_q_gain[l], 4).reshape(1, 256)
        kg = jnp.tile(a_k_gain[l], 2).reshape(1, LANES)
        inproj = functools.partial(_inproj, g_pre=row2(g_pre_mix[l]), w_in=w_in_l, q_gain=qg, k_gain=kg,
                                   tabs=tabs, bd=bd)
        qat_l, ka_l, vat_l, qct_l, kc_l, vct_l, pdu_l = inproj(xl, mod_l, rope=True, tm=tm_lat)
        qat_c, ka_c, vat_c, qct_c, kc_c, vct_c, pdu_c = inproj(xc, mod_c, rope=False, tm=nctx)

        ka = jnp.concatenate([ka_c, ka_l], axis=2)
        vat = jnp.concatenate([vat_c, vat_l], axis=2)
        kc = jnp.concatenate([kc_c, kc_l], axis=2)
        vct = jnp.concatenate([vct_c, vct_l], axis=2)
        sub_gain = c_subln_gain[l].reshape(HEAD_DIM, 1)

        ya_l = _flash(qat_l, ka, vat, diff=False, tq=tq)
        yc_l = _flash(qct_l, kc, vct, diff=True, tq=tq, lam_qk=lam_qk[l], gain=sub_gain,
                      lam_init=lam_init)

        poolw = _block_diag([pool_w[l, g] for g in range(len(POOL_WINDOWS))]).astype(BF16)
        sgub = jnp.repeat(jnp.transpose(sgu_b[l]), d // 16, axis=1)
        wo = w_out[l]
        wo_c = jnp.pad(wo[512:768].reshape(4, HEAD_DIM, d), ((0, 0), (0, LANES - HEAD_DIM), (0, 0)))
        wout = jnp.concatenate([wo[0:512], wo_c.reshape(4 * LANES, d), wo[768:1024]], axis=0).astype(BF16)
        mixout = functools.partial(_mixout, poolw=poolw, pscale=row2(pool_scale[l]), sguw=sgu_w[l].astype(BF16),
                                   sgub=sgub, wout=wout, gpost=row2(g_post_mix[l]), gpre=row2(g_pre_ffn[l]))
        rw = jnp.pad(router_w[l], ((0, 0), (0, LANES - N_EXPERTS)))
        moe = functools.partial(_moe, rw=rw, rb=router_bias[l].reshape(N_EXPERTS, 1), w1=exp_w1, w3=exp_w3,
                                w2=exp_w2, s1=sh_w1[l].astype(BF16), s3=sh_w3[l].astype(BF16),
                                s2=sh_w2[l].astype(BF16), gpost=row2(g_post_ffn[l]), layer=l)

        xl_mid, tok_l = mixout(xl, mod_l, ya_l, yc_l, pdu_l, tm=tm_lat)
        xl = moe(tok_l, xl_mid, mod_l, tm=tm_moe)
        if need_ctx:
            ya_c = _flash(qat_c, ka_c, vat_c, diff=False, tq=nctx)
            yc_c = _flash(qct_c, kc_c, vct_c, diff=True, tq=nctx, lam_qk=lam_qk[l], gain=sub_gain,
                          lam_init=lam_init)
            xc_mid, tok_c = mixout(xc, mod_c, ya_c, yc_c, pdu_c, tm=nctx)
            xc = moe(tok_c, xc_mid, mod_c, tm=nctx)
    return xl
```

```python
import functools
import math

import jax
import jax.numpy as jnp
from jax import lax
from jax.experimental import pallas as pl
from jax.experimental.pallas import tpu as pltpu

F32 = jnp.float32
BF16 = jnp.bfloat16

GRID_W = 64
ROPE_THETA = 10000.0
HEAD_DIM = 64
C_QK_DIM = 32
POOL_WINDOWS = (2, 4, 8, 16)
CHUNK = 128
N_EXPERTS = 64
TOP_K = 8
N_EXPERT_GROUPS = 8
TOPK_GROUPS = 4
ROUTED_SCALE = 2.5
EXPERTS_PER_STEP = 2

LANES = 128
KEY_BLOCK = 256
PIPE_SETS = 2
STEPS_PER_TRIP = 16
V_ROWS = 80
VMEM_LIMIT = 56 * 1024 * 1024

NEG_INF = float("-inf")
LOG2E = math.log2(math.e)


def _cparams(*sem):
    return pltpu.CompilerParams(dimension_semantics=sem, vmem_limit_bytes=VMEM_LIMIT)


def _rms(x, eps=1e-6):
    return x * lax.rsqrt(jnp.mean(x * x, axis=-1, keepdims=True) + eps)


def _segsum(sq, bd):
    hi = sq.astype(BF16)
    lo = (sq - hi.astype(F32)).astype(BF16)
    return (jnp.dot(hi, bd, preferred_element_type=F32)
            + jnp.dot(lo, bd, preferred_element_type=F32))


def _rope(x, cos, sin_signed, quarter):
    w = x.shape[1]
    lane = lax.broadcasted_iota(jnp.int32, x.shape, 1)
    first = (lane & quarter) == 0
    rot = jnp.where(first, pltpu.roll(x, w - quarter, 1), pltpu.roll(x, quarter, 1))
    return x * cos + rot * sin_signed


def _silu(x):
    return x * jax.nn.sigmoid(x)


def _ada_body(c_ref, w_ref, b_ref, o_ref):
    sc = _silu(c_ref[...])
    o_ref[0] = jnp.dot(sc, w_ref[0], precision=lax.Precision.HIGHEST,
                       preferred_element_type=F32) + b_ref[0]


def _ada(cvec, ada_w, ada_b):
    nl, d, d6 = ada_w.shape
    return pl.pallas_call(
        _ada_body,
        grid=(nl, d6 // d),
        in_specs=[pl.BlockSpec((8, d), lambda l, j: (0, 0)),
                  pl.BlockSpec((1, d, d), lambda l, j: (l, 0, j)),
                  pl.BlockSpec((1, 1, d), lambda l, j: (l, 0, j))],
        out_specs=pl.BlockSpec((1, 8, d), lambda l, j: (l, 0, j)),
        out_shape=jax.ShapeDtypeStruct((nl, 8, d6), F32),
        compiler_params=_cparams("arbitrary", "arbitrary"),
        name="ada",
    )(cvec, ada_w, ada_b.reshape(nl, 1, d6))


def _inproj_body(x_ref, mod_ref, g_ref, w_ref, qg_ref, kg_ref, ca_ref, sa_ref, cc_ref, sc_ref, bd_ref,
                 qat_ref, ka_ref, vat_ref, qct_ref, kc_ref, vct_ref, pdu_ref, *, rope, tm):
    x = x_ref[0]
    mod = mod_ref[0]
    h = _rms(x) * g_ref[...] * (1.0 + mod[1:2]) + mod[0:1]
    p = jnp.dot(h.astype(BF16), w_ref[...], preferred_element_type=F32)

    lane = lax.broadcasted_iota(jnp.int32, (tm, LANES), 1)
    low = lane < HEAD_DIM
    ones_col = (lane == HEAD_DIM).astype(F32)
    nkb = tm // KEY_BLOCK

    aq = p[:, 0:256]
    qn = aq * lax.rsqrt(_segsum(aq * aq, bd_ref[...]) * (1.0 / HEAD_DIM) + 1e-6) * qg_ref[...]
    if rope:
        ca = ca_ref[...]
        sa = sa_ref[...]
        qn = _rope(qn, jnp.concatenate([ca, ca], axis=1), jnp.concatenate([sa, sa], axis=1), HEAD_DIM // 4)
    qn = qn * (HEAD_DIM ** -0.5 * LOG2E)
    for kv in range(2):
        qat_ref[0, kv] = qn[:, kv * LANES:(kv + 1) * LANES].T.astype(BF16)

    ak = p[:, 256:384]
    kn = ak * lax.rsqrt(_segsum(ak * ak, bd_ref[0:LANES, 0:LANES]) * (1.0 / HEAD_DIM) + 1e-6) * kg_ref[...]
    if rope:
        kn = _rope(kn, ca_ref[...], sa_ref[...], HEAD_DIM // 4)
    ksw = pltpu.roll(kn, HEAD_DIM, 1)
    ka_ref[0, 0] = jnp.where(low, kn, ksw).astype(BF16)
    ka_ref[0, 1] = jnp.where(low, ksw, kn).astype(BF16)

    def store_vt(ref, unit, vext):
        for j in range(nkb):
            ref[0, unit, j] = vext[j * KEY_BLOCK:(j + 1) * KEY_BLOCK].T[0:V_ROWS].astype(BF16)

    av = p[:, 384:512]
    store_vt(vat_ref, 0, jnp.where(low, av, ones_col))
    store_vt(vat_ref, 1, jnp.where(low, pltpu.roll(av, HEAD_DIM, 1), ones_col))

    cq = p[:, 768:1024]
    ck = p[:, 1024:1280]
    if rope:
        cc = cc_ref[...]
        sc = sc_ref[...]
        cc2 = jnp.concatenate([cc, cc], axis=1)
        sc2 = jnp.concatenate([sc, sc], axis=1)
        cq = _rope(cq, cc2, sc2, C_QK_DIM // 4)
        ck = _rope(ck, cc2, sc2, C_QK_DIM // 4)
    cq = cq * (C_QK_DIM ** -0.5 * LOG2E)
    for pr in range(2):
        qct_ref[0, pr] = cq[:, pr * LANES:(pr + 1) * LANES].T.astype(BF16)
        kc_ref[0, pr] = ck[:, pr * LANES:(pr + 1) * LANES].astype(BF16)
    cv = p[:, 1280:1536]
    for hd in range(4):
        seg = cv[:, (hd // 2) * LANES:(hd // 2 + 1) * LANES]
        if hd % 2:
            seg = pltpu.roll(seg, HEAD_DIM, 1)
        store_vt(vct_ref, hd, jnp.where(low, seg, ones_col))

    pdu_ref[0, :, 0:256] = p[:, 512:768]
    pdu_ref[0, :, 256:768] = p[:, 1536:2048]


def _inproj(x, mod, g_pre, w_in, q_gain, k_gain, tabs, bd, *, rope, tm):
    b, s, d = x.shape
    nkb = s // KEY_BLOCK
    tkb = tm // KEY_BLOCK
    full = lambda shape: pl.BlockSpec(shape, lambda bi, i: (0,) * len(shape))
    tab = pl.BlockSpec((tm, LANES), lambda bi, i: (i, 0))
    return pl.pallas_call(
        functools.partial(_inproj_body, rope=rope, tm=tm),
        grid=(b, s // tm),
        in_specs=[pl.BlockSpec((1, tm, d), lambda bi, i: (bi, i, 0)),
                  pl.BlockSpec((1, 6, d), lambda bi, i: (bi, 0, 0)),
                  full((1, d)), full(w_in.shape), full((1, 256)), full((1, LANES)),
                  tab, tab, tab, tab, full((256, 256))],
        out_specs=[pl.BlockSpec((1, 2, LANES, tm), lambda bi, i: (bi, 0, 0, i)),
                   pl.BlockSpec((1, 2, tm, LANES), lambda bi, i: (bi, 0, i, 0)),
                   pl.BlockSpec((1, 2, tkb, V_ROWS, KEY_BLOCK), lambda bi, i: (bi, 0, i, 0, 0)),
                   pl.BlockSpec((1, 2, LANES, tm), lambda bi, i: (bi, 0, 0, i)),
                   pl.BlockSpec((1, 2, tm, LANES), lambda bi, i: (bi, 0, i, 0)),
                   pl.BlockSpec((1, 4, tkb, V_ROWS, KEY_BLOCK), lambda bi, i: (bi, 0, i, 0, 0)),
                   pl.BlockSpec((1, tm, 768), lambda bi, i: (bi, i, 0))],
        out_shape=[jax.ShapeDtypeStruct((b, 2, LANES, s), BF16),
                   jax.ShapeDtypeStruct((b, 2, s, LANES), BF16),
                   jax.ShapeDtypeStruct((b, 2, nkb, V_ROWS, KEY_BLOCK), BF16),
                   jax.ShapeDtypeStruct((b, 2, LANES, s), BF16),
                   jax.ShapeDtypeStruct((b, 2, s, LANES), BF16),
                   jax.ShapeDtypeStruct((b, 4, nkb, V_ROWS, KEY_BLOCK), BF16),
                   jax.ShapeDtypeStruct((b, s, 768), F32)],
        compiler_params=_cparams("arbitrary", "arbitrary"),
        name="inproj",
    )(x, mod, g_pre, w_in, q_gain, k_gain, *tabs, bd)


def _flash_body(qt_ref, k_ref, vt_ref, *rest, diff, tq, nkb, lam_init):
    if diff:
        lamqk_ref, gain_ref, o_ref, s_sc, p_sc, a_sc, b_sc, m_sc, acc_sc = rest
    else:
        o_ref, s_sc, p_sc, a_sc, b_sc, m_sc, acc_sc = rest
    row =lax.broadcasted_iota(jnp.int32, (LANES, tq), 0)
    if diff:
        base = (pl.program_id(1) % 2) * HEAD_DIM
        mask0 = (row >= base) & (row < base + C_QK_DIM)
        mask1 = (row >= base + C_QK_DIM) & (row < base + 2 * C_QK_DIM)
    else:
        mask0 = row < HEAD_DIM
        mask1 = row >= HEAD_DIM
    qt = qt_ref[0, 0].astype(F32)
    qst = jnp.concatenate([jnp.where(mask0, qt, 0.0), jnp.where(mask1, qt, 0.0)], axis=1).astype(BF16)

    m_sc[...] = jnp.full(m_sc.shape, NEG_INF, F32)
    acc_sc[...] = jnp.zeros(acc_sc.shape, F32)
    for slot in range(PIPE_SETS):
        p_sc[slot] = jnp.zeros(p_sc.shape[1:], BF16)
        a_sc[slot] = jnp.ones(a_sc.shape[1:], F32)
    last = nkb - 1

    def scores(j, slot):
        k = k_ref[0, 0, pl.ds(pl.multiple_of(j * KEY_BLOCK, KEY_BLOCK), KEY_BLOCK), :]
        s = jnp.dot(k, qst, preferred_element_type=F32)
        s_sc[slot] = s
        b_sc[slot] = jnp.max(s, axis=0, keepdims=True)

    def softmax(slot):
        m_prev = m_sc[...]
        m_new = jnp.maximum(m_prev, b_sc[slot])
        a_sc[slot] = jnp.exp2(m_prev - m_new)
        p_sc[slot] = jnp.exp2((s_sc[slot] - m_new).astype(BF16))
        m_sc[...] = m_new

    def values(j, slot):
        pv = jnp.dot(vt_ref[0, 0, j], p_sc[slot], preferred_element_type=F32)
        acc_sc[...] = acc_sc[...] * a_sc[slot] + pv

    def step(j, slot, prefetch=True):
        values(jnp.maximum(j - PIPE_SETS, 0), slot)
        softmax(slot)
        if prefetch:
            scores(jnp.minimum(j + PIPE_SETS, last), slot)

    scores(0, 0)
    scores(jnp.minimum(1, last), 1)

    def trip(i, carry):
        for r in range(STEPS_PER_TRIP):
            step(STEPS_PER_TRIP * i + r, r % PIPE_SETS)
        return carry

    lax.fori_loop(0, last // STEPS_PER_TRIP, trip, 0)
    step(last, 0, prefetch=False)
    if last >= 1:
        values(last - 1, 1)
    values(last, 0)

    acc = acc_sc[...]
    o = acc[0:HEAD_DIM] / acc[HEAD_DIM:HEAD_DIM + 1]
    o0 = o[:, :tq]
    o1 = o[:, tq:]
    if diff:
        lq = lamqk_ref[...]
        lam = (jnp.exp(jnp.sum(lq[0:1] * lq[1:2], axis=1, keepdims=True))
               - jnp.exp(jnp.sum(lq[2:3] * lq[3:4], axis=1, keepdims=True)) + lam_init)
        dlt = o0 - lam * o1
        ms = jnp.mean(dlt * dlt, axis=0, keepdims=True)
        y = dlt * lax.rsqrt(ms + 1e-6) * gain_ref[...] * (1.0 - lam_init)
        out_t = jnp.concatenate([y, jnp.zeros_like(y)], axis=0)
    else:
        out_t = jnp.concatenate([o0, o1], axis=0)
    o_ref[0] = out_t.T.astype(o_ref.dtype)


def _flash(qt, k, vt, *, diff, tq, lam_qk=None, gain=None, lam_init=0.0):
    b, _, _, s = qt.shape
    units, nkb = vt.shape[1:3]
    nk = k.shape[2]
    assert nkb * KEY_BLOCK == nk and nkb % STEPS_PER_TRIP == 1
    n = 2 * tq
    ku = (lambda u: u // 2) if diff else (lambda u: u)
    in_specs = [pl.BlockSpec((1, 1, LANES, tq), lambda bi, u, i: (bi, ku(u), 0, i)),
                pl.BlockSpec((1, 1, nk, LANES), lambda bi, u, i: (bi, ku(u), 0, 0)),
                pl.BlockSpec((1, 1, nkb, V_ROWS, KEY_BLOCK), lambda bi, u, i: (bi, u, 0, 0, 0))]
    args = [qt, k, vt]
    if diff:
        in_specs += [pl.BlockSpec(lam_qk.shape, lambda bi, u, i: (0, 0)),
                     pl.BlockSpec((HEAD_DIM, 1), lambda bi, u, i: (0, 0))]
        args += [lam_qk, gain]
    return pl.pallas_call(
        functools.partial(_flash_body, diff=diff, tq=tq, nkb=nkb, lam_init=lam_init),
        grid=(b, units, s // tq),
        in_specs=in_specs,
        out_specs=pl.BlockSpec((1, tq, LANES), lambda bi, u, i: (bi, i, u)),
        out_shape=jax.ShapeDtypeStruct((b, s, units * LANES), BF16),
        scratch_shapes=[pltpu.VMEM((PIPE_SETS, KEY_BLOCK, n), F32), pltpu.VMEM((PIPE_SETS, KEY_BLOCK, n), BF16),
                        pltpu.VMEM((PIPE_SETS, 1, n), F32), pltpu.VMEM((PIPE_SETS, 1, n), F32),
                        pltpu.VMEM((1, n), F32), pltpu.VMEM((V_ROWS, n), F32)],
        compiler_params=_cparams("arbitrary", "arbitrary", "arbitrary"),
        name="flash_diff" if diff else "flash_gqa",
    )(*args)


def _mixout_body(x_ref, mod_ref, ya_ref, yc_ref, pdu_ref, prev_ref, next_ref, poolw_ref, pscale_ref,
                 sguw_ref, sgub_ref, wout_ref, gpost_ref, gpre_ref, xo_ref, tok_ref, *, tm, n):
    i = pl.program_id(1)
    nt = pl.num_programs(1)
    mod = mod_ref[0]
    pdu = pdu_ref[0]
    pb = pdu[:, 0:256]
    du = pdu[:, 256:512]
    dv = pdu[:, 512:768]

    prev = jnp.where(i > 0, prev_ref[0], 0.0)
    nxt = jnp.where(i < nt - 1, next_ref[0], 0.0)
    ext = jnp.concatenate([prev, pb, nxt], axis=0)
    rows = tm + 16
    up = lambda a, k: pltpu.roll(a, rows - k, 0)
    s2 = ext + up(ext, 1)
    s4 = s2 + up(s2, 2)
    s8 = s4 + up(s4, 4)
    s16 = s8 + up(s8, 8)
    lane = lax.broadcasted_iota(jnp.int32, (tm, 256), 1)
    grp = lane // 64
    win = jnp.where(grp == 0, up(s2, 7)[0:tm],
                    jnp.where(grp == 1, up(s4, 6)[0:tm],
                              jnp.where(grp == 2, up(s8, 4)[0:tm], s16[0:tm])))
    tok_idx = i * tm + lax.broadcasted_iota(jnp.int32, (tm, 256), 0)
    half = jnp.left_shift(1, grp)
    cnt = jnp.minimum(tok_idx + half, n) - jnp.maximum(tok_idx - half, 0)
    pooled = win / cnt.astype(F32) - pb
    pool = jnp.dot(pooled.astype(BF16), poolw_ref[...], preferred_element_type=F32) * pscale_ref[...]

    mu = jnp.mean(dv, axis=1, keepdims=True)
    dc = dv - mu
    vln = (dc * lax.rsqrt(jnp.mean(dc * dc, axis=1, keepdims=True) + 1e-5)).astype(BF16)
    head = lax.broadcasted_iota(jnp.int32, (CHUNK, 256), 1) // 64
    svs = []
    for c in range(tm // CHUNK):
        vch = vln[c * CHUNK:(c + 1) * CHUNK]
        sv = sgub_ref[...]
        for hd in range(4):
            r = jnp.dot(sguw_ref[hd], vch, preferred_element_type=F32)
            sv = sv + jnp.where(head == hd, r, 0.0)
        svs.append(sv)
    sgu = du * jnp.concatenate(svs, axis=0)

    ycat = jnp.concatenate([ya_ref[0], pool.astype(BF16), yc_ref[0], sgu.astype(BF16)], axis=1)
    o = jnp.dot(ycat, wout_ref[...], preferred_element_type=F32)
    xn = x_ref[0] + mod[2:3] * (_rms(o) * gpost_ref[...])
    xo_ref[0] = xn
    tok_ref[0] = _rms(xn) * gpre_ref[...] * (1.0 + mod[4:5]) + mod[3:4]


def _mixout(x, mod, ya, yc, pdu, poolw, pscale, sguw, sgub, wout, gpost, gpre, *, tm):
    b, s, d = x.shape
    t8 = tm // 8
    last8 = s // 8 - 1
    full = lambda shape: pl.BlockSpec(shape, lambda bi, i: (0,) * len(shape))
    row = lambda w: pl.BlockSpec((1, tm, w), lambda bi, i: (bi, i, 0))
    return pl.pallas_call(
        functools.partial(_mixout_body, tm=tm, n=s),
        grid=(b, s // tm),
        in_specs=[row(d), pl.BlockSpec((1, 6, d), lambda bi, i: (bi, 0, 0)),
                  row(256), row(512), row(768),
                  pl.BlockSpec((1, 8, 256), lambda bi, i: (bi, jnp.maximum(i * t8 - 1, 0), 0)),
                  pl.BlockSpec((1, 8, 256), lambda bi, i: (bi, jnp.minimum((i + 1) * t8, last8), 0)),
                  full((256, 256)), full((1, 256)), full(sguw.shape), full((CHUNK, 256)),
                  full(wout.shape), full((1, d)), full((1, d))],
        out_specs=[row(d), row(d)],
        out_shape=[jax.ShapeDtypeStruct((b, s, d), F32), jax.ShapeDtypeStruct((b, s, d), F32)],
        compiler_params=_cparams("arbitrary", "arbitrary"),
        name="mixout",
    )(x, mod, ya, yc, pdu, pdu, pdu, poolw, pscale, sguw, sgub, wout, gpost, gpre)


def _route(sel, scores, tm):
    per = N_EXPERTS // N_EXPERT_GROUPS
    i8 = lax.broadcasted_iota(jnp.int32, (per, tm), 0)
    gsc = []
    for g in range(N_EXPERT_GROUPS):
        blk = sel[g * per:(g + 1) * per]
        m1 = jnp.max(blk, axis=0, keepdims=True)
        i1 = jnp.min(jnp.where(blk == m1, i8, per), axis=0, keepdims=True)
        m2 = jnp.max(jnp.where(i8 == i1, NEG_INF, blk), axis=0, keepdims=True)
        gsc.append(m1 + m2)
    gs = jnp.concatenate(gsc, axis=0)
    g8 = lax.broadcasted_iota(jnp.int32, (N_EXPERT_GROUPS, tm), 0)
    gmask = jnp.zeros((N_EXPERT_GROUPS, tm), F32)
    for _ in range(TOPK_GROUPS):
        gm = jnp.max(gs, axis=0, keepdims=True)
        gi = jnp.min(jnp.where(gs == gm, g8, N_EXPERT_GROUPS), axis=0, keepdims=True)
        hit = g8 == gi
        gmask = jnp.where(hit, 1.0, gmask)
        gs = jnp.where(hit, NEG_INF, gs)
    ms = jnp.concatenate(
        [jnp.where(gmask[g:g + 1] > 0.0, sel[g * per:(g + 1) * per], NEG_INF) for g in range(N_EXPERT_GROUPS)],
        axis=0)
    e64 = lax.broadcasted_iota(jnp.int32, (N_EXPERTS, tm), 0)
    chosen = jnp.zeros((N_EXPERTS, tm), F32)
    for _ in range(TOP_K):
        m = jnp.max(ms, axis=0, keepdims=True)
        ii = jnp.min(jnp.where(ms == m, e64, N_EXPERTS), axis=0, keepdims=True)
        hit = e64 == ii
        chosen = jnp.where(hit, 1.0, chosen)
        ms = jnp.where(hit, NEG_INF, ms)
    w = chosen * scores
    return w / jnp.sum(w, axis=0, keepdims=True) * ROUTED_SCALE


def _moe_body(tok_ref, x_ref, mod_ref, rw_ref, rb_ref, w1_ref, w3_ref, w2_ref, s1_ref, s3_ref, s2_ref,
              gpost_ref, o_ref, hb_sc, gates_sc, acc_sc, *, tm):
    e = pl.program_id(2)

    @pl.when(e == 0)
    def _():
        h = tok_ref[0]
        hb = h.astype(BF16)
        hb_sc[...] = hb
        logits = jnp.dot(h, rw_ref[...], precision=lax.Precision.HIGHEST, preferred_element_type=F32)
        lt = logits.T[0:N_EXPERTS]
        scores = jax.nn.sigmoid(lt)
        gates_t = _route(scores + rb_ref[...], scores, tm)
        gates_sc[...] = jnp.concatenate([gates_t, jnp.zeros_like(gates_t)], axis=0).T
        a = jnp.dot(hb, s1_ref[...], preferred_element_type=F32)
        g = jnp.dot(hb, s3_ref[...], preferred_element_type=F32)
        acc_sc[...] = jnp.dot((_silu(a) * g).astype(BF16), s2_ref[...], preferred_element_type=F32)

    hb = hb_sc[...]
    lane = lax.broadcasted_iota(jnp.int32, (tm, LANES), 1)
    gates = gates_sc[...]
    acts = []
    for i in range(EXPERTS_PER_STEP):
        gate = jnp.sum(jnp.where(lane == e * EXPERTS_PER_STEP + i, gates, 0.0), axis=1, keepdims=True)
        a = jnp.dot(hb, w1_ref[0, i].astype(BF16), preferred_element_type=F32)
        g = jnp.dot(hb, w3_ref[0, i].astype(BF16), preferred_element_type=F32)
        acts.append((_silu(a) * g * gate).astype(BF16))
    hid = w2_ref.shape[2]
    w2 = w2_ref[0].reshape(EXPERTS_PER_STEP * hid, w2_ref.shape[3]).astype(BF16)
    acc_sc[...] += jnp.dot(jnp.concatenate(acts, axis=1), w2, preferred_element_type=F32)

    @pl.when(e == pl.num_programs(2) - 1)
    def _():
        mod = mod_ref[0]
        o_ref[0] = x_ref[0] + mod[5:6] * (_rms(acc_sc[...]) * gpost_ref[...])


def _moe(tok, x, mod, rw, rb, w1, w3, w2, s1, s3, s2, gpost, *, layer, tm):
    b, s, d = x.shape
    ne, _, hid = w1.shape[1:]
    full = lambda shape: pl.BlockSpec(shape, lambda bi, i, e: (0,) * len(shape))
    row = pl.BlockSpec((1, tm, d), lambda bi, i, e: (bi, i, 0))
    return pl.pallas_call(
        functools.partial(_moe_body, tm=tm),
        grid=(b, s // tm, ne // EXPERTS_PER_STEP),
        in_specs=[row, row, pl.BlockSpec((1, 6, d), lambda bi, i, e: (bi, 0, 0)),
                  full((d, LANES)), full((N_EXPERTS, 1)),
                  pl.BlockSpec((1, EXPERTS_PER_STEP, d, hid), lambda bi, i, e: (layer, e, 0, 0)),
                  pl.BlockSpec((1, EXPERTS_PER_STEP, d, hid), lambda bi, i, e: (layer, e, 0, 0)),
                  pl.BlockSpec((1, EXPERTS_PER_STEP, hid, d), lambda bi, i, e: (layer, e, 0, 0)),
                  full(s1.shape), full(s3.shape), full(s2.shape), full((1, d))],
        out_specs=row,
        out_shape=jax.ShapeDtypeStruct((b, s, d), F32),
        scratch_shapes=[pltpu.VMEM((tm, d), BF16), pltpu.VMEM((tm, LANES), F32), pltpu.VMEM((tm, d), F32)],
        compiler_params=_cparams("arbitrary", "arbitrary", "arbitrary"),
        name="moe",
    )(tok, x, mod, rw, rb, w1, w3, w2, s1, s3, s2, gpost)


def _rope_tables(s, dim):
    rows = s // GRID_W
    row = jnp.repeat(jnp.arange(rows, dtype=F32), GRID_W)
    col = jnp.tile(jnp.arange(GRID_W, dtype=F32), rows)
    half = dim // 2
    inv = ROPE_THETA ** (-jnp.arange(0, half, 2, dtype=F32) / half)
    ar = row[:, None] * inv[None, :]
    ac = col[:, None] * inv[None, :]
    ang = jnp.concatenate([ar, ar, ac, ac], axis=-1)
    sign = jnp.where((jnp.arange(dim) & (dim // 4)) == 0, -1.0, 1.0).astype(F32)
    reps = LANES // dim
    return jnp.tile(jnp.cos(ang), (1, reps)), jnp.tile(jnp.sin(ang) * sign, (1, reps))


def _block_diag(blocks):
    n = len(blocks)
    r, c = blocks[0].shape
    out = jnp.zeros((n * r, n * c), blocks[0].dtype)
    for i, blk in enumerate(blocks):
        out = out.at[i * r:(i + 1) * r, i * c:(i + 1) * c].set(blk)
    return out


def kernel(x, c, ctx, c_ctx, ada_w, ada_b, g_pre_mix, g_post_mix, g_pre_ffn, g_post_ffn, w_in, w_out, a_q_gain, a_k_gain, pool_w, pool_scale, lam_qk, c_subln_gain, sgu_w, sgu_b, router_w, router_bias, exp_w1, exp_w3, exp_w2, sh_w1, sh_w3, sh_w2):
    b, s, d = x.shape
    nctx = ctx.shape[1]
    depth = ada_w.shape[0]
    tm_lat = 512
    tq = 512
    tm_moe = 1024

    cvec = jnp.zeros((8, d), F32).at[0:b].set(c).at[b].set(c_ctx)
    mods = _ada(cvec, ada_w, ada_b)

    tabs = _rope_tables(s, HEAD_DIM) + _rope_tables(s, C_QK_DIM)
    bd = _block_diag([jnp.ones((HEAD_DIM, HEAD_DIM), BF16)] * 4)
    row2 = lambda v: v.reshape(1, -1)

    xl, xc = x, ctx
    for l in range(depth):
        need_ctx = l < depth - 1
        lam_init = 0.8 - 0.6 * math.exp(-0.3 * l)
        m6 = mods[l].reshape(8, 6, d)
        mod_l = m6[0:b]
        mod_c = jnp.broadcast_to(m6[b:b + 1], (b, 6, d))

        w_in_l = w_in[l].astype(BF16)
        qg = jnp.tile(a_q_gain[l], 4).reshape(1, 256)
        kg = jnp.tile(a_k_gain[l], 2).reshape(1, LANES)
        inproj = functools.partial(_inproj, g_pre=row2(g_pre_mix[l]), w_in=w_in_l, q_gain=qg, k_gain=kg,
                                   tabs=tabs, bd=bd)
        qat_l, ka_l, vat_l, qct_l, kc_l, vct_l, pdu_l = inproj(xl, mod_l, rope=True, tm=tm_lat)
        qat_c, ka_c, vat_c, qct_c, kc_c, vct_c, pdu_c = inproj(xc, mod_c, rope=False, tm=nctx)

        ka = jnp.concatenate([ka_c, ka_l], axis=2)
        vat = jnp.concatenate([vat_c, vat_l], axis=2)
        kc = jnp.concatenate([kc_c, kc_l], axis=2)
        vct = jnp.concatenate([vct_c, vct_l], axis=2)
        sub_gain = c_subln_gain[l].reshape(HEAD_DIM, 1)

        ya_l = _flash(qat_l, ka, vat, diff=False, tq=tq)
        yc_l = _flash(qct_l, kc, vct, diff=True, tq=tq, lam_qk=lam_qk[l], gain=sub_gain,
                      lam_init=lam_init)

        poolw = _block_diag([pool_w[l, g] for g in range(len(POOL_WINDOWS))]).astype(BF16)
        sgub = jnp.repeat(jnp.transpose(sgu_b[l]), d // 16, axis=1)
        wo = w_out[l]
        wo_c = jnp.pad(wo[512:768].reshape(4, HEAD_DIM, d), ((0, 0), (0, LANES - HEAD_DIM), (0, 0)))
        wout = jnp.concatenate([wo[0:512], wo_c.reshape(4 * LANES, d), wo[768:1024]], axis=0).astype(BF16)
        mixout = functools.partial(_mixout, poolw=poolw, pscale=row2(pool_scale[l]), sguw=sgu_w[l].astype(BF16),
                                   sgub=sgub, wout=wout, gpost=row2(g_post_mix[l]), gpre=row2(g_pre_ffn[l]))
        rw = jnp.pad(router_w[l], ((0, 0), (0, LANES - N_EXPERTS)))
        moe = functools.partial(_moe, rw=rw, rb=router_bias[l].reshape(N_EXPERTS, 1), w1=exp_w1, w3=exp_w3,
                                w2=exp_w2, s1=sh_w1[l].astype(BF16), s3=sh_w3[l].astype(BF16),
                                s2=sh_w2[l].astype(BF16), gpost=row2(g_post_ffn[l]), layer=l)

        xl_mid, tok_l = mixout(xl, mod_l, ya_l, yc_l, pdu_l, tm=tm_lat)
        xl = moe(tok_l, xl_mid, mod_l, tm=tm_moe)
        if need_ctx:
            ya_c = _flash(qat_c, ka_c, vat_c, diff=False, tq=nctx)
            yc_c = _flash(qct_c, kc_c, vct_c, diff=True, tq=nctx, lam_qk=lam_qk[l], gain=sub_gain,
                          lam_init=lam_init)
            xc_mid, tok_c = mixout(xc, mod_c, ya_c, yc_c, pdu_c, tm=nctx)
            xc = moe(tok_c, xc_mid, mod_c, tm=nctx)
    return xl
```

```python
import functools
import math

import jax
import jax.numpy as jnp
from jax import lax
from jax.experimental import pallas as pl
from jax.experimental.pallas import tpu as pltpu

F32 = jnp.float32
BF16 = jnp.bfloat16

GRID_W = 64
ROPE_THETA = 10000.0
HEAD_DIM = 64
C_QK_DIM = 32
POOL_WINDOWS = (2, 4, 8, 16)
CHUNK = 128
N_EXPERTS = 64
TOP_K = 8
N_EXPERT_GROUPS = 8
TOPK_GROUPS = 4
ROUTED_SCALE = 2.5
EXPERTS_PER_STEP = 2

LANES = 128
KEY_BLOCK = 256
PIPE_SETS = 2
STEPS_PER_TRIP = 16
V_ROWS = 80
VMEM_LIMIT = 56 * 1024 * 1024

NEG_INF = float("-inf")
LOG2E = math.log2(math.e)


def _cparams(*sem, flags=None):
    return pltpu.CompilerParams(dimension_semantics=sem, vmem_limit_bytes=VMEM_LIMIT, flags=flags)


def _rms(x, eps=1e-6):
    return x * lax.rsqrt(jnp.mean(x * x, axis=-1, keepdims=True) + eps)


def _segsum(sq, bd):
    hi = sq.astype(BF16)
    lo = (sq - hi.astype(F32)).astype(BF16)
    return (jnp.dot(hi, bd, preferred_element_type=F32)
            + jnp.dot(lo, bd, preferred_element_type=F32))


def _rope(x, cos, sin_signed, quarter):
    w = x.shape[1]
    lane = lax.broadcasted_iota(jnp.int32, x.shape, 1)
    first = (lane & quarter) == 0
    rot = jnp.where(first, pltpu.roll(x, w - quarter, 1), pltpu.roll(x, quarter, 1))
    return x * cos + rot * sin_signed


def _silu(x):
    return x * jax.nn.sigmoid(x)


def _ada_body(c_ref, w_ref, b_ref, o_ref):
    sc = _silu(c_ref[...])
    o_ref[0] = jnp.dot(sc, w_ref[0], precision=lax.Precision.HIGHEST,
                       preferred_element_type=F32) + b_ref[0]


def _ada(cvec, ada_w, ada_b):
    nl, d, d6 = ada_w.shape
    return pl.pallas_call(
        _ada_body,
        grid=(nl, d6 // d),
        in_specs=[pl.BlockSpec((8, d), lambda l, j: (0, 0)),
                  pl.BlockSpec((1, d, d), lambda l, j: (l, 0, j)),
                  pl.BlockSpec((1, 1, d), lambda l, j: (l, 0, j))],
        out_specs=pl.BlockSpec((1, 8, d), lambda l, j: (l, 0, j)),
        out_shape=jax.ShapeDtypeStruct((nl, 8, d6), F32),
        compiler_params=_cparams("arbitrary", "arbitrary"),
        name="ada",
    )(cvec, ada_w, ada_b.reshape(nl, 1, d6))


def _inproj_body(x_ref, mod_ref, g_ref, w_ref, qg_ref, kg_ref, ca_ref, sa_ref, cc_ref, sc_ref, bd_ref,
                 qat_ref, ka_ref, vat_ref, qct_ref, kc_ref, vct_ref, pdu_ref, *, rope, tm):
    x = x_ref[0]
    mod = mod_ref[0]
    h = _rms(x) * g_ref[...] * (1.0 + mod[1:2]) + mod[0:1]
    p = jnp.dot(h.astype(BF16), w_ref[...], preferred_element_type=F32)

    lane = lax.broadcasted_iota(jnp.int32, (tm, LANES), 1)
    low = lane < HEAD_DIM
    ones_col = (lane == HEAD_DIM).astype(F32)
    nkb = tm // KEY_BLOCK

    aq = p[:, 0:256]
    qn = aq * lax.rsqrt(_segsum(aq * aq, bd_ref[...]) * (1.0 / HEAD_DIM) + 1e-6) * qg_ref[...]
    if rope:
        ca = ca_ref[...]
        sa = sa_ref[...]
        qn = _rope(qn, jnp.concatenate([ca, ca], axis=1), jnp.concatenate([sa, sa], axis=1), HEAD_DIM // 4)
    qn = qn * (HEAD_DIM ** -0.5 * LOG2E)
    for kv in range(2):
        qat_ref[0, kv] = qn[:, kv * LANES:(kv + 1) * LANES].T.astype(BF16)

    ak = p[:, 256:384]
    kn = ak * lax.rsqrt(_segsum(ak * ak, bd_ref[0:LANES, 0:LANES]) * (1.0 / HEAD_DIM) + 1e-6) * kg_ref[...]
    if rope:
        kn = _rope(kn, ca_ref[...], sa_ref[...], HEAD_DIM // 4)
    ksw = pltpu.roll(kn, HEAD_DIM, 1)
    ka_ref[0, 0] = jnp.where(low, kn, ksw).astype(BF16)
    ka_ref[0, 1] = jnp.where(low, ksw, kn).astype(BF16)

    def store_vt(ref, unit, vext):
        for j in range(nkb):
            ref[0, unit, j] = vext[j * KEY_BLOCK:(j + 1) * KEY_BLOCK].T[0:V_ROWS].astype(BF16)

    av = p[:, 384:512]
    store_vt(vat_ref, 0, jnp.where(low, av, ones_col))
    store_vt(vat_ref, 1, jnp.where(low, pltpu.roll(av, HEAD_DIM, 1), ones_col))

    cq = p[:, 768:1024]
    ck = p[:, 1024:1280]
    if rope:
        cc = cc_ref[...]
        sc = sc_ref[...]
        cc2 = jnp.concatenate([cc, cc], axis=1)
        sc2 = jnp.concatenate([sc, sc], axis=1)
        cq = _rope(cq, cc2, sc2, C_QK_DIM // 4)
        ck = _rope(ck, cc2, sc2, C_QK_DIM // 4)
    cq = cq * (C_QK_DIM ** -0.5 * LOG2E)
    for pr in range(2):
        qct_ref[0, pr] = cq[:, pr * LANES:(pr + 1) * LANES].T.astype(BF16)
        kc_ref[0, pr] = ck[:, pr * LANES:(pr + 1) * LANES].astype(BF16)
    cv = p[:, 1280:1536]
    for hd in range(4):
        seg = cv[:, (hd // 2) * LANES:(hd // 2 + 1) * LANES]
        if hd % 2:
            seg = pltpu.roll(seg, HEAD_DIM, 1)
        store_vt(vct_ref, hd, jnp.where(low, seg, ones_col))

    pdu_ref[0, :, 0:256] = p[:, 512:768]
    pdu_ref[0, :, 256:768] = p[:, 1536:2048]


def _inproj(x, mod, g_pre, w_in, q_gain, k_gain, tabs, bd, *, rope, tm):
    b, s, d = x.shape
    nkb = s // KEY_BLOCK
    tkb = tm // KEY_BLOCK
    full = lambda shape: pl.BlockSpec(shape, lambda bi, i: (0,) * len(shape))
    tab = pl.BlockSpec((tm, LANES), lambda bi, i: (i, 0))
    return pl.pallas_call(
        functools.partial(_inproj_body, rope=rope, tm=tm),
        grid=(b, s // tm),
        in_specs=[pl.BlockSpec((1, tm, d), lambda bi, i: (bi, i, 0)),
                  pl.BlockSpec((1, 6, d), lambda bi, i: (bi, 0, 0)),
                  full((1, d)), full(w_in.shape), full((1, 256)), full((1, LANES)),
                  tab, tab, tab, tab, full((256, 256))],
        out_specs=[pl.BlockSpec((1, 2, LANES, tm), lambda bi, i: (bi, 0, 0, i)),
                   pl.BlockSpec((1, 2, tm, LANES), lambda bi, i: (bi, 0, i, 0)),
                   pl.BlockSpec((1, 2, tkb, V_ROWS, KEY_BLOCK), lambda bi, i: (bi, 0, i, 0, 0)),
                   pl.BlockSpec((1, 2, LANES, tm), lambda bi, i: (bi, 0, 0, i)),
                   pl.BlockSpec((1, 2, tm, LANES), lambda bi, i: (bi, 0, i, 0)),
                   pl.BlockSpec((1, 4, tkb, V_ROWS, KEY_BLOCK), lambda bi, i: (bi, 0, i, 0, 0)),
                   pl.BlockSpec((1, tm, 768), lambda bi, i: (bi, i, 0))],
        out_shape=[jax.ShapeDtypeStruct((b, 2, LANES, s), BF16),
                   jax.ShapeDtypeStruct((b, 2, s, LANES), BF16),
                   jax.ShapeDtypeStruct((b, 2, nkb, V_ROWS, KEY_BLOCK), BF16),
                   jax.ShapeDtypeStruct((b, 2, LANES, s), BF16),
                   jax.ShapeDtypeStruct((b, 2, s, LANES), BF16),
                   jax.ShapeDtypeStruct((b, 4, nkb, V_ROWS, KEY_BLOCK), BF16),
                   jax.ShapeDtypeStruct((b, s, 768), F32)],
        compiler_params=_cparams("arbitrary", "arbitrary"),
        name="inproj",
    )(x, mod, g_pre, w_in, q_gain, k_gain, *tabs, bd)


def _flash_body(qt_ref, k_ref, vt_ref, *rest, diff, tq, nkb, lam_init):
    if diff:
        lamqk_ref, gain_ref, o_ref, s_sc, p_sc, a_sc, b_sc, m_sc, acc_sc = rest
    else:
        o_ref, s_sc, p_sc, a_sc, b_sc, m_sc, acc_sc = rest
    row =lax.broadcasted_iota(jnp.int32, (LANES, tq), 0)
    if diff:
        base = (pl.program_id(1) % 2) * HEAD_DIM
        mask0 = (row >= base) & (row < base + C_QK_DIM)
        mask1 = (row >= base + C_QK_DIM) & (row < base + 2 * C_QK_DIM)
    else:
        mask0 = row < HEAD_DIM
        mask1 = row >= HEAD_DIM
    qt = qt_ref[0, 0].astype(F32)
    qst = jnp.concatenate([jnp.where(mask0, qt, 0.0), jnp.where(mask1, qt, 0.0)], axis=1).astype(BF16)

    m_sc[...] = jnp.full(m_sc.shape, NEG_INF, F32)
    acc_sc[...] = jnp.zeros(acc_sc.shape, F32)
    for slot in range(PIPE_SETS):
        p_sc[slot] = jnp.zeros(p_sc.shape[1:], BF16)
        a_sc[slot] = jnp.ones(a_sc.shape[1:], F32)
    last = nkb - 1

    def scores(j, slot):
        k = k_ref[0, 0, pl.ds(pl.multiple_of(j * KEY_BLOCK, KEY_BLOCK), KEY_BLOCK), :]
        s = jnp.dot(k, qst, preferred_element_type=F32)
        s_sc[slot] = s
        b_sc[slot] = jnp.max(s, axis=0, keepdims=True)

    def softmax(slot):
        m_prev = m_sc[...]
        m_new = jnp.maximum(m_prev, b_sc[slot])
        a_sc[slot] = jnp.exp2(m_prev - m_new)
        p_sc[slot] = jnp.exp2((s_sc[slot] - m_new).astype(BF16))
        m_sc[...] = m_new

    def values(j, slot):
        pv = jnp.dot(vt_ref[0, 0, j], p_sc[slot], preferred_element_type=F32)
        acc_sc[...] = acc_sc[...] * a_sc[slot] + pv

    def step(j, slot, prefetch=True):
        values(jnp.maximum(j - PIPE_SETS, 0), slot)
        softmax(slot)
        if prefetch:
            scores(jnp.minimum(j + PIPE_SETS, last), slot)

    scores(0, 0)
    scores(jnp.minimum(1, last), 1)

    def trip(i, carry):
        for r in range(STEPS_PER_TRIP):
            step(STEPS_PER_TRIP * i + r, r % PIPE_SETS)
        return carry

    lax.fori_loop(0, last // STEPS_PER_TRIP, trip, 0)
    step(last, 0, prefetch=False)
    if last >= 1:
        values(last - 1, 1)
    values(last, 0)

    acc = acc_sc[...]
    o = acc[0:HEAD_DIM] / acc[HEAD_DIM:HEAD_DIM + 1]
    o0 = o[:, :tq]
    o1 = o[:, tq:]
    if diff:
        lq = lamqk_ref[...]
        lam = (jnp.exp(jnp.sum(lq[0:1] * lq[1:2], axis=1, keepdims=True))
               - jnp.exp(jnp.sum(lq[2:3] * lq[3:4], axis=1, keepdims=True)) + lam_init)
        dlt = o0 - lam * o1
        ms = jnp.mean(dlt * dlt, axis=0, keepdims=True)
        y = dlt * lax.rsqrt(ms + 1e-6) * gain_ref[...] * (1.0 - lam_init)
        out_t = jnp.concatenate([y, jnp.zeros_like(y)], axis=0)
    else:
        out_t = jnp.concatenate([o0, o1], axis=0)
    o_ref[0] = out_t.T.astype(o_ref.dtype)


def _flash(qt, k, vt, *, diff, tq, lam_qk=None, gain=None, lam_init=0.0):
    b, _, _, s = qt.shape
    units, nkb = vt.shape[1:3]
    nk = k.shape[2]
    assert nkb * KEY_BLOCK == nk and nkb % STEPS_PER_TRIP == 1
    n = 2 * tq
    ku = (lambda u: u // 2) if diff else (lambda u: u)
    in_specs = [pl.BlockSpec((1, 1, LANES, tq), lambda bi, u, i: (bi, ku(u), 0, i)),
                pl.BlockSpec((1, 1, nk, LANES), lambda bi, u, i: (bi, ku(u), 0, 0)),
                pl.BlockSpec((1, 1, nkb, V_ROWS, KEY_BLOCK), lambda bi, u, i: (bi, u, 0, 0, 0))]
    args = [qt, k, vt]
    if diff:
        in_specs += [pl.BlockSpec(lam_qk.shape, lambda bi, u, i: (0, 0)),
                     pl.BlockSpec((HEAD_DIM, 1), lambda bi, u, i: (0, 0))]
        args += [lam_qk, gain]
    return pl.pallas_call(
        functools.partial(_flash_body, diff=diff, tq=tq, nkb=nkb, lam_init=lam_init),
        grid=(b, units, s // tq),
        in_specs=in_specs,
        out_specs=pl.BlockSpec((1, tq, LANES), lambda bi, u, i: (bi, i, u)),
        out_shape=jax.ShapeDtypeStruct((b, s, units * LANES), BF16),
        scratch_shapes=[pltpu.VMEM((PIPE_SETS, KEY_BLOCK, n), F32), pltpu.VMEM((PIPE_SETS, KEY_BLOCK, n), BF16),
                        pltpu.VMEM((PIPE_SETS, 1, n), F32), pltpu.VMEM((PIPE_SETS, 1, n), F32),
                        pltpu.VMEM((1, n), F32), pltpu.VMEM((V_ROWS, n), F32)],
        compiler_params=_cparams("arbitrary", "arbitrary", "arbitrary"),
        name="flash_diff" if diff else "flash_gqa",
    )(*args)


def _mixout_body(x_ref, mod_ref, ya_ref, yc_ref, pdu_ref, prev_ref, next_ref, poolw_ref, pscale_ref,
                 sguw_ref, sgub_ref, wout_ref, gpost_ref, gpre_ref, xo_ref, tok_ref, *, tm, n):
    i = pl.program_id(1)
    nt = pl.num_programs(1)
    mod = mod_ref[0]
    pdu = pdu_ref[0]
    pb = pdu[:, 0:256]
    du = pdu[:, 256:512]
    dv = pdu[:, 512:768]

    prev = jnp.where(i > 0, prev_ref[0], 0.0)
    nxt = jnp.where(i < nt - 1, next_ref[0], 0.0)
    ext = jnp.concatenate([prev, pb, nxt], axis=0)
    rows = tm + 16
    up = lambda a, k: pltpu.roll(a, rows - k, 0)
    s2 = ext + up(ext, 1)
    s4 = s2 + up(s2, 2)
    s8 = s4 + up(s4, 4)
    s16 = s8 + up(s8, 8)
    lane = lax.broadcasted_iota(jnp.int32, (tm, 256), 1)
    grp = lane // 64
    win = jnp.where(grp == 0, up(s2, 7)[0:tm],
                    jnp.where(grp == 1, up(s4, 6)[0:tm],
                              jnp.where(grp == 2, up(s8, 4)[0:tm], s16[0:tm])))
    tok_idx = i * tm + lax.broadcasted_iota(jnp.int32, (tm, 256), 0)
    half = jnp.left_shift(1, grp)
    cnt = jnp.minimum(tok_idx + half, n) - jnp.maximum(tok_idx - half, 0)
    pooled = win / cnt.astype(F32) - pb
    pool = jnp.dot(pooled.astype(BF16), poolw_ref[...], preferred_element_type=F32) * pscale_ref[...]

    mu = jnp.mean(dv, axis=1, keepdims=True)
    dc = dv - mu
    vln = (dc * lax.rsqrt(jnp.mean(dc * dc, axis=1, keepdims=True) + 1e-5)).astype(BF16)
    head = lax.broadcasted_iota(jnp.int32, (CHUNK, 256), 1) // 64
    svs = []
    for c in range(tm // CHUNK):
        vch = vln[c * CHUNK:(c + 1) * CHUNK]
        sv = sgub_ref[...]
        for hd in range(4):
            r = jnp.dot(sguw_ref[hd], vch, preferred_element_type=F32)
            sv = sv + jnp.where(head == hd, r, 0.0)
        svs.append(sv)
    sgu = du * jnp.concatenate(svs, axis=0)

    ycat = jnp.concatenate([ya_ref[0], pool.astype(BF16), yc_ref[0], sgu.astype(BF16)], axis=1)
    o = jnp.dot(ycat, wout_ref[...], preferred_element_type=F32)
    xn = x_ref[0] + mod[2:3] * (_rms(o) * gpost_ref[...])
    xo_ref[0] = xn
    tok_ref[0] = _rms(xn) * gpre_ref[...] * (1.0 + mod[4:5]) + mod[3:4]


def _mixout(x, mod, ya, yc, pdu, poolw, pscale, sguw, sgub, wout, gpost, gpre, *, tm):
    b, s, d = x.shape
    t8 = tm // 8
    last8 = s // 8 - 1
    full = lambda shape: pl.BlockSpec(shape, lambda bi, i: (0,) * len(shape))
    row = lambda w: pl.BlockSpec((1, tm, w), lambda bi, i: (bi, i, 0))
    return pl.pallas_call(
        functools.partial(_mixout_body, tm=tm, n=s),
        grid=(b, s // tm),
        in_specs=[row(d), pl.BlockSpec((1, 6, d), lambda bi, i: (bi, 0, 0)),
                  row(256), row(512), row(768),
                  pl.BlockSpec((1, 8, 256), lambda bi, i: (bi, jnp.maximum(i * t8 - 1, 0), 0)),
                  pl.BlockSpec((1, 8, 256), lambda bi, i: (bi, jnp.minimum((i + 1) * t8, last8), 0)),
                  full((256, 256)), full((1, 256)), full(sguw.shape), full((CHUNK, 256)),
                  full(wout.shape), full((1, d)), full((1, d))],
        out_specs=[row(d), row(d)],
        out_shape=[jax.ShapeDtypeStruct((b, s, d), F32), jax.ShapeDtypeStruct((b, s, d), F32)],
        compiler_params=_cparams("arbitrary", "arbitrary"),
        name="mixout",
    )(x, mod, ya, yc, pdu, pdu, pdu, poolw, pscale, sguw, sgub, wout, gpost, gpre)


def _route(sel, scores, tm):
    per = N_EXPERTS // N_EXPERT_GROUPS
    i8 = lax.broadcasted_iota(jnp.int32, (per, tm), 0)
    gsc = []
    for g in range(N_EXPERT_GROUPS):
        blk = sel[g * per:(g + 1) * per]
        m1 = jnp.max(blk, axis=0, keepdims=True)
        i1 = jnp.min(jnp.where(blk == m1, i8, per), axis=0, keepdims=True)
        m2 = jnp.max(jnp.where(i8 == i1, NEG_INF, blk), axis=0, keepdims=True)
        gsc.append(m1 + m2)
    gs = jnp.concatenate(gsc, axis=0)
    g8 = lax.broadcasted_iota(jnp.int32, (N_EXPERT_GROUPS, tm), 0)
    gmask = jnp.zeros((N_EXPERT_GROUPS, tm), F32)
    for _ in range(TOPK_GROUPS):
        gm = jnp.max(gs, axis=0, keepdims=True)
        gi = jnp.min(jnp.where(gs == gm, g8, N_EXPERT_GROUPS), axis=0, keepdims=True)
        hit = g8 == gi
        gmask = jnp.where(hit, 1.0, gmask)
        gs = jnp.where(hit, NEG_INF, gs)
    ms = jnp.concatenate(
        [jnp.where(gmask[g:g + 1] > 0.0, sel[g * per:(g + 1) * per], NEG_INF) for g in range(N_EXPERT_GROUPS)],
        axis=0)
    e64 = lax.broadcasted_iota(jnp.int32, (N_EXPERTS, tm), 0)
    chosen = jnp.zeros((N_EXPERTS, tm), F32)
    for _ in range(TOP_K):
        m = jnp.max(ms, axis=0, keepdims=True)
        ii = jnp.min(jnp.where(ms == m, e64, N_EXPERTS), axis=0, keepdims=True)
        hit = e64 == ii
        chosen = jnp.where(hit, 1.0, chosen)
        ms = jnp.where(hit, NEG_INF, ms)
    w = chosen * scores
    return w / jnp.sum(w, axis=0, keepdims=True) * ROUTED_SCALE


def _moe_body(tok_ref, x_ref, mod_ref, rw_ref, rb_ref, w1_ref, w3_ref, w2_ref, s1_ref, s3_ref, s2_ref,
              gpost_ref, o_ref, hb_sc, gates_sc, acc_sc, *, tm):
    e = pl.program_id(2)

    @pl.when(e == 0)
    def _():
        h = tok_ref[0]
        hb = h.astype(BF16)
        hb_sc[...] = hb
        logits = jnp.dot(h, rw_ref[...], precision=lax.Precision.HIGHEST, preferred_element_type=F32)
        lt = logits.T[0:N_EXPERTS]
        scores = jax.nn.sigmoid(lt)
        gates_t = _route(scores + rb_ref[...], scores, tm)
        gates_sc[...] = jnp.concatenate([gates_t, jnp.zeros_like(gates_t)], axis=0).T
        a = jnp.dot(hb, s1_ref[...], preferred_element_type=F32)
        g = jnp.dot(hb, s3_ref[...], preferred_element_type=F32)
        acc_sc[...] = jnp.dot((_silu(a) * g).astype(BF16), s2_ref[...], preferred_element_type=F32)

    hb = hb_sc[...]
    lane = lax.broadcasted_iota(jnp.int32, (tm, LANES), 1)
    gates = gates_sc[...]
    acts = []
    for i in range(EXPERTS_PER_STEP):
        gate = jnp.sum(jnp.where(lane == e * EXPERTS_PER_STEP + i, gates, 0.0), axis=1, keepdims=True)
        a = jnp.dot(hb, w1_ref[0, i].astype(BF16), preferred_element_type=F32)
        g = jnp.dot(hb, w3_ref[0, i].astype(BF16), preferred_element_type=F32)
        acts.append((_silu(a) * g * gate).astype(BF16))
    hid = w2_ref.shape[2]
    w2 = w2_ref[0].reshape(EXPERTS_PER_STEP * hid, w2_ref.shape[3]).astype(BF16)
    acc_sc[...] += jnp.dot(jnp.concatenate(acts, axis=1), w2, preferred_element_type=F32)

    @pl.when(e == pl.num_programs(2) - 1)
    def _():
        mod = mod_ref[0]
        o_ref[0] = x_ref[0] + mod[5:6] * (_rms(acc_sc[...]) * gpost_ref[...])


def _moe(tok, x, mod, rw, rb, w1, w3, w2, s1, s3, s2, gpost, *, layer, tm):
    b, s, d = x.shape
    ne, _, hid = w1.shape[1:]
    full = lambda shape: pl.BlockSpec(shape, lambda bi, i, e: (0,) * len(shape))
    row = pl.BlockSpec((1, tm, d), lambda bi, i, e: (bi, i, 0))
    return pl.pallas_call(
        functools.partial(_moe_body, tm=tm),
        grid=(b, s // tm, ne // EXPERTS_PER_STEP),
        in_specs=[row, row, pl.BlockSpec((1, 6, d), lambda bi, i, e: (bi, 0, 0)),
                  full((d, LANES)), full((N_EXPERTS, 1)),
                  pl.BlockSpec((1, EXPERTS_PER_STEP, d, hid), lambda bi, i, e: (layer, e, 0, 0)),
                  pl.BlockSpec((1, EXPERTS_PER_STEP, d, hid), lambda bi, i, e: (layer, e, 0, 0)),
                  pl.BlockSpec((1, EXPERTS_PER_STEP, hid, d), lambda bi, i, e: (layer, e, 0, 0)),
                  full(s1.shape), full(s3.shape), full(s2.shape), full((1, d))],
        out_specs=row,
        out_shape=jax.ShapeDtypeStruct((b, s, d), F32),
        scratch_shapes=[pltpu.VMEM((tm, d), BF16), pltpu.VMEM((tm, LANES), F32), pltpu.VMEM((tm, d), F32)],
        compiler_params=_cparams("arbitrary", "arbitrary", "arbitrary"),
        name="moe",
    )(tok, x, mod, rw, rb, w1, w3, w2, s1, s3, s2, gpost)


SPARSE_TILE = 512
SC_WINDOW = 128
PIECE = 256
HI16 = 0xFFFF0000


def _pack_pair(lo, hi):
    def rne(x):
        u = lax.bitcast_convert_type(x, jnp.uint32)
        return u + jnp.uint32(0x7FFF) + ((u >> 16) & jnp.uint32(1))
    word = (rne(hi) & jnp.uint32(HI16)) | (rne(lo) >> 16)
    return lax.bitcast_convert_type(word, jnp.int32)


def _unpack_pair(word):
    u = lax.bitcast_convert_type(word, jnp.uint32)
    return (lax.bitcast_convert_type(u << 16, F32), lax.bitcast_convert_type(u & jnp.uint32(HI16), F32))


def _pack_row(x):
    return [_pack_pair(x[:, 2 * p * PIECE:(2 * p + 1) * PIECE], x[:, (2 * p + 1) * PIECE:(2 * p + 2) * PIECE])
            for p in range(2)]


def _unpack_row(p0, p1):
    return jnp.concatenate(_unpack_pair(p0) + _unpack_pair(p1), axis=1)


def _sc_mesh():
    from jax.experimental.pallas import tpu_sc as plsc
    return plsc.VectorSubcoreMesh(core_axis_name="core", subcore_axis_name="subcore")


def _sc_gather(table, idx):
    nb = idx.shape[0]
    d = table.shape[1]
    assert nb % SC_WINDOW == 0

    @functools.partial(pl.kernel, out_type=jax.ShapeDtypeStruct((nb, d), table.dtype), mesh=_sc_mesh())
    def gather_kernel(x_hbm, i_hbm, o_hbm):
        def body(i_vmem, o_vmem):
            pltpu.sync_copy(x_hbm.at[i_vmem.at[0]], o_vmem)

        pltpu.emit_pipeline(
            body,
            grid=(nb // SC_WINDOW,),
            in_specs=[pl.BlockSpec((1, SC_WINDOW), lambda i: (0, i))],
            out_specs=[pl.BlockSpec((SC_WINDOW, d), lambda i: (i, 0))],
            core_axis_name=("core", "subcore"),
            dimension_semantics=(pltpu.PARALLEL,),
        )(i_hbm, o_hbm)

    return gather_kernel(table, idx.reshape(1, nb))


def _sc_scatter(x, idx, out_rows):
    rounds, m = idx.shape
    d = x.shape[1]
    nblk = m // SC_WINDOW
    assert m % SC_WINDOW == 0

    @functools.partial(pl.kernel, out_type=jax.ShapeDtypeStruct((out_rows, d), x.dtype), mesh=_sc_mesh())
    def scatter_kernel(x_hbm, i_hbm, o_hbm):
        def body(x_vmem, i_vmem):
            pltpu.sync_copy(x_vmem, o_hbm.at[i_vmem.at[0]])

        pltpu.emit_pipeline(
            body,
            grid=(rounds * nblk,),
            in_specs=[pl.BlockSpec((SC_WINDOW, d), lambda i: (i % nblk, 0)),
                      pl.BlockSpec((1, SC_WINDOW), lambda i: (0, i))],
            out_specs=[],
            core_axis_name=("core", "subcore"),
            dimension_semantics=(pltpu.PARALLEL,),
        )(x_hbm, i_hbm)

    return scatter_kernel(x, idx.reshape(1, rounds * m))


def _topk_route(sel, scores, tm):
    per = N_EXPERTS // N_EXPERT_GROUPS
    i8 = lax.broadcasted_iota(jnp.int32, (per, tm), 0)
    gsc = []
    for g in range(N_EXPERT_GROUPS):
        blk = sel[g * per:(g + 1) * per]
        m1 = jnp.max(blk, axis=0, keepdims=True)
        i1 = jnp.min(jnp.where(blk == m1, i8, per), axis=0, keepdims=True)
        m2 = jnp.max(jnp.where(i8 == i1, NEG_INF, blk), axis=0, keepdims=True)
        gsc.append(m1 + m2)
    gs = jnp.concatenate(gsc, axis=0)
    g8 = lax.broadcasted_iota(jnp.int32, (N_EXPERT_GROUPS, tm), 0)
    gmask = jnp.zeros((N_EXPERT_GROUPS, tm), F32)
    for _ in range(TOPK_GROUPS):
        gm = jnp.max(gs, axis=0, keepdims=True)
        gi = jnp.min(jnp.where(gs == gm, g8, N_EXPERT_GROUPS), axis=0, keepdims=True)
        hit = g8 == gi
        gmask = jnp.where(hit, 1.0, gmask)
        gs = jnp.where(hit, NEG_INF, gs)
    ms = jnp.concatenate(
        [jnp.where(gmask[g:g + 1] > 0.0, sel[g * per:(g + 1) * per], NEG_INF) for g in range(N_EXPERT_GROUPS)],
        axis=0)
    e64 = lax.broadcasted_iota(jnp.int32, (N_EXPERTS, tm), 0)
    hits, ids = [], []
    for _ in range(TOP_K):
        m = jnp.max(ms, axis=0, keepdims=True)
        ii = jnp.min(jnp.where(ms == m, e64, N_EXPERTS), axis=0, keepdims=True)
        hit = e64 == ii
        hits.append(hit)
        ids.append(ii)
        ms = jnp.where(hit, NEG_INF, ms)
    return hits, ids


def _route_body(tok_ref, rw_ref, rb_ref, tri_ref, tokp_ref, eidx_ref, posk_ref, wts_ref, cnt_ref, run_sc, *, tm):
    first = (pl.program_id(0) == 0) & (pl.program_id(1) == 0)

    @pl.when(first)
    def _():
        run_sc[...] = jnp.zeros(run_sc.shape, F32)

    h = tok_ref[0]
    pieces = _pack_row(h)
    tokp_ref[0] = pieces[0]
    tokp_ref[1] = pieces[1]

    logits = jnp.dot(h, rw_ref[...], precision=lax.Precision.HIGHEST, preferred_element_type=F32)
    scores = jax.nn.sigmoid(logits.T[0:N_EXPERTS])
    hits, ids = _topk_route(scores + rb_ref[...], scores, tm)
    raw = [jnp.sum(jnp.where(hit, scores, 0.0), axis=0, keepdims=True) for hit in hits]
    denom = raw[0]
    for r in raw[1:]:
        denom = denom + r
    wts = jnp.concatenate([r / denom * ROUTED_SCALE for r in raw], axis=0)
    wts_ref[...] = jnp.concatenate([wts, jnp.zeros((LANES - TOP_K, tm), F32)], axis=0).T

    chosen = jnp.zeros((N_EXPERTS, tm), F32)
    for hit in hits:
        chosen = jnp.where(hit, 1.0, chosen)
    incl = jnp.dot(chosen.astype(BF16), tri_ref[...], preferred_element_type=F32)
    before = run_sc[...] + incl - chosen
    posk_ref[...] = jnp.concatenate(
        [jnp.sum(jnp.where(hit, before, 0.0), axis=0, keepdims=True) for hit in hits], axis=0).astype(jnp.int32)
    eidx_ref[...] = jnp.concatenate(ids, axis=0)
    run_sc[...] = run_sc[...] + jnp.sum(chosen, axis=1, keepdims=True)
    cnt_ref[...] = jnp.broadcast_to(run_sc[...], cnt_ref.shape)


def _route_tokens(tok, rw, rb, tri, *, tm):
    b, s, d = tok.shape
    n = b * s
    nt = s // tm
    full = lambda shape: pl.BlockSpec(shape, lambda bi, i: (0,) * len(shape))
    col = lambda rows: pl.BlockSpec((rows, tm), lambda bi, i: (0, bi * nt + i))
    return pl.pallas_call(
        functools.partial(_route_body, tm=tm),
        grid=(b, nt),
        in_specs=[pl.BlockSpec((1, tm, d), lambda bi, i: (bi, i, 0)),
                  full((d, LANES)), full((N_EXPERTS, 1)), full((tm, tm))],
        out_specs=[pl.BlockSpec((2, tm, PIECE), lambda bi, i: (0, bi * nt + i, 0)),
                   col(TOP_K), col(TOP_K),
                   pl.BlockSpec((tm, LANES), lambda bi, i: (bi * nt + i, 0)),
                   full((N_EXPERTS, LANES))],
        out_shape=[jax.ShapeDtypeStruct((2, n, PIECE), jnp.int32),
                   jax.ShapeDtypeStruct((TOP_K, n), jnp.int32),
                   jax.ShapeDtypeStruct((TOP_K, n), jnp.int32),
                   jax.ShapeDtypeStruct((n, LANES), F32),
                   jax.ShapeDtypeStruct((N_EXPERTS, LANES), F32)],
        scratch_shapes=[pltpu.VMEM((N_EXPERTS, 1), F32)],
        compiler_params=_cparams("arbitrary", "arbitrary"),
        name="route",
    )(tok, rw, rb, tri)


def _ffn_body(te_ref, tv_ref, x_ref, w1_ref, w3_ref, w2_ref, y_ref):
    valid = tv_ref[pl.program_id(0)]

    @pl.when(valid > 0)
    def _():
        x = _unpack_row(x_ref[0], x_ref[1])
        rows = lax.broadcasted_iota(jnp.int32, x.shape, 0)
        xb = jnp.where(rows < valid, x, 0.0).astype(BF16)
        a = jnp.dot(xb, w1_ref[0, 0].astype(BF16), preferred_element_type=F32)
        g = jnp.dot(xb, w3_ref[0, 0].astype(BF16), preferred_element_type=F32)
        y = jnp.dot((_silu(a) * g).astype(BF16), w2_ref[0, 0].astype(BF16), preferred_element_type=F32)
        pieces = _pack_row(y)
        y_ref[0] = pieces[0]
        y_ref[1] = pieces[1]


def _expert_ffn(tile_expert, tile_valid, xs, w1, w3, w2, *, layer):
    _, rows, _ = xs.shape
    d, hid = w1.shape[2:]
    blk = pl.BlockSpec((2, SPARSE_TILE, PIECE), lambda i, te, tv: (0, i, 0))
    return pl.pallas_call(
        _ffn_body,
        grid_spec=pltpu.PrefetchScalarGridSpec(
            num_scalar_prefetch=2,
            grid=(rows // SPARSE_TILE,),
            in_specs=[blk,
                      pl.BlockSpec((1, 1, d, hid), lambda i, te, tv: (layer, te[i], 0, 0)),
                      pl.BlockSpec((1, 1, d, hid), lambda i, te, tv: (layer, te[i], 0, 0)),
                      pl.BlockSpec((1, 1, hid, d), lambda i, te, tv: (layer, te[i], 0, 0))],
            out_specs=blk),
        out_shape=jax.ShapeDtypeStruct(xs.shape, jnp.int32),
        compiler_params=_cparams("arbitrary"),
        name="expert_ffn",
    )(tile_expert, tile_valid, xs, w1, w3, w2)


def _combine_body(tok_ref, x_ref, mod_ref, yg_ref, wts_ref, s1_ref, s3_ref, s2_ref, gpost_ref, o_ref):
    hb = tok_ref[0].astype(BF16)
    a = jnp.dot(hb, s1_ref[...], preferred_element_type=F32)
    g = jnp.dot(hb, s3_ref[...], preferred_element_type=F32)
    f = jnp.dot((_silu(a) * g).astype(BF16), s2_ref[...], preferred_element_type=F32)
    wts = wts_ref[...]
    for k in range(TOP_K):
        f = f + wts[:, k:k + 1] * _unpack_row(yg_ref[0, k], yg_ref[1, k])
    o_ref[0] = x_ref[0] + mod_ref[0][5:6] * (_rms(f) * gpost_ref[...])


def _combine(tok, x, mod, yg, wts, s1, s3, s2, gpost, *, tm):
    b, s, d = x.shape
    nt = s // tm
    full = lambda shape: pl.BlockSpec(shape, lambda bi, i: (0,) * len(shape))
    row = pl.BlockSpec((1, tm, d), lambda bi, i: (bi, i, 0))
    return pl.pallas_call(
        _combine_body,
        grid=(b, nt),
        in_specs=[row, row, pl.BlockSpec((1, 6, d), lambda bi, i: (bi, 0, 0)),
                  pl.BlockSpec((2, TOP_K, tm, PIECE), lambda bi, i: (0, 0, bi * nt + i, 0)),
                  pl.BlockSpec((tm, LANES), lambda bi, i: (bi * nt + i, 0)),
                  full(s1.shape), full(s3.shape), full(s2.shape), full((1, d))],
        out_specs=row,
        out_shape=jax.ShapeDtypeStruct((b, s, d), F32),
        compiler_params=_cparams("arbitrary", "arbitrary"),
        name="combine",
    )(tok, x, mod, yg, wts, s1, s3, s2, gpost)


def _sparse_moe(tok, x, mod, rw, rb, w1, w3, w2, s1, s3, s2, gpost, *, layer, tm_route, tm_combine):
    b, s, d = x.shape
    n = b * s
    rows = n * TOP_K + N_EXPERTS * SPARSE_TILE
    ntiles = rows // SPARSE_TILE
    tri = jnp.triu(jnp.ones((tm_route, tm_route), BF16))
    tokp, eidx, posk, wts, cnt = _route_tokens(tok, rw, rb, tri, tm=tm_route)

    cnt = cnt[:, 0].astype(jnp.int32)
    padded = (cnt + SPARSE_TILE - 1) // SPARSE_TILE * SPARSE_TILE
    ends = jnp.cumsum(padded)
    offs = ends - padded
    experts = jnp.arange(N_EXPERTS, dtype=jnp.int32)
    dest = posk + jnp.sum(jnp.where(eidx[None] == experts[:, None, None], offs[:, None, None], 0), axis=0)
    tile_start = jnp.arange(ntiles, dtype=jnp.int32) * SPARSE_TILE
    tile_expert = jnp.minimum(jnp.sum(tile_start[:, None] >= ends[None, :], axis=1), N_EXPERTS - 1).astype(jnp.int32)
    tile_valid = jnp.clip(cnt[tile_expert] - (tile_start - offs[tile_expert]), 0, SPARSE_TILE).astype(jnp.int32)

    piece_base = jnp.arange(2, dtype=jnp.int32) * rows
    sidx = (dest[:, None, :] + piece_base[None, :, None]).reshape(TOP_K, 2 * n)
    xs = _sc_scatter(tokp.reshape(2 * n, PIECE), sidx, 2 * rows).reshape(2, rows, PIECE)
    ys = _expert_ffn(tile_expert, tile_valid, xs, w1, w3, w2, layer=layer)
    gidx = (piece_base[:, None, None] + dest[None, :, :]).reshape(2 * TOP_K * n)
    yg = _sc_gather(ys.reshape(2 * rows, PIECE), gidx).reshape(2, TOP_K, n, PIECE)
    return _combine(tok, x, mod, yg, wts, s1, s3, s2, gpost, tm=tm_combine)


def _rope_tables(s, dim):
    rows = s // GRID_W
    row = jnp.repeat(jnp.arange(rows, dtype=F32), GRID_W)
    col = jnp.tile(jnp.arange(GRID_W, dtype=F32), rows)
    half = dim // 2
    inv = ROPE_THETA ** (-jnp.arange(0, half, 2, dtype=F32) / half)
    ar = row[:, None] * inv[None, :]
    ac = col[:, None] * inv[None, :]
    ang = jnp.concatenate([ar, ar, ac, ac], axis=-1)
    sign = jnp.where((jnp.arange(dim) & (dim // 4)) == 0, -1.0, 1.0).astype(F32)
    reps = LANES // dim
    return jnp.tile(jnp.cos(ang), (1, reps)), jnp.tile(jnp.sin(ang) * sign, (1, reps))


def _block_diag(blocks):
    n = len(blocks)
    r, c = blocks[0].shape
    out = jnp.zeros((n * r, n * c), blocks[0].dtype)
    for i, blk in enumerate(blocks):
        out = out.at[i * r:(i + 1) * r, i * c:(i + 1) * c].set(blk)
    return out


def kernel(x, c, ctx, c_ctx, ada_w, ada_b, g_pre_mix, g_post_mix, g_pre_ffn, g_post_ffn, w_in, w_out, a_q_gain, a_k_gain, pool_w, pool_scale, lam_qk, c_subln_gain, sgu_w, sgu_b, router_w, router_bias, exp_w1, exp_w3, exp_w2, sh_w1, sh_w3, sh_w2):
    b, s, d = x.shape
    nctx = ctx.shape[1]
    depth = ada_w.shape[0]
    tm_lat = 512
    tq = 512
    tm_moe = 1024

    cvec = jnp.zeros((8, d), F32).at[0:b].set(c).at[b].set(c_ctx)
    mods = _ada(cvec, ada_w, ada_b)

    tabs = _rope_tables(s, HEAD_DIM) + _rope_tables(s, C_QK_DIM)
    bd = _block_diag([jnp.ones((HEAD_DIM, HEAD_DIM), BF16)] * 4)
    row2 = lambda v: v.reshape(1, -1)

    xl, xc = x, ctx
    for l in range(depth):
        need_ctx = l < depth - 1
        lam_init = 0.8 - 0.6 * math.exp(-0.3 * l)
        m6 = mods[l].reshape(8, 6, d)
        mod_l = m6[0:b]
        mod_c = jnp.broadcast_to(m6[b:b + 1], (b, 6, d))

        w_in_l = w_in[l].astype(BF16)
        qg = jnp.tile(a_q_gain[l], 4).reshape(1, 256)
        kg = jnp.tile(a_k_gain[l], 2).reshape(1, LANES)
        inproj = functools.partial(_inproj, g_pre=row2(g_pre_mix[l]), w_in=w_in_l, q_gain=qg, k_gain=kg,
                                   tabs=tabs, bd=bd)
        qat_l, ka_l, vat_l, qct_l, kc_l, vct_l, pdu_l = inproj(xl, mod_l, rope=True, tm=tm_lat)
        qat_c, ka_c, vat_c, qct_c, kc_c, vct_c, pdu_c = inproj(xc, mod_c, rope=False, tm=nctx)

        ka = jnp.concatenate([ka_c, ka_l], axis=2)
        vat = jnp.concatenate([vat_c, vat_l], axis=2)
        kc = jnp.concatenate([kc_c, kc_l], axis=2)
        vct = jnp.concatenate([vct_c, vct_l], axis=2)
        sub_gain = c_subln_gain[l].reshape(HEAD_DIM, 1)

        ya_l = _flash(qat_l, ka, vat, diff=False, tq=tq)
        yc_l = _flash(qct_l, kc, vct, diff=True, tq=tq, lam_qk=lam_qk[l], gain=sub_gain,
                      lam_init=lam_init)

        poolw = _block_diag([pool_w[l, g] for g in range(len(POOL_WINDOWS))]).astype(BF16)
        sgub = jnp.repeat(jnp.transpose(sgu_b[l]), d // 16, axis=1)
        wo = w_out[l]
        wo_c = jnp.pad(wo[512:768].reshape(4, HEAD_DIM, d), ((0, 0), (0, LANES - HEAD_DIM), (0, 0)))
        wout = jnp.concatenate([wo[0:512], wo_c.reshape(4 * LANES, d), wo[768:1024]], axis=0).astype(BF16)
        mixout = functools.partial(_mixout, poolw=poolw, pscale=row2(pool_scale[l]), sguw=sgu_w[l].astype(BF16),
                                   sgub=sgub, wout=wout, gpost=row2(g_post_mix[l]), gpre=row2(g_pre_ffn[l]))
        rw = jnp.pad(router_w[l], ((0, 0), (0, LANES - N_EXPERTS)))
        moe_args = dict(rw=rw, rb=router_bias[l].reshape(N_EXPERTS, 1), w1=exp_w1, w3=exp_w3,
                        w2=exp_w2, s1=sh_w1[l].astype(BF16), s3=sh_w3[l].astype(BF16),
                        s2=sh_w2[l].astype(BF16), gpost=row2(g_post_ffn[l]), layer=l)
        moe = functools.partial(_moe, **moe_args)

        xl_mid, tok_l = mixout(xl, mod_l, ya_l, yc_l, pdu_l, tm=tm_lat)
        xl = _sparse_moe(tok_l, xl_mid, mod_l, tm_route=tm_moe, tm_combine=tm_lat, **moe_args)
        if need_ctx:
            ya_c = _flash(qat_c, ka_c, vat_c, diff=False, tq=nctx)
            yc_c = _flash(qct_c, kc_c, vct_c, diff=True, tq=nctx, lam_qk=lam_qk[l], gain=sub_gain,
                          lam_init=lam_init)
            xc_mid, tok_c = mixout(xc, mod_c, ya_c, yc_c, pdu_c, tm=nctx)
            xc = moe(tok_c, xc_mid, mod_c, tm=nctx)
    return xl
```

```python
import functools
import math

import jax
import jax.numpy as jnp
from jax import lax
from jax.experimental import pallas as pl
from jax.experimental.pallas import tpu as pltpu

F32 = jnp.float32
BF16 = jnp.bfloat16

GRID_W = 64
ROPE_THETA = 10000.0
HEAD_DIM = 64
C_QK_DIM = 32
POOL_WINDOWS = (2, 4, 8, 16)
CHUNK = 128
N_EXPERTS = 64
TOP_K = 8
N_EXPERT_GROUPS = 8
TOPK_GROUPS = 4
ROUTED_SCALE = 2.5
EXPERTS_PER_STEP = 2

LANES = 128
KEY_BLOCK = 256
PIPE_SETS = 2
STEPS_PER_TRIP = 16
V_ROWS = 80
VMEM_LIMIT = 56 * 1024 * 1024

NEG_INF = float("-inf")
LOG2E = math.log2(math.e)


def _cparams(*sem, flags=None):
    return pltpu.CompilerParams(dimension_semantics=sem, vmem_limit_bytes=VMEM_LIMIT, flags=flags)


def _rms(x, eps=1e-6):
    return x * lax.rsqrt(jnp.mean(x * x, axis=-1, keepdims=True) + eps)


def _segsum(sq, bd):
    hi = sq.astype(BF16)
    lo = (sq - hi.astype(F32)).astype(BF16)
    return (jnp.dot(hi, bd, preferred_element_type=F32)
            + jnp.dot(lo, bd, preferred_element_type=F32))


def _rope(x, cos, sin_signed, quarter):
    w = x.shape[1]
    lane = lax.broadcasted_iota(jnp.int32, x.shape, 1)
    first = (lane & quarter) == 0
    rot = jnp.where(first, pltpu.roll(x, w - quarter, 1), pltpu.roll(x, quarter, 1))
    return x * cos + rot * sin_signed


def _silu(x):
    return x * jax.nn.sigmoid(x)


def _ada_body(c_ref, w_ref, b_ref, o_ref):
    sc = _silu(c_ref[...])
    o_ref[0] = jnp.dot(sc, w_ref[0], precision=lax.Precision.HIGHEST,
                       preferred_element_type=F32) + b_ref[0]


def _ada(cvec, ada_w, ada_b):
    nl, d, d6 = ada_w.shape
    return pl.pallas_call(
        _ada_body,
        grid=(nl, d6 // d),
        in_specs=[pl.BlockSpec((8, d), lambda l, j: (0, 0)),
                  pl.BlockSpec((1, d, d), lambda l, j: (l, 0, j)),
                  pl.BlockSpec((1, 1, d), lambda l, j: (l, 0, j))],
        out_specs=pl.BlockSpec((1, 8, d), lambda l, j: (l, 0, j)),
        out_shape=jax.ShapeDtypeStruct((nl, 8, d6), F32),
        compiler_params=_cparams("arbitrary", "arbitrary"),
        name="ada",
    )(cvec, ada_w, ada_b.reshape(nl, 1, d6))


def _inproj_body(x_ref, mod_ref, g_ref, w_ref, qg_ref, kg_ref, ca_ref, sa_ref, cc_ref, sc_ref, bd_ref,
                 qat_ref, ka_ref, vat_ref, qct_ref, kc_ref, vct_ref, pdu_ref, *, rope, tm):
    x = x_ref[0]
    mod = mod_ref[0]
    h = _rms(x) * g_ref[...] * (1.0 + mod[1:2]) + mod[0:1]
    p = jnp.dot(h.astype(BF16), w_ref[...], preferred_element_type=F32)

    lane = lax.broadcasted_iota(jnp.int32, (tm, LANES), 1)
    low = lane < HEAD_DIM
    ones_col = (lane == HEAD_DIM).astype(F32)
    nkb = tm // KEY_BLOCK

    aq = p[:, 0:256]
    qn = aq * lax.rsqrt(_segsum(aq * aq, bd_ref[...]) * (1.0 / HEAD_DIM) + 1e-6) * qg_ref[...]
    if rope:
        ca = ca_ref[...]
        sa = sa_ref[...]
        qn = _rope(qn, jnp.concatenate([ca, ca], axis=1), jnp.concatenate([sa, sa], axis=1), HEAD_DIM // 4)
    qn = qn * (HEAD_DIM ** -0.5 * LOG2E)
    for kv in range(2):
        qat_ref[0, kv] = qn[:, kv * LANES:(kv + 1) * LANES].T.astype(BF16)

    ak = p[:, 256:384]
    kn = ak * lax.rsqrt(_segsum(ak * ak, bd_ref[0:LANES, 0:LANES]) * (1.0 / HEAD_DIM) + 1e-6) * kg_ref[...]
    if rope:
        kn = _rope(kn, ca_ref[...], sa_ref[...], HEAD_DIM // 4)
    ksw = pltpu.roll(kn, HEAD_DIM, 1)
    ka_ref[0, 0] = jnp.where(low, kn, ksw).astype(BF16)
    ka_ref[0, 1] = jnp.where(low, ksw, kn).astype(BF16)

    def store_vt(ref, unit, vext):
        for j in range(nkb):
            ref[0, unit, j] = vext[j * KEY_BLOCK:(j + 1) * KEY_BLOCK].T[0:V_ROWS].astype(BF16)

    av = p[:, 384:512]
    store_vt(vat_ref, 0, jnp.where(low, av, ones_col))
    store_vt(vat_ref, 1, jnp.where(low, pltpu.roll(av, HEAD_DIM, 1), ones_col))

    cq = p[:, 768:1024]
    ck = p[:, 1024:1280]
    if rope:
        cc = cc_ref[...]
        sc = sc_ref[...]
        cc2 = jnp.concatenate([cc, cc], axis=1)
        sc2 = jnp.concatenate([sc, sc], axis=1)
        cq = _rope(cq, cc2, sc2, C_QK_DIM // 4)
        ck = _rope(ck, cc2, sc2, C_QK_DIM // 4)
    cq = cq * (C_QK_DIM ** -0.5 * LOG2E)
    for pr in range(2):
        qct_ref[0, pr] = cq[:, pr * LANES:(pr + 1) * LANES].T.astype(BF16)
        kc_ref[0, pr] = ck[:, pr * LANES:(pr + 1) * LANES].astype(BF16)
    cv = p[:, 1280:1536]
    for hd in range(4):
        seg = cv[:, (hd // 2) * LANES:(hd // 2 + 1) * LANES]
        if hd % 2:
            seg = pltpu.roll(seg, HEAD_DIM, 1)
        store_vt(vct_ref, hd, jnp.where(low, seg, ones_col))

    pdu_ref[0, :, 0:256] = p[:, 512:768]
    pdu_ref[0, :, 256:768] = p[:, 1536:2048]


def _inproj(x, mod, g_pre, w_in, q_gain, k_gain, tabs, bd, *, rope, tm):
    b, s, d = x.shape
    nkb = s // KEY_BLOCK
    tkb = tm // KEY_BLOCK
    full = lambda shape: pl.BlockSpec(shape, lambda bi, i: (0,) * len(shape))
    tab = pl.BlockSpec((tm, LANES), lambda bi, i: (i, 0))
    return pl.pallas_call(
        functools.partial(_inproj_body, rope=rope, tm=tm),
        grid=(b, s // tm),
        in_specs=[pl.BlockSpec((1, tm, d), lambda bi, i: (bi, i, 0)),
                  pl.BlockSpec((1, 6, d), lambda bi, i: (bi, 0, 0)),
                  full((1, d)), full(w_in.shape), full((1, 256)), full((1, LANES)),
                  tab, tab, tab, tab, full((256, 256))],
        out_specs=[pl.BlockSpec((1, 2, LANES, tm), lambda bi, i: (bi, 0, 0, i)),
                   pl.BlockSpec((1, 2, tm, LANES), lambda bi, i: (bi, 0, i, 0)),
                   pl.BlockSpec((1, 2, tkb, V_ROWS, KEY_BLOCK), lambda bi, i: (bi, 0, i, 0, 0)),
                   pl.BlockSpec((1, 2, LANES, tm), lambda bi, i: (bi, 0, 0, i)),
                   pl.BlockSpec((1, 2, tm, LANES), lambda bi, i: (bi, 0, i, 0)),
                   pl.BlockSpec((1, 4, tkb, V_ROWS, KEY_BLOCK), lambda bi, i: (bi, 0, i, 0, 0)),
                   pl.BlockSpec((1, tm, 768), lambda bi, i: (bi, i, 0))],
        out_shape=[jax.ShapeDtypeStruct((b, 2, LANES, s), BF16),
                   jax.ShapeDtypeStruct((b, 2, s, LANES), BF16),
                   jax.ShapeDtypeStruct((b, 2, nkb, V_ROWS, KEY_BLOCK), BF16),
                   jax.ShapeDtypeStruct((b, 2, LANES, s), BF16),
                   jax.ShapeDtypeStruct((b, 2, s, LANES), BF16),
                   jax.ShapeDtypeStruct((b, 4, nkb, V_ROWS, KEY_BLOCK), BF16),
                   jax.ShapeDtypeStruct((b, s, 768), F32)],
        compiler_params=_cparams("arbitrary", "arbitrary"),
        name="inproj",
    )(x, mod, g_pre, w_in, q_gain, k_gain, *tabs, bd)


def _flash_body(qt_ref, k_ref, vt_ref, *rest, diff, tq, nkb, lam_init):
    if diff:
        lamqk_ref, gain_ref, o_ref, s_sc, p_sc, a_sc, b_sc, m_sc, acc_sc = rest
    else:
        o_ref, s_sc, p_sc, a_sc, b_sc, m_sc, acc_sc = rest
    row =lax.broadcasted_iota(jnp.int32, (LANES, tq), 0)
    if diff:
        base = (pl.program_id(1) % 2) * HEAD_DIM
        mask0 = (row >= base) & (row < base + C_QK_DIM)
        mask1 = (row >= base + C_QK_DIM) & (row < base + 2 * C_QK_DIM)
    else:
        mask0 = row < HEAD_DIM
        mask1 = row >= HEAD_DIM
    qt = qt_ref[0, 0].astype(F32)
    qst = jnp.concatenate([jnp.where(mask0, qt, 0.0), jnp.where(mask1, qt, 0.0)], axis=1).astype(BF16)

    m_sc[...] = jnp.full(m_sc.shape, NEG_INF, F32)
    acc_sc[...] = jnp.zeros(acc_sc.shape, F32)
    for slot in range(PIPE_SETS):
        p_sc[slot] = jnp.zeros(p_sc.shape[1:], BF16)
        a_sc[slot] = jnp.ones(a_sc.shape[1:], F32)
    last = nkb - 1

    def scores(j, slot):
        k = k_ref[0, 0, pl.ds(pl.multiple_of(j * KEY_BLOCK, KEY_BLOCK), KEY_BLOCK), :]
        s = jnp.dot(k, qst, preferred_element_type=F32)
        s_sc[slot] = s
        b_sc[slot] = jnp.max(s, axis=0, keepdims=True)

    def softmax(slot):
        m_prev = m_sc[...]
        m_new = jnp.maximum(m_prev, b_sc[slot])
        a_sc[slot] = jnp.exp2(m_prev - m_new)
        p_sc[slot] = jnp.exp2((s_sc[slot] - m_new).astype(BF16))
        m_sc[...] = m_new

    def values(j, slot):
        pv = jnp.dot(vt_ref[0, 0, j], p_sc[slot], preferred_element_type=F32)
        acc_sc[...] = acc_sc[...] * a_sc[slot] + pv

    def step(j, slot, prefetch=True):
        values(jnp.maximum(j - PIPE_SETS, 0), slot)
        softmax(slot)
        if prefetch:
            scores(jnp.minimum(j + PIPE_SETS, last), slot)

    scores(0, 0)
    scores(jnp.minimum(1, last), 1)

    def trip(i, carry):
        for r in range(STEPS_PER_TRIP):
            step(STEPS_PER_TRIP * i + r, r % PIPE_SETS)
        return carry

    lax.fori_loop(0, last // STEPS_PER_TRIP, trip, 0)
    step(last, 0, prefetch=False)
    if last >= 1:
        values(last - 1, 1)
    values(last, 0)

    acc = acc_sc[...]
    o = acc[0:HEAD_DIM] / acc[HEAD_DIM:HEAD_DIM + 1]
    o0 = o[:, :tq]
    o1 = o[:, tq:]
    if diff:
        lq = lamqk_ref[...]
        lam = (jnp.exp(jnp.sum(lq[0:1] * lq[1:2], axis=1, keepdims=True))
               - jnp.exp(jnp.sum(lq[2:3] * lq[3:4], axis=1, keepdims=True)) + lam_init)
        dlt = o0 - lam * o1
        ms = jnp.mean(dlt * dlt, axis=0, keepdims=True)
        y = dlt * lax.rsqrt(ms + 1e-6) * gain_ref[...] * (1.0 - lam_init)
        out_t = jnp.concatenate([y, jnp.zeros_like(y)], axis=0)
    else:
        out_t = jnp.concatenate([o0, o1], axis=0)
    o_ref[0] = out_t.T.astype(o_ref.dtype)


def _flash(qt, k, vt, *, diff, tq, lam_qk=None, gain=None, lam_init=0.0):
    b, _, _, s = qt.shape
    units, nkb = vt.shape[1:3]
    nk = k.shape[2]
    assert nkb * KEY_BLOCK == nk and nkb % STEPS_PER_TRIP == 1
    n = 2 * tq
    ku = (lambda u: u // 2) if diff else (lambda u: u)
    in_specs = [pl.BlockSpec((1, 1, LANES, tq), lambda bi, u, i: (bi, ku(u), 0, i)),
                pl.BlockSpec((1, 1, nk, LANES), lambda bi, u, i: (bi, ku(u), 0, 0)),
                pl.BlockSpec((1, 1, nkb, V_ROWS, KEY_BLOCK), lambda bi, u, i: (bi, u, 0, 0, 0))]
    args = [qt, k, vt]
    if diff:
        in_specs += [pl.BlockSpec(lam_qk.shape, lambda bi, u, i: (0, 0)),
                     pl.BlockSpec((HEAD_DIM, 1), lambda bi, u, i: (0, 0))]
        args += [lam_qk, gain]
    return pl.pallas_call(
        functools.partial(_flash_body, diff=diff, tq=tq, nkb=nkb, lam_init=lam_init),
        grid=(b, units, s // tq),
        in_specs=in_specs,
        out_specs=pl.BlockSpec((1, tq, LANES), lambda bi, u, i: (bi, i, u)),
        out_shape=jax.ShapeDtypeStruct((b, s, units * LANES), BF16),
        scratch_shapes=[pltpu.VMEM((PIPE_SETS, KEY_BLOCK, n), F32), pltpu.VMEM((PIPE_SETS, KEY_BLOCK, n), BF16),
                        pltpu.VMEM((PIPE_SETS, 1, n), F32), pltpu.VMEM((PIPE_SETS, 1, n), F32),
                        pltpu.VMEM((1, n), F32), pltpu.VMEM((V_ROWS, n), F32)],
        compiler_params=_cparams("arbitrary", "arbitrary", "arbitrary"),
        name="flash_diff" if diff else "flash_gqa",
    )(*args)


SHIFT_MARGIN = 1.01
SHIFT_DENOM_FLOOR = 2.0 ** -90


def _shift_flash_body(qt_ref, k_ref, vt_ref, kmax_ref, *rest, diff, tq, nkb, lam_init):
    if diff:
        lamqk_ref, gain_ref, o_ref, den_ref, p_sc, acc_sc = rest
    else:
        o_ref, den_ref, p_sc, acc_sc = rest
    row = lax.broadcasted_iota(jnp.int32, (LANES, tq), 0)
    if diff:
        base = (pl.program_id(1) % 2) * HEAD_DIM
        mask0 = (row >= base) & (row < base + C_QK_DIM)
        mask1 = (row >= base + C_QK_DIM) & (row < base + 2 * C_QK_DIM)
    else:
        mask0 = row < HEAD_DIM
        mask1 = row >= HEAD_DIM
    qt = qt_ref[0, 0].astype(F32)
    heads = [jnp.where(mask0, qt, 0.0), jnp.where(mask1, qt, 0.0)]
    qst = jnp.concatenate(heads, axis=1).astype(BF16)
    kmax = kmax_ref[0, 0]
    shift = jnp.concatenate(
        [jnp.sqrt(jnp.sum(hq * hq, axis=0, keepdims=True)) * kmax[i:i + 1, 0:1] for i, hq in enumerate(heads)],
        axis=1) * SHIFT_MARGIN

    acc_sc[...] = jnp.zeros(acc_sc.shape, F32)
    for slot in range(PIPE_SETS):
        p_sc[slot] = jnp.zeros(p_sc.shape[1:], BF16)
    last = nkb - 1

    def values(j, slot):
        acc_sc[...] += jnp.dot(vt_ref[0, 0, j], p_sc[slot], preferred_element_type=F32)

    def probs(j, slot):
        k = k_ref[0, 0, pl.ds(pl.multiple_of(j * KEY_BLOCK, KEY_BLOCK), KEY_BLOCK), :]
        s = jnp.dot(k, qst, preferred_element_type=F32)
        p_sc[slot] = jnp.exp2(s - shift).astype(BF16)

    def step(j, slot):
        values(jnp.maximum(j - PIPE_SETS, 0), slot)
        probs(j, slot)

    def trip(i, carry):
        for r in range(STEPS_PER_TRIP):
            step(STEPS_PER_TRIP * i + r, r % PIPE_SETS)
        return carry

    lax.fori_loop(0, last // STEPS_PER_TRIP, trip, 0)
    step(last, 0)
    if last >= 1:
        values(last - 1, 1)
    values(last, 0)

    acc = acc_sc[...]
    den = acc[HEAD_DIM:HEAD_DIM + 1]
    den_ref[0, 0] = jnp.concatenate([den[:, :tq], den[:, tq:]], axis=0)
    o = acc[0:HEAD_DIM] / den
    o0 = o[:, :tq]
    o1 = o[:, tq:]
    if diff:
        lq = lamqk_ref[...]
        lam = (jnp.exp(jnp.sum(lq[0:1] * lq[1:2], axis=1, keepdims=True))
               - jnp.exp(jnp.sum(lq[2:3] * lq[3:4], axis=1, keepdims=True)) + lam_init)
        dlt = o0 - lam * o1
        ms = jnp.mean(dlt * dlt, axis=0, keepdims=True)
        y = dlt * lax.rsqrt(ms + 1e-6) * gain_ref[...] * (1.0 - lam_init)
        out_t = jnp.concatenate([y, jnp.zeros_like(y)], axis=0)
    else:
        out_t = jnp.concatenate([o0, o1], axis=0)
    o_ref[0] = out_t.T.astype(o_ref.dtype)


def _shift_flash(qt, k, vt, kmax, *, diff, tq, lam_qk=None, gain=None, lam_init=0.0):
    b, _, _, s = qt.shape
    units, nkb = vt.shape[1:3]
    nk = k.shape[2]
    assert nkb * KEY_BLOCK == nk and nkb % STEPS_PER_TRIP == 1
    n = 2 * tq
    ku = (lambda u: u // 2) if diff else (lambda u: u)
    in_specs = [pl.BlockSpec((1, 1, LANES, tq), lambda bi, u, i: (bi, ku(u), 0, i)),
                pl.BlockSpec((1, 1, nk, LANES), lambda bi, u, i: (bi, ku(u), 0, 0)),
                pl.BlockSpec((1, 1, nkb, V_ROWS, KEY_BLOCK), lambda bi, u, i: (bi, u, 0, 0, 0)),
                pl.BlockSpec((1, 1, 2, LANES), lambda bi, u, i: (bi, u, 0, 0))]
    args = [qt, k, vt, kmax]
    if diff:
        in_specs += [pl.BlockSpec(lam_qk.shape, lambda bi, u, i: (0, 0)),
                     pl.BlockSpec((HEAD_DIM, 1), lambda bi, u, i: (0, 0))]
        args += [lam_qk, gain]
    return pl.pallas_call(
        functools.partial(_shift_flash_body, diff=diff, tq=tq, nkb=nkb, lam_init=lam_init),
        grid=(b, units, s // tq),
        in_specs=in_specs,
        out_specs=[pl.BlockSpec((1, tq, LANES), lambda bi, u, i: (bi, i, u)),
                   pl.BlockSpec((1, 1, 2, tq), lambda bi, u, i: (bi, u, 0, i))],
        out_shape=[jax.ShapeDtypeStruct((b, s, units * LANES), BF16),
                   jax.ShapeDtypeStruct((b, units, 2, s), F32)],
        scratch_shapes=[pltpu.VMEM((PIPE_SETS, KEY_BLOCK, n), BF16), pltpu.VMEM((V_ROWS, n), F32)],
        compiler_params=_cparams("arbitrary", "arbitrary", "arbitrary"),
        name="shift_flash_diff" if diff else "shift_flash_gqa",
    )(*args)


def _key_norm_max(k, width):
    kf = k.astype(F32)
    sq = jnp.sum((kf * kf).reshape(k.shape[:3] + (LANES // width, width)), axis=-1)
    return jnp.sqrt(jnp.max(sq, axis=2))


def _attend(qt, k, vt, kmax, *, diff, tq, **kw):
    y, den = _shift_flash(qt, k, vt, kmax, diff=diff, tq=tq, **kw)
    ok = jnp.all(den >= SHIFT_DENOM_FLOOR)
    return lax.cond(ok, lambda: y, lambda: _flash(qt, k, vt, diff=diff, tq=tq, **kw))


def _mixout_body(x_ref, mod_ref, ya_ref, yc_ref, pdu_ref, prev_ref, next_ref, poolw_ref, pscale_ref,
                 sguw_ref, sgub_ref, wout_ref, gpost_ref, gpre_ref, xo_ref, tok_ref, *, tm, n):
    i = pl.program_id(1)
    nt = pl.num_programs(1)
    mod = mod_ref[0]
    pdu = pdu_ref[0]
    pb = pdu[:, 0:256]
    du = pdu[:, 256:512]
    dv = pdu[:, 512:768]

    prev = jnp.where(i > 0, prev_ref[0], 0.0)
    nxt = jnp.where(i < nt - 1, next_ref[0], 0.0)
    ext = jnp.concatenate([prev, pb, nxt], axis=0)
    rows = tm + 16
    up = lambda a, k: pltpu.roll(a, rows - k, 0)
    s2 = ext + up(ext, 1)
    s4 = s2 + up(s2, 2)
    s8 = s4 + up(s4, 4)
    s16 = s8 + up(s8, 8)
    lane = lax.broadcasted_iota(jnp.int32, (tm, 256), 1)
    grp = lane // 64
    win = jnp.where(grp == 0, up(s2, 7)[0:tm],
                    jnp.where(grp == 1, up(s4, 6)[0:tm],
                              jnp.where(grp == 2, up(s8, 4)[0:tm], s16[0:tm])))
    tok_idx = i * tm + lax.broadcasted_iota(jnp.int32, (tm, 256), 0)
    half = jnp.left_shift(1, grp)
    cnt = jnp.minimum(tok_idx + half, n) - jnp.maximum(tok_idx - half, 0)
    pooled = win / cnt.astype(F32) - pb
    pool = jnp.dot(pooled.astype(BF16), poolw_ref[...], preferred_element_type=F32) * pscale_ref[...]

    mu = jnp.mean(dv, axis=1, keepdims=True)
    dc = dv - mu
    vln = (dc * lax.rsqrt(jnp.mean(dc * dc, axis=1, keepdims=True) + 1e-5)).astype(BF16)
    head = lax.broadcasted_iota(jnp.int32, (CHUNK, 256), 1) // 64
    svs = []
    for c in range(tm // CHUNK):
        vch = vln[c * CHUNK:(c + 1) * CHUNK]
        sv = sgub_ref[...]
        for hd in range(4):
            r = jnp.dot(sguw_ref[hd], vch, preferred_element_type=F32)
            sv = sv + jnp.where(head == hd, r, 0.0)
        svs.append(sv)
    sgu = du * jnp.concatenate(svs, axis=0)

    ycat = jnp.concatenate([ya_ref[0], pool.astype(BF16), yc_ref[0], sgu.astype(BF16)], axis=1)
    o = jnp.dot(ycat, wout_ref[...], preferred_element_type=F32)
    xn = x_ref[0] + mod[2:3] * (_rms(o) * gpost_ref[...])
    xo_ref[0] = xn
    tok_ref[0] = _rms(xn) * gpre_ref[...] * (1.0 + mod[4:5]) + mod[3:4]


def _mixout(x, mod, ya, yc, pdu, poolw, pscale, sguw, sgub, wout, gpost, gpre, *, tm):
    b, s, d = x.shape
    t8 = tm // 8
    last8 = s // 8 - 1
    full = lambda shape: pl.BlockSpec(shape, lambda bi, i: (0,) * len(shape))
    row = lambda w: pl.BlockSpec((1, tm, w), lambda bi, i: (bi, i, 0))
    return pl.pallas_call(
        functools.partial(_mixout_body, tm=tm, n=s),
        grid=(b, s // tm),
        in_specs=[row(d), pl.BlockSpec((1, 6, d), lambda bi, i: (bi, 0, 0)),
                  row(256), row(512), row(768),
                  pl.BlockSpec((1, 8, 256), lambda bi, i: (bi, jnp.maximum(i * t8 - 1, 0), 0)),
                  pl.BlockSpec((1, 8, 256), lambda bi, i: (bi, jnp.minimum((i + 1) * t8, last8), 0)),
                  full((256, 256)), full((1, 256)), full(sguw.shape), full((CHUNK, 256)),
                  full(wout.shape), full((1, d)), full((1, d))],
        out_specs=[row(d), row(d)],
        out_shape=[jax.ShapeDtypeStruct((b, s, d), F32), jax.ShapeDtypeStruct((b, s, d), F32)],
        compiler_params=_cparams("arbitrary", "arbitrary"),
        name="mixout",
    )(x, mod, ya, yc, pdu, pdu, pdu, poolw, pscale, sguw, sgub, wout, gpost, gpre)


def _route(sel, scores, tm):
    per = N_EXPERTS // N_EXPERT_GROUPS
    i8 = lax.broadcasted_iota(jnp.int32, (per, tm), 0)
    gsc = []
    for g in range(N_EXPERT_GROUPS):
        blk = sel[g * per:(g + 1) * per]
        m1 = jnp.max(blk, axis=0, keepdims=True)
        i1 = jnp.min(jnp.where(blk == m1, i8, per), axis=0, keepdims=True)
        m2 = jnp.max(jnp.where(i8 == i1, NEG_INF, blk), axis=0, keepdims=True)
        gsc.append(m1 + m2)
    gs = jnp.concatenate(gsc, axis=0)
    g8 = lax.broadcasted_iota(jnp.int32, (N_EXPERT_GROUPS, tm), 0)
    gmask = jnp.zeros((N_EXPERT_GROUPS, tm), F32)
    for _ in range(TOPK_GROUPS):
        gm = jnp.max(gs, axis=0, keepdims=True)
        gi = jnp.min(jnp.where(gs == gm, g8, N_EXPERT_GROUPS), axis=0, keepdims=True)
        hit = g8 == gi
        gmask = jnp.where(hit, 1.0, gmask)
        gs = jnp.where(hit, NEG_INF, gs)
    ms = jnp.concatenate(
        [jnp.where(gmask[g:g + 1] > 0.0, sel[g * per:(g + 1) * per], NEG_INF) for g in range(N_EXPERT_GROUPS)],
        axis=0)
    e64 = lax.broadcasted_iota(jnp.int32, (N_EXPERTS, tm), 0)
    chosen = jnp.zeros((N_EXPERTS, tm), F32)
    for _ in range(TOP_K):
        m = jnp.max(ms, axis=0, keepdims=True)
        ii = jnp.min(jnp.where(ms == m, e64, N_EXPERTS), axis=0, keepdims=True)
        hit = e64 == ii
        chosen = jnp.where(hit, 1.0, chosen)
        ms = jnp.where(hit, NEG_INF, ms)
    w = chosen * scores
    return w / jnp.sum(w, axis=0, keepdims=True) * ROUTED_SCALE


def _moe_body(tok_ref, x_ref, mod_ref, rw_ref, rb_ref, w1_ref, w3_ref, w2_ref, s1_ref, s3_ref, s2_ref,
              gpost_ref, o_ref, hb_sc, gates_sc, acc_sc, *, tm):
    e = pl.program_id(2)

    @pl.when(e == 0)
    def _():
        h = tok_ref[0]
        hb = h.astype(BF16)
        hb_sc[...] = hb
        logits = jnp.dot(h, rw_ref[...], precision=lax.Precision.HIGHEST, preferred_element_type=F32)
        lt = logits.T[0:N_EXPERTS]
        scores = jax.nn.sigmoid(lt)
        gates_t = _route(scores + rb_ref[...], scores, tm)
        gates_sc[...] = jnp.concatenate([gates_t, jnp.zeros_like(gates_t)], axis=0).T
        a = jnp.dot(hb, s1_ref[...], preferred_element_type=F32)
        g = jnp.dot(hb, s3_ref[...], preferred_element_type=F32)
        acc_sc[...] = jnp.dot((_silu(a) * g).astype(BF16), s2_ref[...], preferred_element_type=F32)

    hb = hb_sc[...]
    lane = lax.broadcasted_iota(jnp.int32, (tm, LANES), 1)
    gates = gates_sc[...]
    acts = []
    for i in range(EXPERTS_PER_STEP):
        gate = jnp.sum(jnp.where(lane == e * EXPERTS_PER_STEP + i, gates, 0.0), axis=1, keepdims=True)
        a = jnp.dot(hb, w1_ref[0, i].astype(BF16), preferred_element_type=F32)
        g = jnp.dot(hb, w3_ref[0, i].astype(BF16), preferred_element_type=F32)
        acts.append((_silu(a) * g * gate).astype(BF16))
    hid = w2_ref.shape[2]
    w2 = w2_ref[0].reshape(EXPERTS_PER_STEP * hid, w2_ref.shape[3]).astype(BF16)
    acc_sc[...] += jnp.dot(jnp.concatenate(acts, axis=1), w2, preferred_element_type=F32)

    @pl.when(e == pl.num_programs(2) - 1)
    def _():
        mod = mod_ref[0]
        o_ref[0] = x_ref[0] + mod[5:6] * (_rms(acc_sc[...]) * gpost_ref[...])


def _moe(tok, x, mod, rw, rb, w1, w3, w2, s1, s3, s2, gpost, *, layer, tm):
    b, s, d = x.shape
    ne, _, hid = w1.shape[1:]
    full = lambda shape: pl.BlockSpec(shape, lambda bi, i, e: (0,) * len(shape))
    row = pl.BlockSpec((1, tm, d), lambda bi, i, e: (bi, i, 0))
    return pl.pallas_call(
        functools.partial(_moe_body, tm=tm),
        grid=(b, s // tm, ne // EXPERTS_PER_STEP),
        in_specs=[row, row, pl.BlockSpec((1, 6, d), lambda bi, i, e: (bi, 0, 0)),
                  full((d, LANES)), full((N_EXPERTS, 1)),
                  pl.BlockSpec((1, EXPERTS_PER_STEP, d, hid), lambda bi, i, e: (layer, e, 0, 0)),
                  pl.BlockSpec((1, EXPERTS_PER_STEP, d, hid), lambda bi, i, e: (layer, e, 0, 0)),
                  pl.BlockSpec((1, EXPERTS_PER_STEP, hid, d), lambda bi, i, e: (layer, e, 0, 0)),
                  full(s1.shape), full(s3.shape), full(s2.shape), full((1, d))],
        out_specs=row,
        out_shape=jax.ShapeDtypeStruct((b, s, d), F32),
        scratch_shapes=[pltpu.VMEM((tm, d), BF16), pltpu.VMEM((tm, LANES), F32), pltpu.VMEM((tm, d), F32)],
        compiler_params=_cparams("arbitrary", "arbitrary", "arbitrary"),
        name="moe",
    )(tok, x, mod, rw, rb, w1, w3, w2, s1, s3, s2, gpost)


SPARSE_TILE = 512
SC_WINDOW = 128
PIECE = 256
HI16 = 0xFFFF0000


def _pack_pair(lo, hi):
    def rne(x):
        u = lax.bitcast_convert_type(x, jnp.uint32)
        return u + jnp.uint32(0x7FFF) + ((u >> 16) & jnp.uint32(1))
    word = (rne(hi) & jnp.uint32(HI16)) | (rne(lo) >> 16)
    return lax.bitcast_convert_type(word, jnp.int32)


def _unpack_pair(word):
    u = lax.bitcast_convert_type(word, jnp.uint32)
    return (lax.bitcast_convert_type(u << 16, F32), lax.bitcast_convert_type(u & jnp.uint32(HI16), F32))


def _pack_row(x):
    return [_pack_pair(x[:, 2 * p * PIECE:(2 * p + 1) * PIECE], x[:, (2 * p + 1) * PIECE:(2 * p + 2) * PIECE])
            for p in range(2)]


def _unpack_row(p0, p1):
    return jnp.concatenate(_unpack_pair(p0) + _unpack_pair(p1), axis=1)


def _sc_mesh():
    from jax.experimental.pallas import tpu_sc as plsc
    return plsc.VectorSubcoreMesh(core_axis_name="core", subcore_axis_name="subcore")


def _sc_gather(table, idx):
    nb = idx.shape[0]
    d = table.shape[1]
    assert nb % SC_WINDOW == 0

    @functools.partial(pl.kernel, out_type=jax.ShapeDtypeStruct((nb, d), table.dtype), mesh=_sc_mesh())
    def gather_kernel(x_hbm, i_hbm, o_hbm):
        def body(i_vmem, o_vmem):
            pltpu.sync_copy(x_hbm.at[i_vmem.at[0]], o_vmem)

        pltpu.emit_pipeline(
            body,
            grid=(nb // SC_WINDOW,),
            in_specs=[pl.BlockSpec((1, SC_WINDOW), lambda i: (0, i))],
            out_specs=[pl.BlockSpec((SC_WINDOW, d), lambda i: (i, 0))],
            core_axis_name=("core", "subcore"),
            dimension_semantics=(pltpu.PARALLEL,),
        )(i_hbm, o_hbm)

    return gather_kernel(table, idx.reshape(1, nb))


def _sc_scatter(x, idx, out_rows):
    rounds, m = idx.shape
    d = x.shape[1]
    nblk = m // SC_WINDOW
    assert m % SC_WINDOW == 0

    @functools.partial(pl.kernel, out_type=jax.ShapeDtypeStruct((out_rows, d), x.dtype), mesh=_sc_mesh())
    def scatter_kernel(x_hbm, i_hbm, o_hbm):
        def body(x_vmem, i_vmem):
            pltpu.sync_copy(x_vmem, o_hbm.at[i_vmem.at[0]])

        pltpu.emit_pipeline(
            body,
            grid=(rounds * nblk,),
            in_specs=[pl.BlockSpec((SC_WINDOW, d), lambda i: (i % nblk, 0)),
                      pl.BlockSpec((1, SC_WINDOW), lambda i: (0, i))],
            out_specs=[],
            core_axis_name=("core", "subcore"),
            dimension_semantics=(pltpu.PARALLEL,),
        )(x_hbm, i_hbm)

    return scatter_kernel(x, idx.reshape(1, rounds * m))


def _topk_route(sel, scores, tm):
    per = N_EXPERTS // N_EXPERT_GROUPS
    i8 = lax.broadcasted_iota(jnp.int32, (per, tm), 0)
    gsc = []
    for g in range(N_EXPERT_GROUPS):
        blk = sel[g * per:(g + 1) * per]
        m1 = jnp.max(blk, axis=0, keepdims=True)
        i1 = jnp.min(jnp.where(blk == m1, i8, per), axis=0, keepdims=True)
        m2 = jnp.max(jnp.where(i8 == i1, NEG_INF, blk), axis=0, keepdims=True)
        gsc.append(m1 + m2)
    gs = jnp.concatenate(gsc, axis=0)
    g8 = lax.broadcasted_iota(jnp.int32, (N_EXPERT_GROUPS, tm), 0)
    gmask = jnp.zeros((N_EXPERT_GROUPS, tm), F32)
    for _ in range(TOPK_GROUPS):
        gm = jnp.max(gs, axis=0, keepdims=True)
        gi = jnp.min(jnp.where(gs == gm, g8, N_EXPERT_GROUPS), axis=0, keepdims=True)
        hit = g8 == gi
        gmask = jnp.where(hit, 1.0, gmask)
        gs = jnp.where(hit, NEG_INF, gs)
    ms = jnp.concatenate(
        [jnp.where(gmask[g:g + 1] > 0.0, sel[g * per:(g + 1) * per], NEG_INF) for g in range(N_EXPERT_GROUPS)],
        axis=0)
    e64 = lax.broadcasted_iota(jnp.int32, (N_EXPERTS, tm), 0)
    hits, ids = [], []
    for _ in range(TOP_K):
        m = jnp.max(ms, axis=0, keepdims=True)
        ii = jnp.min(jnp.where(ms == m, e64, N_EXPERTS), axis=0, keepdims=True)
        hit = e64 == ii
        hits.append(hit)
        ids.append(ii)
        ms = jnp.where(hit, NEG_INF, ms)
    return hits, ids


def _route_body(tok_ref, rw_ref, rb_ref, tri_ref, tokp_ref, eidx_ref, posk_ref, wts_ref, cnt_ref, run_sc, *, tm):
    first = (pl.program_id(0) == 0) & (pl.program_id(1) == 0)

    @pl.when(first)
    def _():
        run_sc[...] = jnp.zeros(run_sc.shape, F32)

    h = tok_ref[0]
    pieces = _pack_row(h)
    tokp_ref[0] = pieces[0]
    tokp_ref[1] = pieces[1]

    logits = jnp.dot(h, rw_ref[...], precision=lax.Precision.HIGHEST, preferred_element_type=F32)
    scores = jax.nn.sigmoid(logits.T[0:N_EXPERTS])
    hits, ids = _topk_route(scores + rb_ref[...], scores, tm)
    raw = [jnp.sum(jnp.where(hit, scores, 0.0), axis=0, keepdims=True) for hit in hits]
    denom = raw[0]
    for r in raw[1:]:
        denom = denom + r
    wts = jnp.concatenate([r / denom * ROUTED_SCALE for r in raw], axis=0)
    wts_ref[...] = jnp.concatenate([wts, jnp.zeros((LANES - TOP_K, tm), F32)], axis=0).T

    chosen = jnp.zeros((N_EXPERTS, tm), F32)
    for hit in hits:
        chosen = jnp.where(hit, 1.0, chosen)
    incl = jnp.dot(chosen.astype(BF16), tri_ref[...], preferred_element_type=F32)
    before = run_sc[...] + incl - chosen
    posk_ref[...] = jnp.concatenate(
        [jnp.sum(jnp.where(hit, before, 0.0), axis=0, keepdims=True) for hit in hits], axis=0).astype(jnp.int32)
    eidx_ref[...] = jnp.concatenate(ids, axis=0)
    run_sc[...] = run_sc[...] + jnp.sum(chosen, axis=1, keepdims=True)
    cnt_ref[...] = jnp.broadcast_to(run_sc[...], cnt_ref.shape)


def _route_tokens(tok, rw, rb, tri, *, tm):
    b, s, d = tok.shape
    n = b * s
    nt = s // tm
    full = lambda shape: pl.BlockSpec(shape, lambda bi, i: (0,) * len(shape))
    col = lambda rows: pl.BlockSpec((rows, tm), lambda bi, i: (0, bi * nt + i))
    return pl.pallas_call(
        functools.partial(_route_body, tm=tm),
        grid=(b, nt),
        in_specs=[pl.BlockSpec((1, tm, d), lambda bi, i: (bi, i, 0)),
                  full((d, LANES)), full((N_EXPERTS, 1)), full((tm, tm))],
        out_specs=[pl.BlockSpec((2, tm, PIECE), lambda bi, i: (0, bi * nt + i, 0)),
                   col(TOP_K), col(TOP_K),
                   pl.BlockSpec((tm, LANES), lambda bi, i: (bi * nt + i, 0)),
                   full((N_EXPERTS, LANES))],
        out_shape=[jax.ShapeDtypeStruct((2, n, PIECE), jnp.int32),
                   jax.ShapeDtypeStruct((TOP_K, n), jnp.int32),
                   jax.ShapeDtypeStruct((TOP_K, n), jnp.int32),
                   jax.ShapeDtypeStruct((n, LANES), F32),
                   jax.ShapeDtypeStruct((N_EXPERTS, LANES), F32)],
        scratch_shapes=[pltpu.VMEM((N_EXPERTS, 1), F32)],
        compiler_params=_cparams("arbitrary", "arbitrary"),
        name="route",
    )(tok, rw, rb, tri)


def _ffn_body(te_ref, tv_ref, x_ref, w1_ref, w3_ref, w2_ref, y_ref):
    valid = tv_ref[pl.program_id(0)]

    @pl.when(valid > 0)
    def _():
        x = _unpack_row(x_ref[0], x_ref[1])
        rows = lax.broadcasted_iota(jnp.int32, x.shape, 0)
        xb = jnp.where(rows < valid, x, 0.0).astype(BF16)
        a = jnp.dot(xb, w1_ref[0, 0].astype(BF16), preferred_element_type=F32)
        g = jnp.dot(xb, w3_ref[0, 0].astype(BF16), preferred_element_type=F32)
        y = jnp.dot((_silu(a) * g).astype(BF16), w2_ref[0, 0].astype(BF16), preferred_element_type=F32)
        pieces = _pack_row(y)
        y_ref[0] = pieces[0]
        y_ref[1] = pieces[1]


def _expert_ffn(tile_expert, tile_valid, xs, w1, w3, w2, *, layer):
    _, rows, _ = xs.shape
    d, hid = w1.shape[2:]
    blk = pl.BlockSpec((2, SPARSE_TILE, PIECE), lambda i, te, tv: (0, i, 0))
    return pl.pallas_call(
        _ffn_body,
        grid_spec=pltpu.PrefetchScalarGridSpec(
            num_scalar_prefetch=2,
            grid=(rows // SPARSE_TILE,),
            in_specs=[blk,
                      pl.BlockSpec((1, 1, d, hid), lambda i, te, tv: (layer, te[i], 0, 0)),
                      pl.BlockSpec((1, 1, d, hid), lambda i, te, tv: (layer, te[i], 0, 0)),
                      pl.BlockSpec((1, 1, hid, d), lambda i, te, tv: (layer, te[i], 0, 0))],
            out_specs=blk),
        out_shape=jax.ShapeDtypeStruct(xs.shape, jnp.int32),
        compiler_params=_cparams("arbitrary"),
        name="expert_ffn",
    )(tile_expert, tile_valid, xs, w1, w3, w2)


def _combine_body(tok_ref, x_ref, mod_ref, yg_ref, wts_ref, s1_ref, s3_ref, s2_ref, gpost_ref, o_ref):
    hb = tok_ref[0].astype(BF16)
    a = jnp.dot(hb, s1_ref[...], preferred_element_type=F32)
    g = jnp.dot(hb, s3_ref[...], preferred_element_type=F32)
    f = jnp.dot((_silu(a) * g).astype(BF16), s2_ref[...], preferred_element_type=F32)
    wts = wts_ref[...]
    for k in range(TOP_K):
        f = f + wts[:, k:k + 1] * _unpack_row(yg_ref[0, k], yg_ref[1, k])
    o_ref[0] = x_ref[0] + mod_ref[0][5:6] * (_rms(f) * gpost_ref[...])


def _combine(tok, x, mod, yg, wts, s1, s3, s2, gpost, *, tm):
    b, s, d = x.shape
    nt = s // tm
    full = lambda shape: pl.BlockSpec(shape, lambda bi, i: (0,) * len(shape))
    row = pl.BlockSpec((1, tm, d), lambda bi, i: (bi, i, 0))
    return pl.pallas_call(
        _combine_body,
        grid=(b, nt),
        in_specs=[row, row, pl.BlockSpec((1, 6, d), lambda bi, i: (bi, 0, 0)),
                  pl.BlockSpec((2, TOP_K, tm, PIECE), lambda bi, i: (0, 0, bi * nt + i, 0)),
                  pl.BlockSpec((tm, LANES), lambda bi, i: (bi * nt + i, 0)),
                  full(s1.shape), full(s3.shape), full(s2.shape), full((1, d))],
        out_specs=row,
        out_shape=jax.ShapeDtypeStruct((b, s, d), F32),
        compiler_params=_cparams("arbitrary", "arbitrary"),
        name="combine",
    )(tok, x, mod, yg, wts, s1, s3, s2, gpost)


def _sparse_moe(tok, x, mod, rw, rb, w1, w3, w2, s1, s3, s2, gpost, *, layer, tm_route, tm_combine):
    b, s, d = x.shape
    n = b * s
    rows = n * TOP_K + N_EXPERTS * SPARSE_TILE
    ntiles = rows // SPARSE_TILE
    tri = jnp.triu(jnp.ones((tm_route, tm_route), BF16))
    tokp, eidx, posk, wts, cnt = _route_tokens(tok, rw, rb, tri, tm=tm_route)

    cnt = cnt[:, 0].astype(jnp.int32)
    padded = (cnt + SPARSE_TILE - 1) // SPARSE_TILE * SPARSE_TILE
    ends = jnp.cumsum(padded)
    offs = ends - padded
    experts = jnp.arange(N_EXPERTS, dtype=jnp.int32)
    dest = posk + jnp.sum(jnp.where(eidx[None] == experts[:, None, None], offs[:, None, None], 0), axis=0)
    tile_start = jnp.arange(ntiles, dtype=jnp.int32) * SPARSE_TILE
    tile_expert = jnp.minimum(jnp.sum(tile_start[:, None] >= ends[None, :], axis=1), N_EXPERTS - 1).astype(jnp.int32)
    tile_valid = jnp.clip(cnt[tile_expert] - (tile_start - offs[tile_expert]), 0, SPARSE_TILE).astype(jnp.int32)

    piece_base = jnp.arange(2, dtype=jnp.int32) * rows
    sidx = (dest[:, None, :] + piece_base[None, :, None]).reshape(TOP_K, 2 * n)
    xs = _sc_scatter(tokp.reshape(2 * n, PIECE), sidx, 2 * rows).reshape(2, rows, PIECE)
    ys = _expert_ffn(tile_expert, tile_valid, xs, w1, w3, w2, layer=layer)
    gidx = (piece_base[:, None, None] + dest[None, :, :]).reshape(2 * TOP_K * n)
    yg = _sc_gather(ys.reshape(2 * rows, PIECE), gidx).reshape(2, TOP_K, n, PIECE)
    return _combine(tok, x, mod, yg, wts, s1, s3, s2, gpost, tm=tm_combine)


def _rope_tables(s, dim):
    rows = s // GRID_W
    row = jnp.repeat(jnp.arange(rows, dtype=F32), GRID_W)
    col = jnp.tile(jnp.arange(GRID_W, dtype=F32), rows)
    half = dim // 2
    inv = ROPE_THETA ** (-jnp.arange(0, half, 2, dtype=F32) / half)
    ar = row[:, None] * inv[None, :]
    ac = col[:, None] * inv[None, :]
    ang = jnp.concatenate([ar, ar, ac, ac], axis=-1)
    sign = jnp.where((jnp.arange(dim) & (dim // 4)) == 0, -1.0, 1.0).astype(F32)
    reps = LANES // dim
    return jnp.tile(jnp.cos(ang), (1, reps)), jnp.tile(jnp.sin(ang) * sign, (1, reps))


def _block_diag(blocks):
    n = len(blocks)
    r, c = blocks[0].shape
    out = jnp.zeros((n * r, n * c), blocks[0].dtype)
    for i, blk in enumerate(blocks):
        out = out.at[i * r:(i + 1) * r, i * c:(i + 1) * c].set(blk)
    return out


def kernel(x, c, ctx, c_ctx, ada_w, ada_b, g_pre_mix, g_post_mix, g_pre_ffn, g_post_ffn, w_in, w_out, a_q_gain, a_k_gain, pool_w, pool_scale, lam_qk, c_subln_gain, sgu_w, sgu_b, router_w, router_bias, exp_w1, exp_w3, exp_w2, sh_w1, sh_w3, sh_w2):
    b, s, d = x.shape
    nctx = ctx.shape[1]
    depth = ada_w.shape[0]
    tm_lat = 512
    tq = 512
    tm_moe = 1024

    cvec = jnp.zeros((8, d), F32).at[0:b].set(c).at[b].set(c_ctx)
    mods = _ada(cvec, ada_w, ada_b)

    tabs = _rope_tables(s, HEAD_DIM) + _rope_tables(s, C_QK_DIM)
    bd = _block_diag([jnp.ones((HEAD_DIM, HEAD_DIM), BF16)] * 4)
    row2 = lambda v: v.reshape(1, -1)

    group = b
    return jnp.concatenate(
        [_layers(x[lo:lo + group], ctx[lo:lo + group], mods, lo, b, tabs, bd, g_pre_mix, g_post_mix, g_pre_ffn,
                 g_post_ffn, w_in, w_out, a_q_gain, a_k_gain, pool_w, pool_scale, lam_qk, c_subln_gain, sgu_w,
                 sgu_b, router_w, router_bias, exp_w1, exp_w3, exp_w2, sh_w1, sh_w3, sh_w2)
         for lo in range(0, b, group)], axis=0)


def _layers(xl, xc, mods, lo, ctx_row, tabs, bd, g_pre_mix, g_post_mix, g_pre_ffn, g_post_ffn, w_in, w_out,
            a_q_gain, a_k_gain, pool_w, pool_scale, lam_qk, c_subln_gain, sgu_w, sgu_b, router_w, router_bias,
            exp_w1, exp_w3, exp_w2, sh_w1, sh_w3, sh_w2):
    b, s, d = xl.shape
    nctx = xc.shape[1]
    depth = w_in.shape[0]
    tm_lat = 512
    tq = 512
    tm_moe = 1024
    row2 = lambda v: v.reshape(1, -1)
    for l in range(depth):
        need_ctx = l < depth - 1
        lam_init = 0.8 - 0.6 * math.exp(-0.3 * l)
        m6 = mods[l].reshape(8, 6, d)
        mod_l = m6[lo:lo + b]
        mod_c = jnp.broadcast_to(m6[ctx_row:ctx_row + 1], (b, 6, d))

        w_in_l = w_in[l].astype(BF16)
        qg = jnp.tile(a_q_gain[l], 4).reshape(1, 256)
        kg = jnp.tile(a_k_gain[l], 2).reshape(1, LANES)
        inproj = functools.partial(_inproj, g_pre=row2(g_pre_mix[l]), w_in=w_in_l, q_gain=qg, k_gain=kg,
                                   tabs=tabs, bd=bd)
        qat_l, ka_l, vat_l, qct_l, kc_l, vct_l, pdu_l = inproj(xl, mod_l, rope=True, tm=tm_lat)
        qat_c, ka_c, vat_c, qct_c, kc_c, vct_c, pdu_c = inproj(xc, mod_c, rope=False, tm=nctx)

        ka = jnp.concatenate([ka_c, ka_l], axis=2)
        vat = jnp.concatenate([vat_c, vat_l], axis=2)
        kc = jnp.concatenate([kc_c, kc_l], axis=2)
        vct = jnp.concatenate([vct_c, vct_l], axis=2)
        sub_gain = c_subln_gain[l].reshape(HEAD_DIM, 1)

        lanes = lambda v: jnp.broadcast_to(v[..., None], v.shape + (LANES,))
        kmax_a = lanes(_key_norm_max(ka, HEAD_DIM))
        kmax_c = lanes(_key_norm_max(kc, C_QK_DIM).reshape(b, 4, 2))
        ya_l = _attend(qat_l, ka, vat, kmax_a, diff=False, tq=tq)
        yc_l = _attend(qct_l, kc, vct, kmax_c, diff=True, tq=tq, lam_qk=lam_qk[l], gain=sub_gain,
                       lam_init=lam_init)

        poolw = _block_diag([pool_w[l, g] for g in range(len(POOL_WINDOWS))]).astype(BF16)
        sgub = jnp.repeat(jnp.transpose(sgu_b[l]), d // 16, axis=1)
        wo = w_out[l]
        wo_c = jnp.pad(wo[512:768].reshape(4, HEAD_DIM, d), ((0, 0), (0, LANES - HEAD_DIM), (0, 0)))
        wout = jnp.concatenate([wo[0:512], wo_c.reshape(4 * LANES, d), wo[768:1024]], axis=0).astype(BF16)
        mixout = functools.partial(_mixout, poolw=poolw, pscale=row2(pool_scale[l]), sguw=sgu_w[l].astype(BF16),
                                   sgub=sgub, wout=wout, gpost=row2(g_post_mix[l]), gpre=row2(g_pre_ffn[l]))
        rw = jnp.pad(router_w[l], ((0, 0), (0, LANES - N_EXPERTS)))
        moe_args = dict(rw=rw, rb=router_bias[l].reshape(N_EXPERTS, 1), w1=exp_w1, w3=exp_w3,
                        w2=exp_w2, s1=sh_w1[l].astype(BF16), s3=sh_w3[l].astype(BF16),
                        s2=sh_w2[l].astype(BF16), gpost=row2(g_post_ffn[l]), layer=l)
        moe = functools.partial(_moe, **moe_args)

        xl_mid, tok_l = mixout(xl, mod_l, ya_l, yc_l, pdu_l, tm=tm_lat)
        xl = _sparse_moe(tok_l, xl_mid, mod_l, tm_route=tm_moe, tm_combine=tm_lat, **moe_args)
        if need_ctx:
            ya_c = _flash(qat_c, ka_c, vat_c, diff=False, tq=nctx)
            yc_c = _flash(qct_c, kc_c, vct_c, diff=True, tq=nctx, lam_qk=lam_qk[l], gain=sub_gain,
                          lam_init=lam_init)
            xc_mid, tok_c = mixout(xc, mod_c, ya_c, yc_c, pdu_c, tm=nctx)
            xc = moe(tok_c, xc_mid, mod_c, tm=nctx)
    return xl
```

```python
import functools
import math

import jax
import jax.numpy as jnp
from jax import lax
from jax.experimental import pallas as pl
from jax.experimental.pallas import tpu as pltpu

F32 = jnp.float32
BF16 = jnp.bfloat16

GRID_W = 64
ROPE_THETA = 10000.0
HEAD_DIM = 64
C_QK_DIM = 32
POOL_WINDOWS = (2, 4, 8, 16)
CHUNK = 128
N_EXPERTS = 64
TOP_K = 8
N_EXPERT_GROUPS = 8
TOPK_GROUPS = 4
ROUTED_SCALE = 2.5
EXPERTS_PER_STEP = 2

LANES = 128
KEY_BLOCK = 256
PIPE_SETS = 2
STEPS_PER_TRIP = 64
FALLBACK_STEPS_PER_TRIP = 16
V_ROWS = 80
VMEM_LIMIT = 56 * 1024 * 1024

NEG_INF = float("-inf")
LOG2E = math.log2(math.e)


def _cparams(*sem, flags=None):
    return pltpu.CompilerParams(dimension_semantics=sem, vmem_limit_bytes=VMEM_LIMIT, flags=flags)


def _rms(x, eps=1e-6):
    return x * lax.rsqrt(jnp.mean(x * x, axis=-1, keepdims=True) + eps)


def _segsum(sq, bd):
    hi = sq.astype(BF16)
    lo = (sq - hi.astype(F32)).astype(BF16)
    return (jnp.dot(hi, bd, preferred_element_type=F32)
            + jnp.dot(lo, bd, preferred_element_type=F32))


def _rope(x, cos, sin_signed, quarter):
    w = x.shape[1]
    lane = lax.broadcasted_iota(jnp.int32, x.shape, 1)
    first = (lane & quarter) == 0
    rot = jnp.where(first, pltpu.roll(x, w - quarter, 1), pltpu.roll(x, quarter, 1))
    return x * cos + rot * sin_signed


def _steps_per_trip(nkb, most):
    looped = max(nkb - 1, PIPE_SETS)
    return max([t for t in range(PIPE_SETS, most + 1, PIPE_SETS) if looped % t == 0], default=PIPE_SETS)


def _silu(x):
    return x * jax.nn.sigmoid(x)


def _ada_body(c_ref, w_ref, b_ref, o_ref):
    sc = _silu(c_ref[...])
    o_ref[0] = jnp.dot(sc, w_ref[0], precision=lax.Precision.HIGHEST,
                       preferred_element_type=F32) + b_ref[0]


def _ada(cvec, ada_w, ada_b):
    nl, d, d6 = ada_w.shape
    return pl.pallas_call(
        _ada_body,
        grid=(nl, d6 // d),
        in_specs=[pl.BlockSpec((8, d), lambda l, j: (0, 0)),
                  pl.BlockSpec((1, d, d), lambda l, j: (l, 0, j)),
                  pl.BlockSpec((1, 1, d), lambda l, j: (l, 0, j))],
        out_specs=pl.BlockSpec((1, 8, d), lambda l, j: (l, 0, j)),
        out_shape=jax.ShapeDtypeStruct((nl, 8, d6), F32),
        compiler_params=_cparams("arbitrary", "arbitrary"),
        name="ada",
    )(cvec, ada_w, ada_b.reshape(nl, 1, d6))


def _inproj_body(x_ref, mod_ref, g_ref, w_ref, qg_ref, kg_ref, ca_ref, sa_ref, cc_ref, sc_ref, bd_ref,
                 qat_ref, ka_ref, vat_ref, qct_ref, kc_ref, vct_ref, pdu_ref, *, rope, tm):
    x = x_ref[0]
    mod = mod_ref[0]
    h = _rms(x) * g_ref[...] * (1.0 + mod[1:2]) + mod[0:1]
    p = jnp.dot(h.astype(BF16), w_ref[...], preferred_element_type=F32)

    lane = lax.broadcasted_iota(jnp.int32, (tm, LANES), 1)
    low = lane < HEAD_DIM
    ones_col = (lane == HEAD_DIM).astype(F32)
    nkb = tm // KEY_BLOCK

    aq = p[:, 0:256]
    qn = aq * lax.rsqrt(_segsum(aq * aq, bd_ref[...]) * (1.0 / HEAD_DIM) + 1e-6) * qg_ref[...]
    if rope:
        ca = ca_ref[...]
        sa = sa_ref[...]
        qn = _rope(qn, jnp.concatenate([ca, ca], axis=1), jnp.concatenate([sa, sa], axis=1), HEAD_DIM // 4)
    qn = qn * (HEAD_DIM ** -0.5 * LOG2E)
    for kv in range(2):
        qat_ref[0, kv] = qn[:, kv * LANES:(kv + 1) * LANES].T.astype(BF16)

    ak = p[:, 256:384]
    kn = ak * lax.rsqrt(_segsum(ak * ak, bd_ref[0:LANES, 0:LANES]) * (1.0 / HEAD_DIM) + 1e-6) * kg_ref[...]
    if rope:
        kn = _rope(kn, ca_ref[...], sa_ref[...], HEAD_DIM // 4)
    ksw = pltpu.roll(kn, HEAD_DIM, 1)
    ka_ref[0, 0] = jnp.where(low, kn, ksw).astype(BF16)
    ka_ref[0, 1] = jnp.where(low, ksw, kn).astype(BF16)

    def store_vt(ref, unit, vext):
        for j in range(nkb):
            ref[0, unit, j] = vext[j * KEY_BLOCK:(j + 1) * KEY_BLOCK].T[0:V_ROWS].astype(BF16)

    av = p[:, 384:512]
    store_vt(vat_ref, 0, jnp.where(low, av, ones_col))
    store_vt(vat_ref, 1, jnp.where(low, pltpu.roll(av, HEAD_DIM, 1), ones_col))

    cq = p[:, 768:1024]
    ck = p[:, 1024:1280]
    if rope:
        cc = cc_ref[...]
        sc = sc_ref[...]
        cc2 = jnp.concatenate([cc, cc], axis=1)
        sc2 = jnp.concatenate([sc, sc], axis=1)
        cq = _rope(cq, cc2, sc2, C_QK_DIM // 4)
        ck = _rope(ck, cc2, sc2, C_QK_DIM // 4)
    cq = cq * (C_QK_DIM ** -0.5 * LOG2E)
    for pr in range(2):
        qct_ref[0, pr] = cq[:, pr * LANES:(pr + 1) * LANES].T.astype(BF16)
        kc_ref[0, pr] = ck[:, pr * LANES:(pr + 1) * LANES].astype(BF16)
    cv = p[:, 1280:1536]
    for hd in range(4):
        seg = cv[:, (hd // 2) * LANES:(hd // 2 + 1) * LANES]
        if hd % 2:
            seg = pltpu.roll(seg, HEAD_DIM, 1)
        store_vt(vct_ref, hd, jnp.where(low, seg, ones_col))

    pdu_ref[0, :, 0:256] = p[:, 512:768]
    pdu_ref[0, :, 256:768] = p[:, 1536:2048]


def _inproj(x, mod, g_pre, w_in, q_gain, k_gain, tabs, bd, *, rope, tm):
    b, s, d = x.shape
    nkb = s // KEY_BLOCK
    tkb = tm // KEY_BLOCK
    full = lambda shape: pl.BlockSpec(shape, lambda bi, i: (0,) * len(shape))
    tab = pl.BlockSpec((tm, LANES), lambda bi, i: (i, 0))
    return pl.pallas_call(
        functools.partial(_inproj_body, rope=rope, tm=tm),
        grid=(b, s // tm),
        in_specs=[pl.BlockSpec((1, tm, d), lambda bi, i: (bi, i, 0)),
                  pl.BlockSpec((1, 6, d), lambda bi, i: (bi, 0, 0)),
                  full((1, d)), full(w_in.shape), full((1, 256)), full((1, LANES)),
                  tab, tab, tab, tab, full((256, 256))],
        out_specs=[pl.BlockSpec((1, 2, LANES, tm), lambda bi, i: (bi, 0, 0, i)),
                   pl.BlockSpec((1, 2, tm, LANES), lambda bi, i: (bi, 0, i, 0)),
                   pl.BlockSpec((1, 2, tkb, V_ROWS, KEY_BLOCK), lambda bi, i: (bi, 0, i, 0, 0)),
                   pl.BlockSpec((1, 2, LANES, tm), lambda bi, i: (bi, 0, 0, i)),
                   pl.BlockSpec((1, 2, tm, LANES), lambda bi, i: (bi, 0, i, 0)),
                   pl.BlockSpec((1, 4, tkb, V_ROWS, KEY_BLOCK), lambda bi, i: (bi, 0, i, 0, 0)),
                   pl.BlockSpec((1, tm, 768), lambda bi, i: (bi, i, 0))],
        out_shape=[jax.ShapeDtypeStruct((b, 2, LANES, s), BF16),
                   jax.ShapeDtypeStruct((b, 2, s, LANES), BF16),
                   jax.ShapeDtypeStruct((b, 2, nkb, V_ROWS, KEY_BLOCK), BF16),
                   jax.ShapeDtypeStruct((b, 2, LANES, s), BF16),
                   jax.ShapeDtypeStruct((b, 2, s, LANES), BF16),
                   jax.ShapeDtypeStruct((b, 4, nkb, V_ROWS, KEY_BLOCK), BF16),
                   jax.ShapeDtypeStruct((b, s, 768), F32)],
        compiler_params=_cparams("arbitrary", "arbitrary"),
        name="inproj",
    )(x, mod, g_pre, w_in, q_gain, k_gain, *tabs, bd)


def _flash_body(qt_ref, k_ref, vt_ref, *rest, diff, tq, nkb, lam_init):
    if diff:
        lamqk_ref, gain_ref, o_ref, s_sc, p_sc, a_sc, b_sc, m_sc, acc_sc = rest
    else:
        o_ref, s_sc, p_sc, a_sc, b_sc, m_sc, acc_sc = rest
    row =lax.broadcasted_iota(jnp.int32, (LANES, tq), 0)
    if diff:
        base = (pl.program_id(1) % 2) * HEAD_DIM
        mask0 = (row >= base) & (row < base + C_QK_DIM)
        mask1 = (row >= base + C_QK_DIM) & (row < base + 2 * C_QK_DIM)
    else:
        mask0 = row < HEAD_DIM
        mask1 = row >= HEAD_DIM
    qt = qt_ref[0, 0].astype(F32)
    qst = jnp.concatenate([jnp.where(mask0, qt, 0.0), jnp.where(mask1, qt, 0.0)], axis=1).astype(BF16)

    m_sc[...] = jnp.full(m_sc.shape, NEG_INF, F32)
    acc_sc[...] = jnp.zeros(acc_sc.shape, F32)
    for slot in range(PIPE_SETS):
        p_sc[slot] = jnp.zeros(p_sc.shape[1:], BF16)
        a_sc[slot] = jnp.ones(a_sc.shape[1:], F32)
    last = nkb - 1

    def scores(j, slot):
        k = k_ref[0, 0, pl.ds(pl.multiple_of(j * KEY_BLOCK, KEY_BLOCK), KEY_BLOCK), :]
        s = jnp.dot(k, qst, preferred_element_type=F32)
        s_sc[slot] = s
        b_sc[slot] = jnp.max(s, axis=0, keepdims=True)

    def softmax(slot):
        m_prev = m_sc[...]
        m_new = jnp.maximum(m_prev, b_sc[slot])
        a_sc[slot] = jnp.exp2(m_prev - m_new)
        p_sc[slot] = jnp.exp2((s_sc[slot] - m_new).astype(BF16))
        m_sc[...] = m_new

    def values(j, slot):
        pv = jnp.dot(vt_ref[0, 0, j], p_sc[slot], preferred_element_type=F32)
        acc_sc[...] = acc_sc[...] * a_sc[slot] + pv

    def step(j, slot, prefetch=True):
        values(jnp.maximum(j - PIPE_SETS, 0), slot)
        softmax(slot)
        if prefetch:
            scores(jnp.minimum(j + PIPE_SETS, last), slot)

    scores(0, 0)
    scores(jnp.minimum(1, last), 1)

    per_trip = _steps_per_trip(nkb, FALLBACK_STEPS_PER_TRIP)

    def trip(i, carry):
        for r in range(per_trip):
            step(per_trip * i + r, r % PIPE_SETS)
        return carry

    lax.fori_loop(0, last // per_trip, trip, 0)
    step(last, 0, prefetch=False)
    if last >= 1:
        values(last - 1, 1)
    values(last, 0)

    acc = acc_sc[...]
    o = acc[0:HEAD_DIM] / acc[HEAD_DIM:HEAD_DIM + 1]
    o0 = o[:, :tq]
    o1 = o[:, tq:]
    if diff:
        lq = lamqk_ref[...]
        lam = (jnp.exp(jnp.sum(lq[0:1] * lq[1:2], axis=1, keepdims=True))
               - jnp.exp(jnp.sum(lq[2:3] * lq[3:4], axis=1, keepdims=True)) + lam_init)
        dlt = o0 - lam * o1
        ms = jnp.mean(dlt * dlt, axis=0, keepdims=True)
        y = dlt * lax.rsqrt(ms + 1e-6) * gain_ref[...] * (1.0 - lam_init)
        out_t = jnp.concatenate([y, jnp.zeros_like(y)], axis=0)
    else:
        out_t = jnp.concatenate([o0, o1], axis=0)
    o_ref[0] = out_t.T.astype(o_ref.dtype)


def _flash(qt, k, vt, *, diff, tq, lam_qk=None, gain=None, lam_init=0.0):
    b, _, _, s = qt.shape
    units, nkb = vt.shape[1:3]
    nk = k.shape[2]
    assert nkb * KEY_BLOCK == nk and nkb % PIPE_SETS == 1
    n = 2 * tq
    ku = (lambda u: u // 2) if diff else (lambda u: u)
    in_specs = [pl.BlockSpec((1, 1, LANES, tq), lambda bi, u, i: (bi, ku(u), 0, i)),
                pl.BlockSpec((1, 1, nk, LANES), lambda bi, u, i: (bi, ku(u), 0, 0)),
                pl.BlockSpec((1, 1, nkb, V_ROWS, KEY_BLOCK), lambda bi, u, i: (bi, u, 0, 0, 0))]
    args = [qt, k, vt]
    if diff:
        in_specs += [pl.BlockSpec(lam_qk.shape, lambda bi, u, i: (0, 0)),
                     pl.BlockSpec((HEAD_DIM, 1), lambda bi, u, i: (0, 0))]
        args += [lam_qk, gain]
    return pl.pallas_call(
        functools.partial(_flash_body, diff=diff, tq=tq, nkb=nkb, lam_init=lam_init),
        grid=(b, units, s // tq),
        in_specs=in_specs,
        out_specs=pl.BlockSpec((1, tq, LANES), lambda bi, u, i: (bi, i, u)),
        out_shape=jax.ShapeDtypeStruct((b, s, units * LANES), BF16),
        scratch_shapes=[pltpu.VMEM((PIPE_SETS, KEY_BLOCK, n), F32), pltpu.VMEM((PIPE_SETS, KEY_BLOCK, n), BF16),
                        pltpu.VMEM((PIPE_SETS, 1, n), F32), pltpu.VMEM((PIPE_SETS, 1, n), F32),
                        pltpu.VMEM((1, n), F32), pltpu.VMEM((V_ROWS, n), F32)],
        compiler_params=_cparams("arbitrary", "arbitrary", "arbitrary"),
        name="flash_diff" if diff else "flash_gqa",
    )(*args)


SHIFT_MARGIN = 1.01
SHIFT_DENOM_FLOOR = 2.0 ** -90


def _shift_flash_body(qt_ref, k_ref, vt_ref, kmax_ref, *rest, diff, tq, nkb, lam_init):
    if diff:
        lamqk_ref, gain_ref, o_ref, den_ref, p_sc, acc_sc = rest
    else:
        o_ref, den_ref, p_sc, acc_sc = rest
    row = lax.broadcasted_iota(jnp.int32, (LANES, tq), 0)
    if diff:
        base = (pl.program_id(1) % 2) * HEAD_DIM
        mask0 = (row >= base) & (row < base + C_QK_DIM)
        mask1 = (row >= base + C_QK_DIM) & (row < base + 2 * C_QK_DIM)
    else:
        mask0 = row < HEAD_DIM
        mask1 = row >= HEAD_DIM
    qt = qt_ref[0, 0].astype(F32)
    heads = [jnp.where(mask0, qt, 0.0), jnp.where(mask1, qt, 0.0)]
    qst = jnp.concatenate(heads, axis=1).astype(BF16)
    kmax = kmax_ref[0, 0]
    shift = jnp.concatenate(
        [jnp.sqrt(jnp.sum(hq * hq, axis=0, keepdims=True)) * kmax[i:i + 1, 0:1] for i, hq in enumerate(heads)],
        axis=1) * SHIFT_MARGIN

    acc_sc[...] = jnp.zeros(acc_sc.shape, F32)
    for slot in range(PIPE_SETS):
        p_sc[slot] = jnp.zeros(p_sc.shape[1:], BF16)
    last = nkb - 1

    def values(j, slot):
        acc_sc[...] += jnp.dot(vt_ref[0, 0, j], p_sc[slot], preferred_element_type=F32)

    def probs(j, slot):
        k = k_ref[0, 0, pl.ds(pl.multiple_of(j * KEY_BLOCK, KEY_BLOCK), KEY_BLOCK), :]
        s = jnp.dot(k, qst, preferred_element_type=F32)
        p_sc[slot] = jnp.exp2(s - shift).astype(BF16)

    def step(j, slot):
        values(jnp.maximum(j - PIPE_SETS, 0), slot)
        probs(j, slot)

    per_trip = _steps_per_trip(nkb, STEPS_PER_TRIP)

    def trip(i, carry):
        for r in range(per_trip):
            step(per_trip * i + r, r % PIPE_SETS)
        return carry

    lax.fori_loop(0, last // per_trip, trip, 0)
    step(last, 0)
    if last >= 1:
        values(last - 1, 1)
    values(last, 0)

    acc = acc_sc[...]
    den = acc[HEAD_DIM:HEAD_DIM + 1]
    den_ref[0, 0] = jnp.concatenate([den[:, :tq], den[:, tq:]], axis=0)
    o = acc[0:HEAD_DIM] / den
    o0 = o[:, :tq]
    o1 = o[:, tq:]
    if diff:
        lq = lamqk_ref[...]
        lam = (jnp.exp(jnp.sum(lq[0:1] * lq[1:2], axis=1, keepdims=True))
               - jnp.exp(jnp.sum(lq[2:3] * lq[3:4], axis=1, keepdims=True)) + lam_init)
        dlt = o0 - lam * o1
        ms = jnp.mean(dlt * dlt, axis=0, keepdims=True)
        y = dlt * lax.rsqrt(ms + 1e-6) * gain_ref[...] * (1.0 - lam_init)
        out_t = jnp.concatenate([y, jnp.zeros_like(y)], axis=0)
    else:
        out_t = jnp.concatenate([o0, o1], axis=0)
    o_ref[0] = out_t.T.astype(o_ref.dtype)


def _shift_flash(qt, k, vt, kmax, *, diff, tq, lam_qk=None, gain=None, lam_init=0.0):
    b, _, _, s = qt.shape
    units, nkb = vt.shape[1:3]
    nk = k.shape[2]
    assert nkb * KEY_BLOCK == nk and nkb % PIPE_SETS == 1
    n = 2 * tq
    ku = (lambda u: u // 2) if diff else (lambda u: u)
    in_specs = [pl.BlockSpec((1, 1, LANES, tq), lambda bi, u, i: (bi, ku(u), 0, i)),
                pl.BlockSpec((1, 1, nk, LANES), lambda bi, u, i: (bi, ku(u), 0, 0)),
                pl.BlockSpec((1, 1, nkb, V_ROWS, KEY_BLOCK), lambda bi, u, i: (bi, u, 0, 0, 0)),
                pl.BlockSpec((1, 1, 2, LANES), lambda bi, u, i: (bi, u, 0, 0))]
    args = [qt, k, vt, kmax]
    if diff:
        in_specs += [pl.BlockSpec(lam_qk.shape, lambda bi, u, i: (0, 0)),
                     pl.BlockSpec((HEAD_DIM, 1), lambda bi, u, i: (0, 0))]
        args += [lam_qk, gain]
    return pl.pallas_call(
        functools.partial(_shift_flash_body, diff=diff, tq=tq, nkb=nkb, lam_init=lam_init),
        grid=(b, units, s // tq),
        in_specs=in_specs,
        out_specs=[pl.BlockSpec((1, tq, LANES), lambda bi, u, i: (bi, i, u)),
                   pl.BlockSpec((1, 1, 2, tq), lambda bi, u, i: (bi, u, 0, i))],
        out_shape=[jax.ShapeDtypeStruct((b, s, units * LANES), BF16),
                   jax.ShapeDtypeStruct((b, units, 2, s), F32)],
        scratch_shapes=[pltpu.VMEM((PIPE_SETS, KEY_BLOCK, n), BF16), pltpu.VMEM((V_ROWS, n), F32)],
        compiler_params=_cparams("arbitrary", "arbitrary", "arbitrary"),
        name="shift_flash_diff" if diff else "shift_flash_gqa",
    )(*args)


def _key_norm_max(k, width):
    kf = k.astype(F32)
    sq = jnp.sum((kf * kf).reshape(k.shape[:3] + (LANES // width, width)), axis=-1)
    return jnp.sqrt(jnp.max(sq, axis=2))


def _attend(qt, k, vt, kmax, *, diff, tq, **kw):
    y, den = _shift_flash(qt, k, vt, kmax, diff=diff, tq=tq, **kw)
    ok = jnp.all(den >= SHIFT_DENOM_FLOOR)
    return lax.cond(ok, lambda: y, lambda: _flash(qt, k, vt, diff=diff, tq=tq, **kw))


def _mixout_body(x_ref, mod_ref, ya_ref, yc_ref, pdu_ref, prev_ref, next_ref, poolw_ref, pscale_ref,
                 sguw_ref, sgub_ref, wout_ref, gpost_ref, gpre_ref, xo_ref, tok_ref, *, tm, n):
    i = pl.program_id(1)
    nt = pl.num_programs(1)
    mod = mod_ref[0]
    pdu = pdu_ref[0]
    pb = pdu[:, 0:256]
    du = pdu[:, 256:512]
    dv = pdu[:, 512:768]

    prev = jnp.where(i > 0, prev_ref[0], 0.0)
    nxt = jnp.where(i < nt - 1, next_ref[0], 0.0)
    ext = jnp.concatenate([prev, pb, nxt], axis=0)
    rows = tm + 16
    up = lambda a, k: pltpu.roll(a, rows - k, 0)
    s2 = ext + up(ext, 1)
    s4 = s2 + up(s2, 2)
    s8 = s4 + up(s4, 4)
    s16 = s8 + up(s8, 8)
    lane = lax.broadcasted_iota(jnp.int32, (tm, 256), 1)
    grp = lane // 64
    win = jnp.where(grp == 0, up(s2, 7)[0:tm],
                    jnp.where(grp == 1, up(s4, 6)[0:tm],
                              jnp.where(grp == 2, up(s8, 4)[0:tm], s16[0:tm])))
    tok_idx = i * tm + lax.broadcasted_iota(jnp.int32, (tm, 256), 0)
    half = jnp.left_shift(1, grp)
    cnt = jnp.minimum(tok_idx + half, n) - jnp.maximum(tok_idx - half, 0)
    pooled = win / cnt.astype(F32) - pb
    pool = jnp.dot(pooled.astype(BF16), poolw_ref[...], preferred_element_type=F32) * pscale_ref[...]

    mu = jnp.mean(dv, axis=1, keepdims=True)
    dc = dv - mu
    vln = (dc * lax.rsqrt(jnp.mean(dc * dc, axis=1, keepdims=True) + 1e-5)).astype(BF16)
    head = lax.broadcasted_iota(jnp.int32, (CHUNK, 256), 1) // 64
    svs = []
    for c in range(tm // CHUNK):
        vch = vln[c * CHUNK:(c + 1) * CHUNK]
        sv = sgub_ref[...]
        for hd in range(4):
            r = jnp.dot(sguw_ref[hd], vch, preferred_element_type=F32)
            sv = sv + jnp.where(head == hd, r, 0.0)
        svs.append(sv)
    sgu = du * jnp.concatenate(svs, axis=0)

    ycat = jnp.concatenate([ya_ref[0], pool.astype(BF16), yc_ref[0], sgu.astype(BF16)], axis=1)
    o = jnp.dot(ycat, wout_ref[...], preferred_element_type=F32)
    xn = x_ref[0] + mod[2:3] * (_rms(o) * gpost_ref[...])
    xo_ref[0] = xn
    tok_ref[0] = _rms(xn) * gpre_ref[...] * (1.0 + mod[4:5]) + mod[3:4]


def _mixout(x, mod, ya, yc, pdu, poolw, pscale, sguw, sgub, wout, gpost, gpre, *, tm):
    b, s, d = x.shape
    t8 = tm // 8
    last8 = s // 8 - 1
    full = lambda shape: pl.BlockSpec(shape, lambda bi, i: (0,) * len(shape))
    row = lambda w: pl.BlockSpec((1, tm, w), lambda bi, i: (bi, i, 0))
    return pl.pallas_call(
        functools.partial(_mixout_body, tm=tm, n=s),
        grid=(b, s // tm),
        in_specs=[row(d), pl.BlockSpec((1, 6, d), lambda bi, i: (bi, 0, 0)),
                  row(256), row(512), row(768),
                  pl.BlockSpec((1, 8, 256), lambda bi, i: (bi, jnp.maximum(i * t8 - 1, 0), 0)),
                  pl.BlockSpec((1, 8, 256), lambda bi, i: (bi, jnp.minimum((i + 1) * t8, last8), 0)),
                  full((256, 256)), full((1, 256)), full(sguw.shape), full((CHUNK, 256)),
                  full(wout.shape), full((1, d)), full((1, d))],
        out_specs=[row(d), row(d)],
        out_shape=[jax.ShapeDtypeStruct((b, s, d), F32), jax.ShapeDtypeStruct((b, s, d), F32)],
        compiler_params=_cparams("arbitrary", "arbitrary"),
        name="mixout",
    )(x, mod, ya, yc, pdu, pdu, pdu, poolw, pscale, sguw, sgub, wout, gpost, gpre)


def _route(sel, scores, tm):
    per = N_EXPERTS // N_EXPERT_GROUPS
    i8 = lax.broadcasted_iota(jnp.int32, (per, tm), 0)
    gsc = []
    for g in range(N_EXPERT_GROUPS):
        blk = sel[g * per:(g + 1) * per]
        m1 = jnp.max(blk, axis=0, keepdims=True)
        i1 = jnp.min(jnp.where(blk == m1, i8, per), axis=0, keepdims=True)
        m2 = jnp.max(jnp.where(i8 == i1, NEG_INF, blk), axis=0, keepdims=True)
        gsc.append(m1 + m2)
    gs = jnp.concatenate(gsc, axis=0)
    g8 = lax.broadcasted_iota(jnp.int32, (N_EXPERT_GROUPS, tm), 0)
    gmask = jnp.zeros((N_EXPERT_GROUPS, tm), F32)
    for _ in range(TOPK_GROUPS):
        gm = jnp.max(gs, axis=0, keepdims=True)
        gi = jnp.min(jnp.where(gs == gm, g8, N_EXPERT_GROUPS), axis=0, keepdims=True)
        hit = g8 == gi
        gmask = jnp.where(hit, 1.0, gmask)
        gs = jnp.where(hit, NEG_INF, gs)
    ms = jnp.concatenate(
        [jnp.where(gmask[g:g + 1] > 0.0, sel[g * per:(g + 1) * per], NEG_INF) for g in range(N_EXPERT_GROUPS)],
        axis=0)
    e64 = lax.broadcasted_iota(jnp.int32, (N_EXPERTS, tm), 0)
    chosen = jnp.zeros((N_EXPERTS, tm), F32)
    for _ in range(TOP_K):
        m = jnp.max(ms, axis=0, keepdims=True)
        ii = jnp.min(jnp.where(ms == m, e64, N_EXPERTS), axis=0, keepdims=True)
        hit = e64 == ii
        chosen = jnp.where(hit, 1.0, chosen)
        ms = jnp.where(hit, NEG_INF, ms)
    w = chosen * scores
    return w / jnp.sum(w, axis=0, keepdims=True) * ROUTED_SCALE


def _moe_body(tok_ref, x_ref, mod_ref, rw_ref, rb_ref, w1_ref, w3_ref, w2_ref, s1_ref, s3_ref, s2_ref,
              gpost_ref, o_ref, hb_sc, gates_sc, acc_sc, *, tm):
    e = pl.program_id(2)

    @pl.when(e == 0)
    def _():
        h = tok_ref[0]
        hb = h.astype(BF16)
        hb_sc[...] = hb
        logits = jnp.dot(h, rw_ref[...], precision=lax.Precision.HIGHEST, preferred_element_type=F32)
        lt = logits.T[0:N_EXPERTS]
        scores = jax.nn.sigmoid(lt)
        gates_t = _route(scores + rb_ref[...], scores, tm)
        gates_sc[...] = jnp.concatenate([gates_t, jnp.zeros_like(gates_t)], axis=0).T
        a = jnp.dot(hb, s1_ref[...], preferred_element_type=F32)
        g = jnp.dot(hb, s3_ref[...], preferred_element_type=F32)
        acc_sc[...] = jnp.dot((_silu(a) * g).astype(BF16), s2_ref[...], preferred_element_type=F32)

    hb = hb_sc[...]
    lane = lax.broadcasted_iota(jnp.int32, (tm, LANES), 1)
    gates = gates_sc[...]
    acts = []
    for i in range(EXPERTS_PER_STEP):
        gate = jnp.sum(jnp.where(lane == e * EXPERTS_PER_STEP + i, gates, 0.0), axis=1, keepdims=True)
        a = jnp.dot(hb, w1_ref[0, i].astype(BF16), preferred_element_type=F32)
        g = jnp.dot(hb, w3_ref[0, i].astype(BF16), preferred_element_type=F32)
        acts.append((_silu(a) * g * gate).astype(BF16))
    hid = w2_ref.shape[2]
    w2 = w2_ref[0].reshape(EXPERTS_PER_STEP * hid, w2_ref.shape[3]).astype(BF16)
    acc_sc[...] += jnp.dot(jnp.concatenate(acts, axis=1), w2, preferred_element_type=F32)

    @pl.when(e == pl.num_programs(2) - 1)
    def _():
        mod = mod_ref[0]
        o_ref[0] = x_ref[0] + mod[5:6] * (_rms(acc_sc[...]) * gpost_ref[...])


def _moe(tok, x, mod, rw, rb, w1, w3, w2, s1, s3, s2, gpost, *, layer, tm):
    b, s, d = x.shape
    ne, _, hid = w1.shape[1:]
    full = lambda shape: pl.BlockSpec(shape, lambda bi, i, e: (0,) * len(shape))
    row = pl.BlockSpec((1, tm, d), lambda bi, i, e: (bi, i, 0))
    return pl.pallas_call(
        functools.partial(_moe_body, tm=tm),
        grid=(b, s // tm, ne // EXPERTS_PER_STEP),
        in_specs=[row, row, pl.BlockSpec((1, 6, d), lambda bi, i, e: (bi, 0, 0)),
                  full((d, LANES)), full((N_EXPERTS, 1)),
                  pl.BlockSpec((1, EXPERTS_PER_STEP, d, hid), lambda bi, i, e: (layer, e, 0, 0)),
                  pl.BlockSpec((1, EXPERTS_PER_STEP, d, hid), lambda bi, i, e: (layer, e, 0, 0)),
                  pl.BlockSpec((1, EXPERTS_PER_STEP, hid, d), lambda bi, i, e: (layer, e, 0, 0)),
                  full(s1.shape), full(s3.shape), full(s2.shape), full((1, d))],
        out_specs=row,
        out_shape=jax.ShapeDtypeStruct((b, s, d), F32),
        scratch_shapes=[pltpu.VMEM((tm, d), BF16), pltpu.VMEM((tm, LANES), F32), pltpu.VMEM((tm, d), F32)],
        compiler_params=_cparams("arbitrary", "arbitrary", "arbitrary"),
        name="moe",
    )(tok, x, mod, rw, rb, w1, w3, w2, s1, s3, s2, gpost)


SPARSE_TILE = 1024
SC_WINDOW = 128
PIECE = 256
HI16 = 0xFFFF0000


def _pack_pair(lo, hi):
    def rne(x):
        u = lax.bitcast_convert_type(x, jnp.uint32)
        return u + jnp.uint32(0x7FFF) + ((u >> 16) & jnp.uint32(1))
    word = (rne(hi) & jnp.uint32(HI16)) | (rne(lo) >> 16)
    return lax.bitcast_convert_type(word, jnp.int32)


def _unpack_pair(word):
    u = lax.bitcast_convert_type(word, jnp.uint32)
    return (lax.bitcast_convert_type(u << 16, F32), lax.bitcast_convert_type(u & jnp.uint32(HI16), F32))


def _pack_row(x):
    return [_pack_pair(x[:, 2 * p * PIECE:(2 * p + 1) * PIECE], x[:, (2 * p + 1) * PIECE:(2 * p + 2) * PIECE])
            for p in range(2)]


def _unpack_row(p0, p1):
    return jnp.concatenate(_unpack_pair(p0) + _unpack_pair(p1), axis=1)


def _sc_mesh():
    from jax.experimental.pallas import tpu_sc as plsc
    return plsc.VectorSubcoreMesh(core_axis_name="core", subcore_axis_name="subcore")


def _sc_gather(table, idx):
    nb = idx.shape[0]
    d = table.shape[1]
    assert nb % SC_WINDOW == 0

    @functools.partial(pl.kernel, out_type=jax.ShapeDtypeStruct((nb, d), table.dtype), mesh=_sc_mesh())
    def gather_kernel(x_hbm, i_hbm, o_hbm):
        def body(i_vmem, o_vmem):
            pltpu.sync_copy(x_hbm.at[i_vmem.at[0]], o_vmem)

        pltpu.emit_pipeline(
            body,
            grid=(nb // SC_WINDOW,),
            in_specs=[pl.BlockSpec((1, SC_WINDOW), lambda i: (0, i))],
            out_specs=[pl.BlockSpec((SC_WINDOW, d), lambda i: (i, 0))],
            core_axis_name=("core", "subcore"),
            dimension_semantics=(pltpu.PARALLEL,),
        )(i_hbm, o_hbm)

    return gather_kernel(table, idx.reshape(1, nb))


def _sc_scatter(x, idx, out_rows):
    rounds, m = idx.shape
    d = x.shape[1]
    nblk = m // SC_WINDOW
    assert m % SC_WINDOW == 0

    @functools.partial(pl.kernel, out_type=jax.ShapeDtypeStruct((out_rows, d), x.dtype), mesh=_sc_mesh())
    def scatter_kernel(x_hbm, i_hbm, o_hbm):
        def body(x_vmem, i_vmem):
            pltpu.sync_copy(x_vmem, o_hbm.at[i_vmem.at[0]])

        pltpu.emit_pipeline(
            body,
            grid=(rounds * nblk,),
            in_specs=[pl.BlockSpec((SC_WINDOW, d), lambda i: (i % nblk, 0)),
                      pl.BlockSpec((1, SC_WINDOW), lambda i: (0, i))],
            out_specs=[],
            core_axis_name=("core", "subcore"),
            dimension_semantics=(pltpu.PARALLEL,),
        )(x_hbm, i_hbm)

    return scatter_kernel(x, idx.reshape(1, rounds * m))


def _topk_route(sel, scores, tm):
    per = N_EXPERTS // N_EXPERT_GROUPS
    i8 = lax.broadcasted_iota(jnp.int32, (per, tm), 0)
    gsc = []
    for g in range(N_EXPERT_GROUPS):
        blk = sel[g * per:(g + 1) * per]
        m1 = jnp.max(blk, axis=0, keepdims=True)
        i1 = jnp.min(jnp.where(blk == m1, i8, per), axis=0, keepdims=True)
        m2 = jnp.max(jnp.where(i8 == i1, NEG_INF, blk), axis=0, keepdims=True)
        gsc.append(m1 + m2)
    gs = jnp.concatenate(gsc, axis=0)
    g8 = lax.broadcasted_iota(jnp.int32, (N_EXPERT_GROUPS, tm), 0)
    gmask = jnp.zeros((N_EXPERT_GROUPS, tm), F32)
    for _ in range(TOPK_GROUPS):
        gm = jnp.max(gs, axis=0, keepdims=True)
        gi = jnp.min(jnp.where(gs == gm, g8, N_EXPERT_GROUPS), axis=0, keepdims=True)
        hit = g8 == gi
        gmask = jnp.where(hit, 1.0, gmask)
        gs = jnp.where(hit, NEG_INF, gs)
    ms = jnp.concatenate(
        [jnp.where(gmask[g:g + 1] > 0.0, sel[g * per:(g + 1) * per], NEG_INF) for g in range(N_EXPERT_GROUPS)],
        axis=0)
    e64 = lax.broadcasted_iota(jnp.int32, (N_EXPERTS, tm), 0)
    hits, ids = [], []
    for _ in range(TOP_K):
        m = jnp.max(ms, axis=0, keepdims=True)
        ii = jnp.min(jnp.where(ms == m, e64, N_EXPERTS), axis=0, keepdims=True)
        hit = e64 == ii
        hits.append(hit)
        ids.append(ii)
        ms = jnp.where(hit, NEG_INF, ms)
    return hits, ids


def _route_body(tok_ref, rw_ref, rb_ref, tri_ref, tokp_ref, eidx_ref, posk_ref, wts_ref, cnt_ref, run_sc, *, tm):
    first = (pl.program_id(0) == 0) & (pl.program_id(1) == 0)

    @pl.when(first)
    def _():
        run_sc[...] = jnp.zeros(run_sc.shape, F32)

    h = tok_ref[0]
    pieces = _pack_row(h)
    tokp_ref[0] = pieces[0]
    tokp_ref[1] = pieces[1]

    logits = jnp.dot(h, rw_ref[...], precision=lax.Precision.HIGHEST, preferred_element_type=F32)
    scores = jax.nn.sigmoid(logits.T[0:N_EXPERTS])
    hits, ids = _topk_route(scores + rb_ref[...], scores, tm)
    raw = [jnp.sum(jnp.where(hit, scores, 0.0), axis=0, keepdims=True) for hit in hits]
    denom = raw[0]
    for r in raw[1:]:
        denom = denom + r
    wts = jnp.concatenate([r / denom * ROUTED_SCALE for r in raw], axis=0)
    wts_ref[...] = jnp.concatenate([wts, jnp.zeros((LANES - TOP_K, tm), F32)], axis=0).T

    chosen = jnp.zeros((N_EXPERTS, tm), F32)
    for hit in hits:
        chosen = jnp.where(hit, 1.0, chosen)
    incl = jnp.dot(chosen.astype(BF16), tri_ref[...], preferred_element_type=F32)
    before = run_sc[...] + incl - chosen
    posk_ref[...] = jnp.concatenate(
        [jnp.sum(jnp.where(hit, before, 0.0), axis=0, keepdims=True) for hit in hits], axis=0).astype(jnp.int32)
    eidx_ref[...] = jnp.concatenate(ids, axis=0)
    run_sc[...] = run_sc[...] + jnp.sum(chosen, axis=1, keepdims=True)
    cnt_ref[...] = jnp.broadcast_to(run_sc[...], cnt_ref.shape)


def _route_tokens(tok, rw, rb, tri, *, tm):
    b, s, d = tok.shape
    n = b * s
    nt = s // tm
    full = lambda shape: pl.BlockSpec(shape, lambda bi, i: (0,) * len(shape))
    col = lambda rows: pl.BlockSpec((rows, tm), lambda bi, i: (0, bi * nt + i))
    return pl.pallas_call(
        functools.partial(_route_body, tm=tm),
        grid=(b, nt),
        in_specs=[pl.BlockSpec((1, tm, d), lambda bi, i: (bi, i, 0)),
                  full((d, LANES)), full((N_EXPERTS, 1)), full((tm, tm))],
        out_specs=[pl.BlockSpec((2, tm, PIECE), lambda bi, i: (0, bi * nt + i, 0)),
                   col(TOP_K), col(TOP_K),
                   pl.BlockSpec((tm, LANES), lambda bi, i: (bi * nt + i, 0)),
                   full((N_EXPERTS, LANES))],
        out_shape=[jax.ShapeDtypeStruct((2, n, PIECE), jnp.int32),
                   jax.ShapeDtypeStruct((TOP_K, n), jnp.int32),
                   jax.ShapeDtypeStruct((TOP_K, n), jnp.int32),
                   jax.ShapeDtypeStruct((n, LANES), F32),
                   jax.ShapeDtypeStruct((N_EXPERTS, LANES), F32)],
        scratch_shapes=[pltpu.VMEM((N_EXPERTS, 1), F32)],
        compiler_params=_cparams("arbitrary", "arbitrary"),
        name="route",
    )(tok, rw, rb, tri)


def _ffn_body(te_ref, tv_ref, x_ref, w1_ref, w3_ref, w2_ref, y_ref):
    valid = tv_ref[pl.program_id(0)]

    @pl.when(valid > 0)
    def _():
        x = _unpack_row(x_ref[0], x_ref[1])
        rows = lax.broadcasted_iota(jnp.int32, x.shape, 0)
        xb = jnp.where(rows < valid, x, 0.0).astype(BF16)
        a = jnp.dot(xb, w1_ref[0, 0].astype(BF16), preferred_element_type=F32)
        g = jnp.dot(xb, w3_ref[0, 0].astype(BF16), preferred_element_type=F32)
        y = jnp.dot((_silu(a) * g).astype(BF16), w2_ref[0, 0].astype(BF16), preferred_element_type=F32)
        pieces = _pack_row(y)
        y_ref[0] = pieces[0]
        y_ref[1] = pieces[1]


def _expert_ffn(tile_expert, tile_valid, xs, w1, w3, w2, *, layer):
    _, rows, _ = xs.shape
    d, hid = w1.shape[2:]
    blk = pl.BlockSpec((2, SPARSE_TILE, PIECE), lambda i, te, tv: (0, i, 0))
    return pl.pallas_call(
        _ffn_body,
        grid_spec=pltpu.PrefetchScalarGridSpec(
            num_scalar_prefetch=2,
            grid=(rows // SPARSE_TILE,),
            in_specs=[blk,
                      pl.BlockSpec((1, 1, d, hid), lambda i, te, tv: (layer, te[i], 0, 0)),
                      pl.BlockSpec((1, 1, d, hid), lambda i, te, tv: (layer, te[i], 0, 0)),
                      pl.BlockSpec((1, 1, hid, d), lambda i, te, tv: (layer, te[i], 0, 0))],
            out_specs=blk),
        out_shape=jax.ShapeDtypeStruct(xs.shape, jnp.int32),
        compiler_params=_cparams("arbitrary"),
        name="expert_ffn",
    )(tile_expert, tile_valid, xs, w1, w3, w2)


def _combine_body(tok_ref, x_ref, mod_ref, yg_ref, wts_ref, s1_ref, s3_ref, s2_ref, gpost_ref, o_ref):
    hb = tok_ref[0].astype(BF16)
    a = jnp.dot(hb, s1_ref[...], preferred_element_type=F32)
    g = jnp.dot(hb, s3_ref[...], preferred_element_type=F32)
    f = jnp.dot((_silu(a) * g).astype(BF16), s2_ref[...], preferred_element_type=F32)
    wts = wts_ref[...]
    for k in range(TOP_K):
        f = f + wts[:, k:k + 1] * _unpack_row(yg_ref[0, k], yg_ref[1, k])
    o_ref[0] = x_ref[0] + mod_ref[0][5:6] * (_rms(f) * gpost_ref[...])


def _combine(tok, x, mod, yg, wts, s1, s3, s2, gpost, *, tm):
    b, s, d = x.shape
    nt = s // tm
    full = lambda shape: pl.BlockSpec(shape, lambda bi, i: (0,) * len(shape))
    row = pl.BlockSpec((1, tm, d), lambda bi, i: (bi, i, 0))
    return pl.pallas_call(
        _combine_body,
        grid=(b, nt),
        in_specs=[row, row, pl.BlockSpec((1, 6, d), lambda bi, i: (bi, 0, 0)),
                  pl.BlockSpec((2, TOP_K, tm, PIECE), lambda bi, i: (0, 0, bi * nt + i, 0)),
                  pl.BlockSpec((tm, LANES), lambda bi, i: (bi * nt + i, 0)),
                  full(s1.shape), full(s3.shape), full(s2.shape), full((1, d))],
        out_specs=row,
        out_shape=jax.ShapeDtypeStruct((b, s, d), F32),
        compiler_params=_cparams("arbitrary", "arbitrary"),
        name="combine",
    )(tok, x, mod, yg, wts, s1, s3, s2, gpost)


def _sparse_moe(tok, x, mod, rw, rb, w1, w3, w2, s1, s3, s2, gpost, *, layer, tm_route, tm_combine):
    b, s, d = x.shape
    n = b * s
    rows = n * TOP_K + N_EXPERTS * SPARSE_TILE
    ntiles = rows // SPARSE_TILE
    tri = jnp.triu(jnp.ones((tm_route, tm_route), BF16))
    tokp, eidx, posk, wts, cnt = _route_tokens(tok, rw, rb, tri, tm=tm_route)

    cnt = cnt[:, 0].astype(jnp.int32)
    padded = (cnt + SPARSE_TILE - 1) // SPARSE_TILE * SPARSE_TILE
    ends = jnp.cumsum(padded)
    offs = ends - padded
    experts = jnp.arange(N_EXPERTS, dtype=jnp.int32)
    dest = posk + jnp.sum(jnp.where(eidx[None] == experts[:, None, None], offs[:, None, None], 0), axis=0)
    tile_start = jnp.arange(ntiles, dtype=jnp.int32) * SPARSE_TILE
    tile_expert = jnp.minimum(jnp.sum(tile_start[:, None] >= ends[None, :], axis=1), N_EXPERTS - 1).astype(jnp.int32)
    tile_valid = jnp.clip(cnt[tile_expert] - (tile_start - offs[tile_expert]), 0, SPARSE_TILE).astype(jnp.int32)

    piece_base = jnp.arange(2, dtype=jnp.int32) * rows
    sidx = (dest[:, None, :] + piece_base[None, :, None]).reshape(TOP_K, 2 * n)
    xs = _sc_scatter(tokp.reshape(2 * n, PIECE), sidx, 2 * rows).reshape(2, rows, PIECE)
    ys = _expert_ffn(tile_expert, tile_valid, xs, w1, w3, w2, layer=layer)
    gidx = (piece_base[:, None, None] + dest[None, :, :]).reshape(2 * TOP_K * n)
    yg = _sc_gather(ys.reshape(2 * rows, PIECE), gidx).reshape(2, TOP_K, n, PIECE)
    return _combine(tok, x, mod, yg, wts, s1, s3, s2, gpost, tm=tm_combine)


def _rope_tables(s, dim):
    rows = s // GRID_W
    row = jnp.repeat(jnp.arange(rows, dtype=F32), GRID_W)
    col = jnp.tile(jnp.arange(GRID_W, dtype=F32), rows)
    half = dim // 2
    inv = ROPE_THETA ** (-jnp.arange(0, half, 2, dtype=F32) / half)
    ar = row[:, None] * inv[None, :]
    ac = col[:, None] * inv[None, :]
    ang = jnp.concatenate([ar, ar, ac, ac], axis=-1)
    sign = jnp.where((jnp.arange(dim) & (dim // 4)) == 0, -1.0, 1.0).astype(F32)
    reps = LANES // dim
    return jnp.tile(jnp.cos(ang), (1, reps)), jnp.tile(jnp.sin(ang) * sign, (1, reps))


def _block_diag(blocks):
    n = len(blocks)
    r, c = blocks[0].shape
    out = jnp.zeros((n * r, n * c), blocks[0].dtype)
    for i, blk in enumerate(blocks):
        out = out.at[i * r:(i + 1) * r, i * c:(i + 1) * c].set(blk)
    return out


def kernel(x, c, ctx, c_ctx, ada_w, ada_b, g_pre_mix, g_post_mix, g_pre_ffn, g_post_ffn, w_in, w_out, a_q_gain, a_k_gain, pool_w, pool_scale, lam_qk, c_subln_gain, sgu_w, sgu_b, router_w, router_bias, exp_w1, exp_w3, exp_w2, sh_w1, sh_w3, sh_w2):
    b, s, d = x.shape
    nctx = ctx.shape[1]
    depth = ada_w.shape[0]
    tm_lat = 512
    tq = 512
    tm_moe = 1024

    cvec = jnp.zeros((8, d), F32).at[0:b].set(c).at[b].set(c_ctx)
    mods = _ada(cvec, ada_w, ada_b)

    tabs = _rope_tables(s, HEAD_DIM) + _rope_tables(s, C_QK_DIM)
    bd = _block_diag([jnp.ones((HEAD_DIM, HEAD_DIM), BF16)] * 4)
    row2 = lambda v: v.reshape(1, -1)

    group = b
    return jnp.concatenate(
        [_layers(x[lo:lo + group], ctx[lo:lo + group], mods, lo, b, tabs, bd, g_pre_mix, g_post_mix, g_pre_ffn,
                 g_post_ffn, w_in, w_out, a_q_gain, a_k_gain, pool_w, pool_scale, lam_qk, c_subln_gain, sgu_w,
                 sgu_b, router_w, router_bias, exp_w1, exp_w3, exp_w2, sh_w1, sh_w3, sh_w2)
         for lo in range(0, b, group)], axis=0)


def _layers(xl, xc, mods, lo, ctx_row, tabs, bd, g_pre_mix, g_post_mix, g_pre_ffn, g_post_ffn, w_in, w_out,
            a_q_gain, a_k_gain, pool_w, pool_scale, lam_qk, c_subln_gain, sgu_w, sgu_b, router_w, router_bias,
            exp_w1, exp_w3, exp_w2, sh_w1, sh_w3, sh_w2):
    b, s, d = xl.shape
    nctx = xc.shape[1]
    depth = w_in.shape[0]
    tm_lat = 512
    tq = 512
    tm_moe = 1024
    row2 = lambda v: v.reshape(1, -1)
    for l in range(depth):
        need_ctx = l < depth - 1
        lam_init = 0.8 - 0.6 * math.exp(-0.3 * l)
        m6 = mods[l].reshape(8, 6, d)
        mod_l = m6[lo:lo + b]
        mod_c = jnp.broadcast_to(m6[ctx_row:ctx_row + 1], (b, 6, d))

        w_in_l = w_in[l].astype(BF16)
        qg = jnp.tile(a_q_gain[l], 4).reshape(1, 256)
        kg = jnp.tile(a_k_gain[l], 2).reshape(1, LANES)
        inproj = functools.partial(_inproj, g_pre=row2(g_pre_mix[l]), w_in=w_in_l, q_gain=qg, k_gain=kg,
                                   tabs=tabs, bd=bd)
        qat_l, ka_l, vat_l, qct_l, kc_l, vct_l, pdu_l = inproj(xl, mod_l, rope=True, tm=tm_lat)
        qat_c, ka_c, vat_c, qct_c, kc_c, vct_c, pdu_c = inproj(xc, mod_c, rope=False, tm=nctx)

        ka = jnp.concatenate([ka_c, ka_l], axis=2)
        vat = jnp.concatenate([vat_c, vat_l], axis=2)
        kc = jnp.concatenate([kc_c, kc_l], axis=2)
        vct = jnp.concatenate([vct_c, vct_l], axis=2)
        sub_gain = c_subln_gain[l].reshape(HEAD_DIM, 1)

        lanes = lambda v: jnp.broadcast_to(v[..., None], v.shape + (LANES,))
        kmax_a = lanes(_key_norm_max(ka, HEAD_DIM))
        kmax_c = lanes(_key_norm_max(kc, C_QK_DIM).reshape(b, 4, 2))
        ya_l = _attend(qat_l, ka, vat, kmax_a, diff=False, tq=tq)
        yc_l = _attend(qct_l, kc, vct, kmax_c, diff=True, tq=tq, lam_qk=lam_qk[l], gain=sub_gain,
                       lam_init=lam_init)

        poolw = _block_diag([pool_w[l, g] for g in range(len(POOL_WINDOWS))]).astype(BF16)
        sgub = jnp.repeat(jnp.transpose(sgu_b[l]), d // 16, axis=1)
        wo = w_out[l]
        wo_c = jnp.pad(wo[512:768].reshape(4, HEAD_DIM, d), ((0, 0), (0, LANES - HEAD_DIM), (0, 0)))
        wout = jnp.concatenate([wo[0:512], wo_c.reshape(4 * LANES, d), wo[768:1024]], axis=0).astype(BF16)
        mixout = functools.partial(_mixout, poolw=poolw, pscale=row2(pool_scale[l]), sguw=sgu_w[l].astype(BF16),
                                   sgub=sgub, wout=wout, gpost=row2(g_post_mix[l]), gpre=row2(g_pre_ffn[l]))
        rw = jnp.pad(router_w[l], ((0, 0), (0, LANES - N_EXPERTS)))
        moe_args = dict(rw=rw, rb=router_bias[l].reshape(N_EXPERTS, 1), w1=exp_w1, w3=exp_w3,
                        w2=exp_w2, s1=sh_w1[l].astype(BF16), s3=sh_w3[l].astype(BF16),
                        s2=sh_w2[l].astype(BF16), gpost=row2(g_post_ffn[l]), layer=l)
        moe = functools.partial(_moe, **moe_args)

        xl_mid, tok_l = mixout(xl, mod_l, ya_l, yc_l, pdu_l, tm=tm_lat)
        xl = _sparse_moe(tok_l, xl_mid, mod_l, tm_route=tm_moe, tm_combine=tm_lat, **moe_args)
        if need_ctx:
            ya_c = _flash(qat_c, ka_c, vat_c, diff=False, tq=nctx)
            yc_c = _flash(qct_c, kc_c, vct_c, diff=True, tq=nctx, lam_qk=lam_qk[l], gain=sub_gain,
                          lam_init=lam_init)
            xc_mid, tok_c = mixout(xc, mod_c, ya_c, yc_c, pdu_c, tm=nctx)
            xc = moe(tok_c, xc_mid, mod_c, tm=nctx)
    return xl
```

```python
import functools
import math

import jax
import jax.numpy as jnp
from jax import lax
from jax.experimental import pallas as pl
from jax.experimental.pallas import tpu as pltpu

F32 = jnp.float32
BF16 = jnp.bfloat16

GRID_W = 64
ROPE_THETA = 10000.0
HEAD_DIM = 64
C_QK_DIM = 32
POOL_WINDOWS = (2, 4, 8, 16)
CHUNK = 128
N_EXPERTS = 64
TOP_K = 8
N_EXPERT_GROUPS = 8
TOPK_GROUPS = 4
ROUTED_SCALE = 2.5
EXPERTS_PER_STEP = 2

LANES = 128
KEY_BLOCK = 256
PIPE_SETS = 2
STEPS_PER_TRIP = 64
FALLBACK_STEPS_PER_TRIP = 16
V_ROWS = 80
VMEM_LIMIT = 56 * 1024 * 1024

NEG_INF = float("-inf")
LOG2E = math.log2(math.e)


def _cparams(*sem, flags=None):
    return pltpu.CompilerParams(dimension_semantics=sem, vmem_limit_bytes=VMEM_LIMIT, flags=flags)


def _rms(x, eps=1e-6):
    return x * lax.rsqrt(jnp.mean(x * x, axis=-1, keepdims=True) + eps)


def _segsum(sq, bd):
    hi = sq.astype(BF16)
    lo = (sq - hi.astype(F32)).astype(BF16)
    return (jnp.dot(hi, bd, preferred_element_type=F32)
            + jnp.dot(lo, bd, preferred_element_type=F32))


def _rope(x, cos, sin_signed, quarter):
    w = x.shape[1]
    lane = lax.broadcasted_iota(jnp.int32, x.shape, 1)
    first = (lane & quarter) == 0
    rot = jnp.where(first, pltpu.roll(x, w - quarter, 1), pltpu.roll(x, quarter, 1))
    return x * cos + rot * sin_signed


def _steps_per_trip(nkb, most):
    looped = max(nkb - 1, PIPE_SETS)
    return max([t for t in range(PIPE_SETS, most + 1, PIPE_SETS) if looped % t == 0], default=PIPE_SETS)


def _silu(x):
    return x * jax.nn.sigmoid(x)


def _ada_body(c_ref, w_ref, b_ref, o_ref):
    sc = _silu(c_ref[...])
    o_ref[0] = jnp.dot(sc, w_ref[0], precision=lax.Precision.HIGHEST,
                       preferred_element_type=F32) + b_ref[0]


def _ada(cvec, ada_w, ada_b):
    nl, d, d6 = ada_w.shape
    return pl.pallas_call(
        _ada_body,
        grid=(nl, d6 // d),
        in_specs=[pl.BlockSpec((8, d), lambda l, j: (0, 0)),
                  pl.BlockSpec((1, d, d), lambda l, j: (l, 0, j)),
                  pl.BlockSpec((1, 1, d), lambda l, j: (l, 0, j))],
        out_specs=pl.BlockSpec((1, 8, d), lambda l, j: (l, 0, j)),
        out_shape=jax.ShapeDtypeStruct((nl, 8, d6), F32),
        compiler_params=_cparams("arbitrary", "arbitrary"),
        name="ada",
    )(cvec, ada_w, ada_b.reshape(nl, 1, d6))


def _inproj_body(x_ref, mod_ref, g_ref, w_ref, qg_ref, kg_ref, ca_ref, sa_ref, cc_ref, sc_ref, bd_ref,
                 qat_ref, ka_ref, vat_ref, qct_ref, kc_ref, vct_ref, pdu_ref, *, rope, tm):
    x = x_ref[0]
    mod = mod_ref[0]
    h = _rms(x) * g_ref[...] * (1.0 + mod[1:2]) + mod[0:1]
    p = jnp.dot(h.astype(BF16), w_ref[...], preferred_element_type=F32)

    lane = lax.broadcasted_iota(jnp.int32, (tm, LANES), 1)
    low = lane < HEAD_DIM
    ones_col = (lane == HEAD_DIM).astype(F32)
    nkb = tm // KEY_BLOCK

    aq = p[:, 0:256]
    qn = aq * lax.rsqrt(_segsum(aq * aq, bd_ref[...]) * (1.0 / HEAD_DIM) + 1e-6) * qg_ref[...]
    if rope:
        ca = ca_ref[...]
        sa = sa_ref[...]
        qn = _rope(qn, jnp.concatenate([ca, ca], axis=1), jnp.concatenate([sa, sa], axis=1), HEAD_DIM // 4)
    qn = qn * (HEAD_DIM ** -0.5 * LOG2E)
    for kv in range(2):
        qat_ref[0, kv] = qn[:, kv * LANES:(kv + 1) * LANES].T.astype(BF16)

    ak = p[:, 256:384]
    kn = ak * lax.rsqrt(_segsum(ak * ak, bd_ref[0:LANES, 0:LANES]) * (1.0 / HEAD_DIM) + 1e-6) * kg_ref[...]
    if rope:
        kn = _rope(kn, ca_ref[...], sa_ref[...], HEAD_DIM // 4)
    ksw = pltpu.roll(kn, HEAD_DIM, 1)
    ka_ref[0, 0] = jnp.where(low, kn, ksw).astype(BF16)
    ka_ref[0, 1] = jnp.where(low, ksw, kn).astype(BF16)

    def store_vt(ref, unit, vext):
        for j in range(nkb):
            ref[0, unit, j] = vext[j * KEY_BLOCK:(j + 1) * KEY_BLOCK].T[0:V_ROWS].astype(BF16)

    av = p[:, 384:512]
    store_vt(vat_ref, 0, jnp.where(low, av, ones_col))
    store_vt(vat_ref, 1, jnp.where(low, pltpu.roll(av, HEAD_DIM, 1), ones_col))

    cq = p[:, 768:1024]
    ck = p[:, 1024:1280]
    if rope:
        cc = cc_ref[...]
        sc = sc_ref[...]
        cc2 = jnp.concatenate([cc, cc], axis=1)
        sc2 = jnp.concatenate([sc, sc], axis=1)
        cq = _rope(cq, cc2, sc2, C_QK_DIM // 4)
        ck = _rope(ck, cc2, sc2, C_QK_DIM // 4)
    cq = cq * (C_QK_DIM ** -0.5 * LOG2E)
    for pr in range(2):
        qct_ref[0, pr] = cq[:, pr * LANES:(pr + 1) * LANES].T.astype(BF16)
        kc_ref[0, pr] = ck[:, pr * LANES:(pr + 1) * LANES].astype(BF16)
    cv = p[:, 1280:1536]
    for hd in range(4):
        seg = cv[:, (hd // 2) * LANES:(hd // 2 + 1) * LANES]
        if hd % 2:
            seg = pltpu.roll(seg, HEAD_DIM, 1)
        store_vt(vct_ref, hd, jnp.where(low, seg, ones_col))

    pdu_ref[0, :, 0:256] = p[:, 512:768]
    pdu_ref[0, :, 256:768] = p[:, 1536:2048]


def _inproj(x, mod, g_pre, w_in, q_gain, k_gain, tabs, bd, *, rope, tm):
    b, s, d = x.shape
    nkb = s // KEY_BLOCK
    tkb = tm // KEY_BLOCK
    full = lambda shape: pl.BlockSpec(shape, lambda bi, i: (0,) * len(shape))
    tab = pl.BlockSpec((tm, LANES), lambda bi, i: (i, 0))
    return pl.pallas_call(
        functools.partial(_inproj_body, rope=rope, tm=tm),
        grid=(b, s // tm),
        in_specs=[pl.BlockSpec((1, tm, d), lambda bi, i: (bi, i, 0)),
                  pl.BlockSpec((1, 6, d), lambda bi, i: (bi, 0, 0)),
                  full((1, d)), full(w_in.shape), full((1, 256)), full((1, LANES)),
                  tab, tab, tab, tab, full((256, 256))],
        out_specs=[pl.BlockSpec((1, 2, LANES, tm), lambda bi, i: (bi, 0, 0, i)),
                   pl.BlockSpec((1, 2, tm, LANES), lambda bi, i: (bi, 0, i, 0)),
                   pl.BlockSpec((1, 2, tkb, V_ROWS, KEY_BLOCK), lambda bi, i: (bi, 0, i, 0, 0)),
                   pl.BlockSpec((1, 2, LANES, tm), lambda bi, i: (bi, 0, 0, i)),
                   pl.BlockSpec((1, 2, tm, LANES), lambda bi, i: (bi, 0, i, 0)),
                   pl.BlockSpec((1, 4, tkb, V_ROWS, KEY_BLOCK), lambda bi, i: (bi, 0, i, 0, 0)),
                   pl.BlockSpec((1, tm, 768), lambda bi, i: (bi, i, 0))],
        out_shape=[jax.ShapeDtypeStruct((b, 2, LANES, s), BF16),
                   jax.ShapeDtypeStruct((b, 2, s, LANES), BF16),
                   jax.ShapeDtypeStruct((b, 2, nkb, V_ROWS, KEY_BLOCK), BF16),
                   jax.ShapeDtypeStruct((b, 2, LANES, s), BF16),
                   jax.ShapeDtypeStruct((b, 2, s, LANES), BF16),
                   jax.ShapeDtypeStruct((b, 4, nkb, V_ROWS, KEY_BLOCK), BF16),
                   jax.ShapeDtypeStruct((b, s, 768), F32)],
        compiler_params=_cparams("arbitrary", "arbitrary"),
        name="inproj",
    )(x, mod, g_pre, w_in, q_gain, k_gain, *tabs, bd)


def _flash_body(qt_ref, k_ref, vt_ref, *rest, diff, tq, nkb, lam_init):
    if diff:
        lamqk_ref, gain_ref, o_ref, s_sc, p_sc, a_sc, b_sc, m_sc, acc_sc = rest
    else:
        o_ref, s_sc, p_sc, a_sc, b_sc, m_sc, acc_sc = rest
    row =lax.broadcasted_iota(jnp.int32, (LANES, tq), 0)
    if diff:
        base = (pl.program_id(1) % 2) * HEAD_DIM
        mask0 = (row >= base) & (row < base + C_QK_DIM)
        mask1 = (row >= base + C_QK_DIM) & (row < base + 2 * C_QK_DIM)
    else:
        mask0 = row < HEAD_DIM
        mask1 = row >= HEAD_DIM
    qt = qt_ref[0, 0].astype(F32)
    qst = jnp.concatenate([jnp.where(mask0, qt, 0.0), jnp.where(mask1, qt, 0.0)], axis=1).astype(BF16)

    m_sc[...] = jnp.full(m_sc.shape, NEG_INF, F32)
    acc_sc[...] = jnp.zeros(acc_sc.shape, F32)
    for slot in range(PIPE_SETS):
        p_sc[slot] = jnp.zeros(p_sc.shape[1:], BF16)
        a_sc[slot] = jnp.ones(a_sc.shape[1:], F32)
    last = nkb - 1

    def scores(j, slot):
        k = k_ref[0, 0, pl.ds(pl.multiple_of(j * KEY_BLOCK, KEY_BLOCK), KEY_BLOCK), :]
        s = jnp.dot(k, qst, preferred_element_type=F32)
        s_sc[slot] = s
        b_sc[slot] = jnp.max(s, axis=0, keepdims=True)

    def softmax(slot):
        m_prev = m_sc[...]
        m_new = jnp.maximum(m_prev, b_sc[slot])
        a_sc[slot] = jnp.exp2(m_prev - m_new)
        p_sc[slot] = jnp.exp2((s_sc[slot] - m_new).astype(BF16))
        m_sc[...] = m_new

    def values(j, slot):
        pv = jnp.dot(vt_ref[0, 0, j], p_sc[slot], preferred_element_type=F32)
        acc_sc[...] = acc_sc[...] * a_sc[slot] + pv

    def step(j, slot, prefetch=True):
        values(jnp.maximum(j - PIPE_SETS, 0), slot)
        softmax(slot)
        if prefetch:
            scores(jnp.minimum(j + PIPE_SETS, last), slot)

    scores(0, 0)
    scores(jnp.minimum(1, last), 1)

    per_trip = _steps_per_trip(nkb, FALLBACK_STEPS_PER_TRIP)

    def trip(i, carry):
        for r in range(per_trip):
            step(per_trip * i + r, r % PIPE_SETS)
        return carry

    lax.fori_loop(0, last // per_trip, trip, 0)
    step(last, 0, prefetch=False)
    if last >= 1:
        values(last - 1, 1)
    values(last, 0)

    acc = acc_sc[...]
    o = acc[0:HEAD_DIM] / acc[HEAD_DIM:HEAD_DIM + 1]
    o0 = o[:, :tq]
    o1 = o[:, tq:]
    if diff:
        lq = lamqk_ref[...]
        lam = (jnp.exp(jnp.sum(lq[0:1] * lq[1:2], axis=1, keepdims=True))
               - jnp.exp(jnp.sum(lq[2:3] * lq[3:4], axis=1, keepdims=True)) + lam_init)
        dlt = o0 - lam * o1
        ms = jnp.mean(dlt * dlt, axis=0, keepdims=True)
        y = dlt * lax.rsqrt(ms + 1e-6) * gain_ref[...] * (1.0 - lam_init)
        out_t = jnp.concatenate([y, jnp.zeros_like(y)], axis=0)
    else:
        out_t = jnp.concatenate([o0, o1], axis=0)
    o_ref[0] = out_t.T.astype(o_ref.dtype)


def _flash(qt, k, vt, *, diff, tq, lam_qk=None, gain=None, lam_init=0.0):
    b, _, _, s = qt.shape
    units, nkb = vt.shape[1:3]
    nk = k.shape[2]
    assert nkb * KEY_BLOCK == nk and nkb % PIPE_SETS == 1
    n = 2 * tq
    ku = (lambda u: u // 2) if diff else (lambda u: u)
    in_specs = [pl.BlockSpec((1, 1, LANES, tq), lambda bi, u, i: (bi, ku(u), 0, i)),
                pl.BlockSpec((1, 1, nk, LANES), lambda bi, u, i: (bi, ku(u), 0, 0)),
                pl.BlockSpec((1, 1, nkb, V_ROWS, KEY_BLOCK), lambda bi, u, i: (bi, u, 0, 0, 0))]
    args = [qt, k, vt]
    if diff:
        in_specs += [pl.BlockSpec(lam_qk.shape, lambda bi, u, i: (0, 0)),
                     pl.BlockSpec((HEAD_DIM, 1), lambda bi, u, i: (0, 0))]
        args += [lam_qk, gain]
    return pl.pallas_call(
        functools.partial(_flash_body, diff=diff, tq=tq, nkb=nkb, lam_init=lam_init),
        grid=(b, units, s // tq),
        in_specs=in_specs,
        out_specs=pl.BlockSpec((1, tq, LANES), lambda bi, u, i: (bi, i, u)),
        out_shape=jax.ShapeDtypeStruct((b, s, units * LANES), BF16),
        scratch_shapes=[pltpu.VMEM((PIPE_SETS, KEY_BLOCK, n), F32), pltpu.VMEM((PIPE_SETS, KEY_BLOCK, n), BF16),
                        pltpu.VMEM((PIPE_SETS, 1, n), F32), pltpu.VMEM((PIPE_SETS, 1, n), F32),
                        pltpu.VMEM((1, n), F32), pltpu.VMEM((V_ROWS, n), F32)],
        compiler_params=_cparams("arbitrary", "arbitrary", "arbitrary"),
        name="flash_diff" if diff else "flash_gqa",
    )(*args)


SHIFT_MARGIN = 1.01
SHIFT_DENOM_FLOOR = 2.0 ** -90


def _shift_flash_body(qt_ref, k_ref, vt_ref, kmax_ref, *rest, diff, tq, nkb, lam_init):
    if diff:
        lamqk_ref, gain_ref, o_ref, den_ref, p_sc, acc_sc = rest
    else:
        o_ref, den_ref, p_sc, acc_sc = rest
    row = lax.broadcasted_iota(jnp.int32, (LANES, tq), 0)
    if diff:
        base = (pl.program_id(1) % 2) * HEAD_DIM
        mask0 = (row >= base) & (row < base + C_QK_DIM)
        mask1 = (row >= base + C_QK_DIM) & (row < base + 2 * C_QK_DIM)
    else:
        mask0 = row < HEAD_DIM
        mask1 = row >= HEAD_DIM
    qt = qt_ref[0, 0].astype(F32)
    heads = [jnp.where(mask0, qt, 0.0), jnp.where(mask1, qt, 0.0)]
    qst = jnp.concatenate(heads, axis=1).astype(BF16)
    kmax = kmax_ref[0, 0]
    shift = jnp.concatenate(
        [jnp.sqrt(jnp.sum(hq * hq, axis=0, keepdims=True)) * kmax[i:i + 1, 0:1] for i, hq in enumerate(heads)],
        axis=1) * SHIFT_MARGIN

    acc_sc[...] = jnp.zeros(acc_sc.shape, F32)
    for slot in range(PIPE_SETS):
        p_sc[slot] = jnp.zeros(p_sc.shape[1:], BF16)
    last = nkb - 1

    def values(j, slot):
        acc_sc[...] += jnp.dot(vt_ref[0, 0, j], p_sc[slot], preferred_element_type=F32)

    def probs(j, slot):
        k = k_ref[0, 0, pl.ds(pl.multiple_of(j * KEY_BLOCK, KEY_BLOCK), KEY_BLOCK), :]
        s = jnp.dot(k, qst, preferred_element_type=F32)
        p_sc[slot] = jnp.exp2(s - shift).astype(BF16)

    def step(j, slot):
        values(jnp.maximum(j - PIPE_SETS, 0), slot)
        probs(j, slot)

    per_trip = _steps_per_trip(nkb, STEPS_PER_TRIP)

    def trip(i, carry):
        for r in range(per_trip):
            step(per_trip * i + r, r % PIPE_SETS)
        return carry

    lax.fori_loop(0, last // per_trip, trip, 0)
    step(last, 0)
    if last >= 1:
        values(last - 1, 1)
    values(last, 0)

    acc = acc_sc[...]
    den = acc[HEAD_DIM:HEAD_DIM + 1]
    den_ref[0, 0] = jnp.concatenate([den[:, :tq], den[:, tq:]], axis=0)
    o = acc[0:HEAD_DIM] / den
    o0 = o[:, :tq]
    o1 = o[:, tq:]
    if diff:
        lq = lamqk_ref[...]
        lam = (jnp.exp(jnp.sum(lq[0:1] * lq[1:2], axis=1, keepdims=True))
               - jnp.exp(jnp.sum(lq[2:3] * lq[3:4], axis=1, keepdims=True)) + lam_init)
        dlt = o0 - lam * o1
        ms = jnp.mean(dlt * dlt, axis=0, keepdims=True)
        y = dlt * lax.rsqrt(ms + 1e-6) * gain_ref[...] * (1.0 - lam_init)
        out_t = jnp.concatenate([y, jnp.zeros_like(y)], axis=0)
    else:
        out_t = jnp.concatenate([o0, o1], axis=0)
    o_ref[0] = out_t.T.astype(o_ref.dtype)


def _shift_flash(qt, k, vt, kmax, *, diff, tq, lam_qk=None, gain=None, lam_init=0.0):
    b, _, _, s = qt.shape
    units, nkb = vt.shape[1:3]
    nk = k.shape[2]
    assert nkb * KEY_BLOCK == nk and nkb % PIPE_SETS == 1
    n = 2 * tq
    ku = (lambda u: u // 2) if diff else (lambda u: u)
    in_specs = [pl.BlockSpec((1, 1, LANES, tq), lambda bi, u, i: (bi, ku(u), 0, i)),
                pl.BlockSpec((1, 1, nk, LANES), lambda bi, u, i: (bi, ku(u), 0, 0)),
                pl.BlockSpec((1, 1, nkb, V_ROWS, KEY_BLOCK), lambda bi, u, i: (bi, u, 0, 0, 0)),
                pl.BlockSpec((1, 1, 2, LANES), lambda bi, u, i: (bi, u, 0, 0))]
    args = [qt, k, vt, kmax]
    if diff:
        in_specs += [pl.BlockSpec(lam_qk.shape, lambda bi, u, i: (0, 0)),
                     pl.BlockSpec((HEAD_DIM, 1), lambda bi, u, i: (0, 0))]
        args += [lam_qk, gain]
    return pl.pallas_call(
        functools.partial(_shift_flash_body, diff=diff, tq=tq, nkb=nkb, lam_init=lam_init),
        grid=(b, units, s // tq),
        in_specs=in_specs,
        out_specs=[pl.BlockSpec((1, tq, LANES), lambda bi, u, i: (bi, i, u)),
                   pl.BlockSpec((1, 1, 2, tq), lambda bi, u, i: (bi, u, 0, i))],
        out_shape=[jax.ShapeDtypeStruct((b, s, units * LANES), BF16),
                   jax.ShapeDtypeStruct((b, units, 2, s), F32)],
        scratch_shapes=[pltpu.VMEM((PIPE_SETS, KEY_BLOCK, n), BF16), pltpu.VMEM((V_ROWS, n), F32)],
        compiler_params=_cparams("arbitrary", "arbitrary", "arbitrary"),
        name="shift_flash_diff" if diff else "shift_flash_gqa",
    )(*args)


def _key_norm_max(k, width):
    kf = k.astype(F32)
    sq = jnp.sum((kf * kf).reshape(k.shape[:3] + (LANES // width, width)), axis=-1)
    return jnp.sqrt(jnp.max(sq, axis=2))


def _attend(qt, k, vt, kmax, *, diff, tq, **kw):
    y, den = _shift_flash(qt, k, vt, kmax, diff=diff, tq=tq, **kw)
    ok = jnp.all(den >= SHIFT_DENOM_FLOOR)
    return lax.cond(ok, lambda: y, lambda: _flash(qt, k, vt, diff=diff, tq=tq, **kw))


def _mixout_body(x_ref, mod_ref, ya_ref, yc_ref, pdu_ref, prev_ref, next_ref, poolw_ref, pscale_ref,
                 sguw_ref, sgub_ref, wout_ref, gpost_ref, gpre_ref, xo_ref, tok_ref, *, tm, n):
    i = pl.program_id(1)
    nt = pl.num_programs(1)
    mod = mod_ref[0]
    pdu = pdu_ref[0]
    pb = pdu[:, 0:256]
    du = pdu[:, 256:512]
    dv = pdu[:, 512:768]

    prev = jnp.where(i > 0, prev_ref[0], 0.0)
    nxt = jnp.where(i < nt - 1, next_ref[0], 0.0)
    ext = jnp.concatenate([prev, pb, nxt], axis=0)
    rows = tm + 16
    up = lambda a, k: pltpu.roll(a, rows - k, 0)
    s2 = ext + up(ext, 1)
    s4 = s2 + up(s2, 2)
    s8 = s4 + up(s4, 4)
    s16 = s8 + up(s8, 8)
    lane = lax.broadcasted_iota(jnp.int32, (tm, 256), 1)
    grp = lane // 64
    win = jnp.where(grp == 0, up(s2, 7)[0:tm],
                    jnp.where(grp == 1, up(s4, 6)[0:tm],
                              jnp.where(grp == 2, up(s8, 4)[0:tm], s16[0:tm])))
    tok_idx = i * tm + lax.broadcasted_iota(jnp.int32, (tm, 256), 0)
    half = jnp.left_shift(1, grp)
    cnt = jnp.minimum(tok_idx + half, n) - jnp.maximum(tok_idx - half, 0)
    pooled = win / cnt.astype(F32) - pb
    pool = jnp.dot(pooled.astype(BF16), poolw_ref[...], preferred_element_type=F32) * pscale_ref[...]

    mu = jnp.mean(dv, axis=1, keepdims=True)
    dc = dv - mu
    vln = (dc * lax.rsqrt(jnp.mean(dc * dc, axis=1, keepdims=True) + 1e-5)).astype(BF16)
    head = lax.broadcasted_iota(jnp.int32, (CHUNK, 256), 1) // 64
    svs = []
    for c in range(tm // CHUNK):
        vch = vln[c * CHUNK:(c + 1) * CHUNK]
        sv = sgub_ref[...]
        for hd in range(4):
            r = jnp.dot(sguw_ref[hd], vch, preferred_element_type=F32)
            sv = sv + jnp.where(head == hd, r, 0.0)
        svs.append(sv)
    sgu = du * jnp.concatenate(svs, axis=0)

    ycat = jnp.concatenate([ya_ref[0], pool.astype(BF16), yc_ref[0], sgu.astype(BF16)], axis=1)
    o = jnp.dot(ycat, wout_ref[...], preferred_element_type=F32)
    xn = x_ref[0] + mod[2:3] * (_rms(o) * gpost_ref[...])
    xo_ref[0] = xn
    tok_ref[0] = _rms(xn) * gpre_ref[...] * (1.0 + mod[4:5]) + mod[3:4]


def _mixout(x, mod, ya, yc, pdu, poolw, pscale, sguw, sgub, wout, gpost, gpre, *, tm):
    b, s, d = x.shape
    t8 = tm // 8
    last8 = s // 8 - 1
    full = lambda shape: pl.BlockSpec(shape, lambda bi, i: (0,) * len(shape))
    row = lambda w: pl.BlockSpec((1, tm, w), lambda bi, i: (bi, i, 0))
    return pl.pallas_call(
        functools.partial(_mixout_body, tm=tm, n=s),
        grid=(b, s // tm),
        in_specs=[row(d), pl.BlockSpec((1, 6, d), lambda bi, i: (bi, 0, 0)),
                  row(256), row(512), row(768),
                  pl.BlockSpec((1, 8, 256), lambda bi, i: (bi, jnp.maximum(i * t8 - 1, 0), 0)),
                  pl.BlockSpec((1, 8, 256), lambda bi, i: (bi, jnp.minimum((i + 1) * t8, last8), 0)),
                  full((256, 256)), full((1, 256)), full(sguw.shape), full((CHUNK, 256)),
                  full(wout.shape), full((1, d)), full((1, d))],
        out_specs=[row(d), row(d)],
        out_shape=[jax.ShapeDtypeStruct((b, s, d), F32), jax.ShapeDtypeStruct((b, s, d), F32)],
        compiler_params=_cparams("arbitrary", "arbitrary"),
        name="mixout",
    )(x, mod, ya, yc, pdu, pdu, pdu, poolw, pscale, sguw, sgub, wout, gpost, gpre)


def _route(sel, scores, tm):
    per = N_EXPERTS // N_EXPERT_GROUPS
    i8 = lax.broadcasted_iota(jnp.int32, (per, tm), 0)
    gsc = []
    for g in range(N_EXPERT_GROUPS):
        blk = sel[g * per:(g + 1) * per]
        m1 = jnp.max(blk, axis=0, keepdims=True)
        i1 = jnp.min(jnp.where(blk == m1, i8, per), axis=0, keepdims=True)
        m2 = jnp.max(jnp.where(i8 == i1, NEG_INF, blk), axis=0, keepdims=True)
        gsc.append(m1 + m2)
    gs = jnp.concatenate(gsc, axis=0)
    g8 = lax.broadcasted_iota(jnp.int32, (N_EXPERT_GROUPS, tm), 0)
    gmask = jnp.zeros((N_EXPERT_GROUPS, tm), F32)
    for _ in range(TOPK_GROUPS):
        gm = jnp.max(gs, axis=0, keepdims=True)
        gi = jnp.min(jnp.where(gs == gm, g8, N_EXPERT_GROUPS), axis=0, keepdims=True)
        hit = g8 == gi
        gmask = jnp.where(hit, 1.0, gmask)
        gs = jnp.where(hit, NEG_INF, gs)
    ms = jnp.concatenate(
        [jnp.where(gmask[g:g + 1] > 0.0, sel[g * per:(g + 1) * per], NEG_INF) for g in range(N_EXPERT_GROUPS)],
        axis=0)
    e64 = lax.broadcasted_iota(jnp.int32, (N_EXPERTS, tm), 0)
    chosen = jnp.zeros((N_EXPERTS, tm), F32)
    for _ in range(TOP_K):
        m = jnp.max(ms, axis=0, keepdims=True)
        ii = jnp.min(jnp.where(ms == m, e64, N_EXPERTS), axis=0, keepdims=True)
        hit = e64 == ii
        chosen = jnp.where(hit, 1.0, chosen)
        ms = jnp.where(hit, NEG_INF, ms)
    w = chosen * scores
    return w / jnp.sum(w, axis=0, keepdims=True) * ROUTED_SCALE


def _moe_body(tok_ref, x_ref, mod_ref, rw_ref, rb_ref, w1_ref, w3_ref, w2_ref, s1_ref, s3_ref, s2_ref,
              gpost_ref, o_ref, hb_sc, gates_sc, acc_sc, *, tm):
    e = pl.program_id(2)

    @pl.when(e == 0)
    def _():
        h = tok_ref[0]
        hb = h.astype(BF16)
        hb_sc[...] = hb
        logits = jnp.dot(h, rw_ref[...], precision=lax.Precision.HIGHEST, preferred_element_type=F32)
        lt = logits.T[0:N_EXPERTS]
        scores = jax.nn.sigmoid(lt)
        gates_t = _route(scores + rb_ref[...], scores, tm)
        gates_sc[...] = jnp.concatenate([gates_t, jnp.zeros_like(gates_t)], axis=0).T
        a = jnp.dot(hb, s1_ref[...], preferred_element_type=F32)
        g = jnp.dot(hb, s3_ref[...], preferred_element_type=F32)
        acc_sc[...] = jnp.dot((_silu(a) * g).astype(BF16), s2_ref[...], preferred_element_type=F32)

    hb = hb_sc[...]
    lane = lax.broadcasted_iota(jnp.int32, (tm, LANES), 1)
    gates = gates_sc[...]
    acts = []
    for i in range(EXPERTS_PER_STEP):
        gate = jnp.sum(jnp.where(lane == e * EXPERTS_PER_STEP + i, gates, 0.0), axis=1, keepdims=True)
        a = jnp.dot(hb, w1_ref[0, i].astype(BF16), preferred_element_type=F32)
        g = jnp.dot(hb, w3_ref[0, i].astype(BF16), preferred_element_type=F32)
        acts.append((_silu(a) * g * gate).astype(BF16))
    hid = w2_ref.shape[2]
    w2 = w2_ref[0].reshape(EXPERTS_PER_STEP * hid, w2_ref.shape[3]).astype(BF16)
    acc_sc[...] += jnp.dot(jnp.concatenate(acts, axis=1), w2, preferred_element_type=F32)

    @pl.when(e == pl.num_programs(2) - 1)
    def _():
        mod = mod_ref[0]
        o_ref[0] = x_ref[0] + mod[5:6] * (_rms(acc_sc[...]) * gpost_ref[...])


def _moe(tok, x, mod, rw, rb, w1, w3, w2, s1, s3, s2, gpost, *, layer, tm):
    b, s, d = x.shape
    ne, _, hid = w1.shape[1:]
    full = lambda shape: pl.BlockSpec(shape, lambda bi, i, e: (0,) * len(shape))
    row = pl.BlockSpec((1, tm, d), lambda bi, i, e: (bi, i, 0))
    return pl.pallas_call(
        functools.partial(_moe_body, tm=tm),
        grid=(b, s // tm, ne // EXPERTS_PER_STEP),
        in_specs=[row, row, pl.BlockSpec((1, 6, d), lambda bi, i, e: (bi, 0, 0)),
                  full((d, LANES)), full((N_EXPERTS, 1)),
                  pl.BlockSpec((1, EXPERTS_PER_STEP, d, hid), lambda bi, i, e: (layer, e, 0, 0)),
                  pl.BlockSpec((1, EXPERTS_PER_STEP, d, hid), lambda bi, i, e: (layer, e, 0, 0)),
                  pl.BlockSpec((1, EXPERTS_PER_STEP, hid, d), lambda bi, i, e: (layer, e, 0, 0)),
                  full(s1.shape), full(s3.shape), full(s2.shape), full((1, d))],
        out_specs=row,
        out_shape=jax.ShapeDtypeStruct((b, s, d), F32),
        scratch_shapes=[pltpu.VMEM((tm, d), BF16), pltpu.VMEM((tm, LANES), F32), pltpu.VMEM((tm, d), F32)],
        compiler_params=_cparams("arbitrary", "arbitrary", "arbitrary"),
        name="moe",
    )(tok, x, mod, rw, rb, w1, w3, w2, s1, s3, s2, gpost)


SPARSE_TILE = 1024
SC_WINDOW = 128
PIECE = 256
HI16 = 0xFFFF0000


def _pack_pair(lo, hi):
    def rne(x):
        u = lax.bitcast_convert_type(x, jnp.uint32)
        return u + jnp.uint32(0x7FFF) + ((u >> 16) & jnp.uint32(1))
    word = (rne(hi) & jnp.uint32(HI16)) | (rne(lo) >> 16)
    return lax.bitcast_convert_type(word, jnp.int32)


def _unpack_pair(word):
    u = lax.bitcast_convert_type(word, jnp.uint32)
    return (lax.bitcast_convert_type(u << 16, F32), lax.bitcast_convert_type(u & jnp.uint32(HI16), F32))


def _pack_row(x):
    return [_pack_pair(x[:, 2 * p * PIECE:(2 * p + 1) * PIECE], x[:, (2 * p + 1) * PIECE:(2 * p + 2) * PIECE])
            for p in range(2)]


def _unpack_row(p0, p1):
    return jnp.concatenate(_unpack_pair(p0) + _unpack_pair(p1), axis=1)


def _sc_mesh():
    from jax.experimental.pallas import tpu_sc as plsc
    return plsc.VectorSubcoreMesh(core_axis_name="core", subcore_axis_name="subcore")


def _sc_gather(table, idx):
    nb = idx.shape[0]
    d = table.shape[1]
    assert nb % SC_WINDOW == 0

    @functools.partial(pl.kernel, out_type=jax.ShapeDtypeStruct((nb, d), table.dtype), mesh=_sc_mesh())
    def gather_kernel(x_hbm, i_hbm, o_hbm):
        def body(i_vmem, o_vmem):
            pltpu.sync_copy(x_hbm.at[i_vmem.at[0]], o_vmem)

        pltpu.emit_pipeline(
            body,
            grid=(nb // SC_WINDOW,),
            in_specs=[pl.BlockSpec((1, SC_WINDOW), lambda i: (0, i))],
            out_specs=[pl.BlockSpec((SC_WINDOW, d), lambda i: (i, 0))],
            core_axis_name=("core", "subcore"),
            dimension_semantics=(pltpu.PARALLEL,),
        )(i_hbm, o_hbm)

    return gather_kernel(table, idx.reshape(1, nb))


def _sc_scatter(x, idx, out_rows):
    rounds, m = idx.shape
    d = x.shape[1]
    nblk = m // SC_WINDOW
    assert m % SC_WINDOW == 0

    @functools.partial(pl.kernel, out_type=jax.ShapeDtypeStruct((out_rows, d), x.dtype), mesh=_sc_mesh())
    def scatter_kernel(x_hbm, i_hbm, o_hbm):
        def body(x_vmem, i_vmem):
            for r in range(rounds):
                pltpu.sync_copy(x_vmem, o_hbm.at[i_vmem.at[r]])

        pltpu.emit_pipeline(
            body,
            grid=(nblk,),
            in_specs=[pl.BlockSpec((SC_WINDOW, d), lambda i: (i, 0)),
                      pl.BlockSpec((rounds, SC_WINDOW), lambda i: (0, i))],
            out_specs=[],
            core_axis_name=("core", "subcore"),
            dimension_semantics=(pltpu.PARALLEL,),
        )(x_hbm, i_hbm)

    return scatter_kernel(x, idx)


def _topk_route(sel, scores, tm):
    per = N_EXPERTS // N_EXPERT_GROUPS
    i8 = lax.broadcasted_iota(jnp.int32, (per, tm), 0)
    gsc = []
    for g in range(N_EXPERT_GROUPS):
        blk = sel[g * per:(g + 1) * per]
        m1 = jnp.max(blk, axis=0, keepdims=True)
        i1 = jnp.min(jnp.where(blk == m1, i8, per), axis=0, keepdims=True)
        m2 = jnp.max(jnp.where(i8 == i1, NEG_INF, blk), axis=0, keepdims=True)
        gsc.append(m1 + m2)
    gs = jnp.concatenate(gsc, axis=0)
    g8 = lax.broadcasted_iota(jnp.int32, (N_EXPERT_GROUPS, tm), 0)
    gmask = jnp.zeros((N_EXPERT_GROUPS, tm), F32)
    for _ in range(TOPK_GROUPS):
        gm = jnp.max(gs, axis=0, keepdims=True)
        gi = jnp.min(jnp.where(gs == gm, g8, N_EXPERT_GROUPS), axis=0, keepdims=True)
        hit = g8 == gi
        gmask = jnp.where(hit, 1.0, gmask)
        gs = jnp.where(hit, NEG_INF, gs)
    ms = jnp.concatenate(
        [jnp.where(gmask[g:g + 1] > 0.0, sel[g * per:(g + 1) * per], NEG_INF) for g in range(N_EXPERT_GROUPS)],
        axis=0)
    e64 = lax.broadcasted_iota(jnp.int32, (N_EXPERTS, tm), 0)
    hits, ids = [], []
    for _ in range(TOP_K):
        m = jnp.max(ms, axis=0, keepdims=True)
        ii = jnp.min(jnp.where(ms == m, e64, N_EXPERTS), axis=0, keepdims=True)
        hit = e64 == ii
        hits.append(hit)
        ids.append(ii)
        ms = jnp.where(hit, NEG_INF, ms)
    return hits, ids


def _route_body(tok_ref, rw_ref, rb_ref, tri_ref, tokp_ref, eidx_ref, posk_ref, wts_ref, cnt_ref, run_sc, *, tm):
    first = (pl.program_id(0) == 0) & (pl.program_id(1) == 0)

    @pl.when(first)
    def _():
        run_sc[...] = jnp.zeros(run_sc.shape, F32)

    h = tok_ref[0]
    pieces = _pack_row(h)
    tokp_ref[0] = pieces[0]
    tokp_ref[1] = pieces[1]

    rw = rw_ref[...]
    h_hi = h.astype(BF16)
    h_lo = (h - h_hi.astype(F32)).astype(BF16)
    w_hi = rw.astype(BF16)
    w_lo = (rw - w_hi.astype(F32)).astype(BF16)
    logits = (jnp.dot(h_hi, w_hi, preferred_element_type=F32) + jnp.dot(h_hi, w_lo, preferred_element_type=F32)
              + jnp.dot(h_lo, w_hi, preferred_element_type=F32))
    scores = jax.nn.sigmoid(logits.T[0:N_EXPERTS])
    hits, ids = _topk_route(scores + rb_ref[...], scores, tm)
    raw = [jnp.sum(jnp.where(hit, scores, 0.0), axis=0, keepdims=True) for hit in hits]
    denom = raw[0]
    for r in raw[1:]:
        denom = denom + r
    wts = jnp.concatenate([r / denom * ROUTED_SCALE for r in raw], axis=0)
    wts_ref[...] = jnp.concatenate([wts, jnp.zeros((LANES - TOP_K, tm), F32)], axis=0).T

    chosen = jnp.zeros((N_EXPERTS, tm), F32)
    for hit in hits:
        chosen = jnp.where(hit, 1.0, chosen)
    incl = jnp.dot(chosen.astype(BF16), tri_ref[...], preferred_element_type=F32)
    before = run_sc[...] + incl - chosen
    posk_ref[...] = jnp.concatenate(
        [jnp.sum(jnp.where(hit, before, 0.0), axis=0, keepdims=True) for hit in hits], axis=0).astype(jnp.int32)
    eidx_ref[...] = jnp.concatenate(ids, axis=0)
    run_sc[...] = run_sc[...] + jnp.sum(chosen, axis=1, keepdims=True)
    cnt_ref[...] = jnp.broadcast_to(run_sc[...], cnt_ref.shape)


def _route_tokens(tok, rw, rb, tri, *, tm):
    b, s, d = tok.shape
    n = b * s
    nt = s // tm
    full = lambda shape: pl.BlockSpec(shape, lambda bi, i: (0,) * len(shape))
    col = lambda rows: pl.BlockSpec((rows, tm), lambda bi, i: (0, bi * nt + i))
    return pl.pallas_call(
        functools.partial(_route_body, tm=tm),
        grid=(b, nt),
        in_specs=[pl.BlockSpec((1, tm, d), lambda bi, i: (bi, i, 0)),
                  full((d, LANES)), full((N_EXPERTS, 1)), full((tm, tm))],
        out_specs=[pl.BlockSpec((2, tm, PIECE), lambda bi, i: (0, bi * nt + i, 0)),
                   col(TOP_K), col(TOP_K),
                   pl.BlockSpec((tm, LANES), lambda bi, i: (bi * nt + i, 0)),
                   full((N_EXPERTS, LANES))],
        out_shape=[jax.ShapeDtypeStruct((2, n, PIECE), jnp.int32),
                   jax.ShapeDtypeStruct((TOP_K, n), jnp.int32),
                   jax.ShapeDtypeStruct((TOP_K, n), jnp.int32),
                   jax.ShapeDtypeStruct((n, LANES), F32),
                   jax.ShapeDtypeStruct((N_EXPERTS, LANES), F32)],
        scratch_shapes=[pltpu.VMEM((N_EXPERTS, 1), F32)],
        compiler_params=_cparams("arbitrary", "arbitrary"),
        name="route",
    )(tok, rw, rb, tri)


def _ffn_body(te_ref, tv_ref, x_ref, w1_ref, w3_ref, w2_ref, y_ref):
    valid = tv_ref[pl.program_id(0)]

    @pl.when(valid > 0)
    def _():
        x = _unpack_row(x_ref[0], x_ref[1])
        rows = lax.broadcasted_iota(jnp.int32, x.shape, 0)
        xb = jnp.where(rows < valid, x, 0.0).astype(BF16)
        a = jnp.dot(xb, w1_ref[0, 0].astype(BF16), preferred_element_type=F32)
        g = jnp.dot(xb, w3_ref[0, 0].astype(BF16), preferred_element_type=F32)
        y = jnp.dot((_silu(a) * g).astype(BF16), w2_ref[0, 0].astype(BF16), preferred_element_type=F32)
        pieces = _pack_row(y)
        y_ref[0] = pieces[0]
        y_ref[1] = pieces[1]


def _expert_ffn(tile_expert, tile_valid, xs, w1, w3, w2, *, layer):
    _, rows, _ = xs.shape
    d, hid = w1.shape[2:]
    blk = pl.BlockSpec((2, SPARSE_TILE, PIECE), lambda i, te, tv: (0, i, 0))
    return pl.pallas_call(
        _ffn_body,
        grid_spec=pltpu.PrefetchScalarGridSpec(
            num_scalar_prefetch=2,
            grid=(rows // SPARSE_TILE,),
            in_specs=[blk,
                      pl.BlockSpec((1, 1, d, hid), lambda i, te, tv: (layer, te[i], 0, 0)),
                      pl.BlockSpec((1, 1, d, hid), lambda i, te, tv: (layer, te[i], 0, 0)),
                      pl.BlockSpec((1, 1, hid, d), lambda i, te, tv: (layer, te[i], 0, 0))],
            out_specs=blk),
        out_shape=jax.ShapeDtypeStruct(xs.shape, jnp.int32),
        compiler_params=_cparams("arbitrary"),
        name="expert_ffn",
    )(tile_expert, tile_valid, xs, w1, w3, w2)


def _combine_body(tok_ref, x_ref, mod_ref, yg_ref, wts_ref, s1_ref, s3_ref, s2_ref, gpost_ref, o_ref):
    hb = tok_ref[0].astype(BF16)
    a = jnp.dot(hb, s1_ref[...], preferred_element_type=F32)
    g = jnp.dot(hb, s3_ref[...], preferred_element_type=F32)
    f = jnp.dot((_silu(a) * g).astype(BF16), s2_ref[...], preferred_element_type=F32)
    wts = wts_ref[...]
    for k in range(TOP_K):
        f = f + wts[:, k:k + 1] * _unpack_row(yg_ref[0, k], yg_ref[1, k])
    o_ref[0] = x_ref[0] + mod_ref[0][5:6] * (_rms(f) * gpost_ref[...])


def _combine(tok, x, mod, yg, wts, s1, s3, s2, gpost, *, tm):
    b, s, d = x.shape
    nt = s // tm
    full = lambda shape: pl.BlockSpec(shape, lambda bi, i: (0,) * len(shape))
    row = pl.BlockSpec((1, tm, d), lambda bi, i: (bi, i, 0))
    return pl.pallas_call(
        _combine_body,
        grid=(b, nt),
        in_specs=[row, row, pl.BlockSpec((1, 6, d), lambda bi, i: (bi, 0, 0)),
                  pl.BlockSpec((2, TOP_K, tm, PIECE), lambda bi, i: (0, 0, bi * nt + i, 0)),
                  pl.BlockSpec((tm, LANES), lambda bi, i: (bi * nt + i, 0)),
                  full(s1.shape), full(s3.shape), full(s2.shape), full((1, d))],
        out_specs=row,
        out_shape=jax.ShapeDtypeStruct((b, s, d), F32),
        compiler_params=_cparams("arbitrary", "arbitrary"),
        name="combine",
    )(tok, x, mod, yg, wts, s1, s3, s2, gpost)


def _dest_body(offs_ref, eidx_ref, posk_ref, dest_ref):
    eidx = eidx_ref[...]
    off = jnp.zeros(eidx.shape, jnp.int32)
    for e in range(N_EXPERTS):
        off = jnp.where(eidx == e, offs_ref[e], off)
    dest_ref[...] = posk_ref[...] + off


def _dest_rows(offs, eidx, posk, *, tm):
    k, n = eidx.shape
    blk = pl.BlockSpec((k, tm), lambda i, offs: (0, i))
    return pl.pallas_call(
        _dest_body,
        grid_spec=pltpu.PrefetchScalarGridSpec(num_scalar_prefetch=1, grid=(n // tm,), in_specs=[blk, blk],
                                               out_specs=blk),
        out_shape=jax.ShapeDtypeStruct((k, n), jnp.int32),
        compiler_params=_cparams("arbitrary"),
        name="dest_rows",
    )(offs, eidx, posk)


def _sparse_moe(tok, x, mod, rw, rb, w1, w3, w2, s1, s3, s2, gpost, *, layer, tm_route, tm_combine):
    b, s, d = x.shape
    n = b * s
    rows = n * TOP_K + N_EXPERTS * SPARSE_TILE
    ntiles = rows // SPARSE_TILE
    tri = jnp.triu(jnp.ones((tm_route, tm_route), BF16))
    tokp, eidx, posk, wts, cnt = _route_tokens(tok, rw, rb, tri, tm=tm_route)

    cnt = cnt[:, 0].astype(jnp.int32)
    padded = (cnt + SPARSE_TILE - 1) // SPARSE_TILE * SPARSE_TILE
    ends = jnp.cumsum(padded)
    offs = ends - padded
    dest = _dest_rows(offs.astype(jnp.int32), eidx, posk, tm=4096)
    tile_start = jnp.arange(ntiles, dtype=jnp.int32) * SPARSE_TILE
    tile_expert = jnp.minimum(jnp.sum(tile_start[:, None] >= ends[None, :], axis=1), N_EXPERTS - 1).astype(jnp.int32)
    tile_valid = jnp.clip(cnt[tile_expert] - (tile_start - offs[tile_expert]), 0, SPARSE_TILE).astype(jnp.int32)

    piece_base = jnp.arange(2, dtype=jnp.int32) * rows
    sidx = (dest[:, None, :] + piece_base[None, :, None]).reshape(TOP_K, 2 * n)
    xs = _sc_scatter(tokp.reshape(2 * n, PIECE), sidx, 2 * rows).reshape(2, rows, PIECE)
    ys = _expert_ffn(tile_expert, tile_valid, xs, w1, w3, w2, layer=layer)
    gidx = (piece_base[:, None, None] + dest[None, :, :]).reshape(2 * TOP_K * n)
    yg = _sc_gather(ys.reshape(2 * rows, PIECE), gidx).reshape(2, TOP_K, n, PIECE)
    return _combine(tok, x, mod, yg, wts, s1, s3, s2, gpost, tm=tm_combine)


def _rope_tables(s, dim):
    rows = s // GRID_W
    row = jnp.repeat(jnp.arange(rows, dtype=F32), GRID_W)
    col = jnp.tile(jnp.arange(GRID_W, dtype=F32), rows)
    half = dim // 2
    inv = ROPE_THETA ** (-jnp.arange(0, half, 2, dtype=F32) / half)
    ar = row[:, None] * inv[None, :]
    ac = col[:, None] * inv[None, :]
    ang = jnp.concatenate([ar, ar, ac, ac], axis=-1)
    sign = jnp.where((jnp.arange(dim) & (dim // 4)) == 0, -1.0, 1.0).astype(F32)
    reps = LANES // dim
    return jnp.tile(jnp.cos(ang), (1, reps)), jnp.tile(jnp.sin(ang) * sign, (1, reps))


def _block_diag(blocks):
    n = len(blocks)
    r, c = blocks[0].shape
    out = jnp.zeros((n * r, n * c), blocks[0].dtype)
    for i, blk in enumerate(blocks):
        out = out.at[i * r:(i + 1) * r, i * c:(i + 1) * c].set(blk)
    return out


def kernel(x, c, ctx, c_ctx, ada_w, ada_b, g_pre_mix, g_post_mix, g_pre_ffn, g_post_ffn, w_in, w_out, a_q_gain, a_k_gain, pool_w, pool_scale, lam_qk, c_subln_gain, sgu_w, sgu_b, router_w, router_bias, exp_w1, exp_w3, exp_w2, sh_w1, sh_w3, sh_w2):
    b, s, d = x.shape

    cvec = jnp.zeros((8, d), F32).at[0:b].set(c).at[b].set(c_ctx)
    mods = _ada(cvec, ada_w, ada_b)

    tabs = _rope_tables(s, HEAD_DIM) + _rope_tables(s, C_QK_DIM)
    bd = _block_diag([jnp.ones((HEAD_DIM, HEAD_DIM), BF16)] * 4)
    return _layers(x, ctx, mods, 0, b, tabs, bd, g_pre_mix, g_post_mix, g_pre_ffn, g_post_ffn, w_in, w_out,
                   a_q_gain, a_k_gain, pool_w, pool_scale, lam_qk, c_subln_gain, sgu_w, sgu_b, router_w,
                   router_bias, exp_w1, exp_w3, exp_w2, sh_w1, sh_w3, sh_w2)


def _layers(xl, xc, mods, lo, ctx_row, tabs, bd, g_pre_mix, g_post_mix, g_pre_ffn, g_post_ffn, w_in, w_out,
            a_q_gain, a_k_gain, pool_w, pool_scale, lam_qk, c_subln_gain, sgu_w, sgu_b, router_w, router_bias,
            exp_w1, exp_w3, exp_w2, sh_w1, sh_w3, sh_w2):
    b, s, d = xl.shape
    nctx = xc.shape[1]
    depth = w_in.shape[0]
    tm_lat = 512
    tq = 512
    tm_moe = 1024
    row2 = lambda v: v.reshape(1, -1)
    for l in range(depth):
        need_ctx = l < depth - 1
        lam_init = 0.8 - 0.6 * math.exp(-0.3 * l)
        m6 = mods[l].reshape(8, 6, d)
        mod_l = m6[lo:lo + b]
        mod_c = jnp.broadcast_to(m6[ctx_row:ctx_row + 1], (b, 6, d))

        w_in_l = w_in[l].astype(BF16)
        qg = jnp.tile(a_q_gain[l], 4).reshape(1, 256)
        kg = jnp.tile(a_k_gain[l], 2).reshape(1, LANES)
        inproj = functools.partial(_inproj, g_pre=row2(g_pre_mix[l]), w_in=w_in_l, q_gain=qg, k_gain=kg,
                                   tabs=tabs, bd=bd)
        qat_l, ka_l, vat_l, qct_l, kc_l, vct_l, pdu_l = inproj(xl, mod_l, rope=True, tm=tm_lat)
        qat_c, ka_c, vat_c, qct_c, kc_c, vct_c, pdu_c = inproj(xc, mod_c, rope=False, tm=nctx)

        ka = jnp.concatenate([ka_c, ka_l], axis=2)
        vat = jnp.concatenate([vat_c, vat_l], axis=2)
        kc = jnp.concatenate([kc_c, kc_l], axis=2)
        vct = jnp.concatenate([vct_c, vct_l], axis=2)
        sub_gain = c_subln_gain[l].reshape(HEAD_DIM, 1)

        lanes = lambda v: jnp.broadcast_to(v[..., None], v.shape + (LANES,))
        kmax_a = lanes(_key_norm_max(ka, HEAD_DIM))
        kmax_c = lanes(_key_norm_max(kc, C_QK_DIM).reshape(b, 4, 2))
        ya_l = _attend(qat_l, ka, vat, kmax_a, diff=False, tq=tq)
        yc_l = _attend(qct_l, kc, vct, kmax_c, diff=True, tq=tq, lam_qk=lam_qk[l], gain=sub_gain,
                       lam_init=lam_init)

        poolw = _block_diag([pool_w[l, g] for g in range(len(POOL_WINDOWS))]).astype(BF16)
        sgub = jnp.repeat(jnp.transpose(sgu_b[l]), d // 16, axis=1)
        wo = w_out[l]
        wo_c = jnp.pad(wo[512:768].reshape(4, HEAD_DIM, d), ((0, 0), (0, LANES - HEAD_DIM), (0, 0)))
        wout = jnp.concatenate([wo[0:512], wo_c.reshape(4 * LANES, d), wo[768:1024]], axis=0).astype(BF16)
        mixout = functools.partial(_mixout, poolw=poolw, pscale=row2(pool_scale[l]), sguw=sgu_w[l].astype(BF16),
                                   sgub=sgub, wout=wout, gpost=row2(g_post_mix[l]), gpre=row2(g_pre_ffn[l]))
        rw = jnp.pad(router_w[l], ((0, 0), (0, LANES - N_EXPERTS)))
        moe_args = dict(rw=rw, rb=router_bias[l].reshape(N_EXPERTS, 1), w1=exp_w1, w3=exp_w3,
                        w2=exp_w2, s1=sh_w1[l].astype(BF16), s3=sh_w3[l].astype(BF16),
                        s2=sh_w2[l].astype(BF16), gpost=row2(g_post_ffn[l]), layer=l)
        moe = functools.partial(_moe, **moe_args)

        xl_mid, tok_l = mixout(xl, mod_l, ya_l, yc_l, pdu_l, tm=tm_lat)
        xl = _sparse_moe(tok_l, xl_mid, mod_l, tm_route=tm_moe, tm_combine=tm_lat, **moe_args)
        if need_ctx:
            ya_c = _flash(qat_c, ka_c, vat_c, diff=False, tq=nctx)
            yc_c = _flash(qct_c, kc_c, vct_c, diff=True, tq=nctx, lam_qk=lam_qk[l], gain=sub_gain,
                          lam_init=lam_init)
            xc_mid, tok_c = mixout(xc, mod_c, ya_c, yc_c, pdu_c, tm=nctx)
            xc = moe(tok_c, xc_mid, mod_c, tm=nctx)
    return xl
```

```python
import functools
import math

import jax
import jax.numpy as jnp
from jax import lax
from jax.experimental import pallas as pl
from jax.experimental.pallas import tpu as pltpu

F32 = jnp.float32
BF16 = jnp.bfloat16

GRID_W = 64
ROPE_THETA = 10000.0
HEAD_DIM = 64
C_QK_DIM = 32
POOL_WINDOWS = (2, 4, 8, 16)
CHUNK = 128
N_EXPERTS = 64
TOP_K = 8
N_EXPERT_GROUPS = 8
TOPK_GROUPS = 4
ROUTED_SCALE = 2.5
EXPERTS_PER_STEP = 2

LANES = 128
KEY_BLOCK = 256
PIPE_SETS = 2
STEPS_PER_TRIP = 64
FALLBACK_STEPS_PER_TRIP = 16
V_ROWS = 80
VMEM_LIMIT = 56 * 1024 * 1024

NEG_INF = float("-inf")
LOG2E = math.log2(math.e)


def _cparams(*sem, flags=None):
    return pltpu.CompilerParams(dimension_semantics=sem, vmem_limit_bytes=VMEM_LIMIT, flags=flags)


def _rms(x, eps=1e-6):
    return x * lax.rsqrt(jnp.mean(x * x, axis=-1, keepdims=True) + eps)


def _segsum(sq, bd):
    hi = sq.astype(BF16)
    lo = (sq - hi.astype(F32)).astype(BF16)
    return (jnp.dot(hi, bd, preferred_element_type=F32)
            + jnp.dot(lo, bd, preferred_element_type=F32))


def _rope(x, cos, sin_signed, quarter):
    w = x.shape[1]
    lane = lax.broadcasted_iota(jnp.int32, x.shape, 1)
    first = (lane & quarter) == 0
    rot = jnp.where(first, pltpu.roll(x, w - quarter, 1), pltpu.roll(x, quarter, 1))
    return x * cos + rot * sin_signed


def _steps_per_trip(nkb, most):
    looped = max(nkb - 1, PIPE_SETS)
    return max([t for t in range(PIPE_SETS, most + 1, PIPE_SETS) if looped % t == 0], default=PIPE_SETS)


def _silu(x):
    return x * jax.nn.sigmoid(x)


def _ada_body(c_ref, w_ref, b_ref, o_ref):
    sc = _silu(c_ref[...])
    o_ref[0] = jnp.dot(sc, w_ref[0], precision=lax.Precision.HIGHEST,
                       preferred_element_type=F32) + b_ref[0]


def _ada(cvec, ada_w, ada_b):
    nl, d, d6 = ada_w.shape
    return pl.pallas_call(
        _ada_body,
        grid=(nl, d6 // d),
        in_specs=[pl.BlockSpec((8, d), lambda l, j: (0, 0)),
                  pl.BlockSpec((1, d, d), lambda l, j: (l, 0, j)),
                  pl.BlockSpec((1, 1, d), lambda l, j: (l, 0, j))],
        out_specs=pl.BlockSpec((1, 8, d), lambda l, j: (l, 0, j)),
        out_shape=jax.ShapeDtypeStruct((nl, 8, d6), F32),
        compiler_params=_cparams("arbitrary", "arbitrary"),
        name="ada",
    )(cvec, ada_w, ada_b.reshape(nl, 1, d6))


def _inproj_body(x_ref, mod_ref, g_ref, w_ref, qg_ref, kg_ref, ca_ref, sa_ref, cc_ref, sc_ref, bd_ref,
                 qat_ref, ka_ref, vat_ref, qct_ref, kc_ref, vct_ref, pdu_ref, *, rope, tm):
    x = x_ref[0]
    mod = mod_ref[0]
    h = _rms(x) * g_ref[...] * (1.0 + mod[1:2]) + mod[0:1]
    p = jnp.dot(h.astype(BF16), w_ref[...], preferred_element_type=F32)

    lane = lax.broadcasted_iota(jnp.int32, (tm, LANES), 1)
    low = lane < HEAD_DIM
    ones_col = (lane == HEAD_DIM).astype(F32)
    nkb = tm // KEY_BLOCK

    aq = p[:, 0:256]
    qn = aq * lax.rsqrt(_segsum(aq * aq, bd_ref[...]) * (1.0 / HEAD_DIM) + 1e-6) * qg_ref[...]
    if rope:
        ca = ca_ref[...]
        sa = sa_ref[...]
        qn = _rope(qn, jnp.concatenate([ca, ca], axis=1), jnp.concatenate([sa, sa], axis=1), HEAD_DIM // 4)
    qn = qn * (HEAD_DIM ** -0.5 * LOG2E)
    for kv in range(2):
        qat_ref[0, kv] = qn[:, kv * LANES:(kv + 1) * LANES].T.astype(BF16)

    ak = p[:, 256:384]
    kn = ak * lax.rsqrt(_segsum(ak * ak, bd_ref[0:LANES, 0:LANES]) * (1.0 / HEAD_DIM) + 1e-6) * kg_ref[...]
    if rope:
        kn = _rope(kn, ca_ref[...], sa_ref[...], HEAD_DIM // 4)
    ksw = pltpu.roll(kn, HEAD_DIM, 1)
    ka_ref[0, 0] = jnp.where(low, kn, ksw).astype(BF16)
    ka_ref[0, 1] = jnp.where(low, ksw, kn).astype(BF16)

    def store_vt(ref, unit, vext):
        for j in range(nkb):
            ref[0, unit, j] = vext[j * KEY_BLOCK:(j + 1) * KEY_BLOCK].T[0:V_ROWS].astype(BF16)

    av = p[:, 384:512]
    store_vt(vat_ref, 0, jnp.where(low, av, ones_col))
    store_vt(vat_ref, 1, jnp.where(low, pltpu.roll(av, HEAD_DIM, 1), ones_col))

    cq = p[:, 768:1024]
    ck = p[:, 1024:1280]
    if rope:
        cc = cc_ref[...]
        sc = sc_ref[...]
        cc2 = jnp.concatenate([cc, cc], axis=1)
        sc2 = jnp.concatenate([sc, sc], axis=1)
        cq = _rope(cq, cc2, sc2, C_QK_DIM // 4)
        ck = _rope(ck, cc2, sc2, C_QK_DIM // 4)
    cq = cq * (C_QK_DIM ** -0.5 * LOG2E)
    for pr in range(2):
        qct_ref[0, pr] = cq[:, pr * LANES:(pr + 1) * LANES].T.astype(BF16)
        kc_ref[0, pr] = ck[:, pr * LANES:(pr + 1) * LANES].astype(BF16)
    cv = p[:, 1280:1536]
    for hd in range(4):
        seg = cv[:, (hd // 2) * LANES:(hd // 2 + 1) * LANES]
        if hd % 2:
            seg = pltpu.roll(seg, HEAD_DIM, 1)
        store_vt(vct_ref, hd, jnp.where(low, seg, ones_col))

    pdu_ref[0, :, 0:256] = p[:, 512:768]
    pdu_ref[0, :, 256:768] = p[:, 1536:2048]


def _inproj(x, mod, g_pre, w_in, q_gain, k_gain, tabs, bd, *, rope, tm):
    b, s, d = x.shape
    nkb = s // KEY_BLOCK
    tkb = tm // KEY_BLOCK
    full = lambda shape: pl.BlockSpec(shape, lambda bi, i: (0,) * len(shape))
    tab = pl.BlockSpec((tm, LANES), lambda bi, i: (i, 0))
    return pl.pallas_call(
        functools.partial(_inproj_body, rope=rope, tm=tm),
        grid=(b, s // tm),
        in_specs=[pl.BlockSpec((1, tm, d), lambda bi, i: (bi, i, 0)),
                  pl.BlockSpec((1, 6, d), lambda bi, i: (bi, 0, 0)),
                  full((1, d)), full(w_in.shape), full((1, 256)), full((1, LANES)),
                  tab, tab, tab, tab, full((256, 256))],
        out_specs=[pl.BlockSpec((1, 2, LANES, tm), lambda bi, i: (bi, 0, 0, i)),
                   pl.BlockSpec((1, 2, tm, LANES), lambda bi, i: (bi, 0, i, 0)),
                   pl.BlockSpec((1, 2, tkb, V_ROWS, KEY_BLOCK), lambda bi, i: (bi, 0, i, 0, 0)),
                   pl.BlockSpec((1, 2, LANES, tm), lambda bi, i: (bi, 0, 0, i)),
                   pl.BlockSpec((1, 2, tm, LANES), lambda bi, i: (bi, 0, i, 0)),
                   pl.BlockSpec((1, 4, tkb, V_ROWS, KEY_BLOCK), lambda bi, i: (bi, 0, i, 0, 0)),
                   pl.BlockSpec((1, tm, 768), lambda bi, i: (bi, i, 0))],
        out_shape=[jax.ShapeDtypeStruct((b, 2, LANES, s), BF16),
                   jax.ShapeDtypeStruct((b, 2, s, LANES), BF16),
                   jax.ShapeDtypeStruct((b, 2, nkb, V_ROWS, KEY_BLOCK), BF16),
                   jax.ShapeDtypeStruct((b, 2, LANES, s), BF16),
                   jax.ShapeDtypeStruct((b, 2, s, LANES), BF16),
                   jax.ShapeDtypeStruct((b, 4, nkb, V_ROWS, KEY_BLOCK), BF16),
                   jax.ShapeDtypeStruct((b, s, 768), F32)],
        compiler_params=_cparams("arbitrary", "arbitrary"),
        name="inproj",
    )(x, mod, g_pre, w_in, q_gain, k_gain, *tabs, bd)


def _flash_body(qt_ref, k_ref, vt_ref, *rest, diff, tq, nkb, lam_init):
    if diff:
        lamqk_ref, gain_ref, o_ref, s_sc, p_sc, a_sc, b_sc, m_sc, acc_sc = rest
    else:
        o_ref, s_sc, p_sc, a_sc, b_sc, m_sc, acc_sc = rest
    row =lax.broadcasted_iota(jnp.int32, (LANES, tq), 0)
    if diff:
        base = (pl.program_id(1) % 2) * HEAD_DIM
        mask0 = (row >= base) & (row < base + C_QK_DIM)
        mask1 = (row >= base + C_QK_DIM) & (row < base + 2 * C_QK_DIM)
    else:
        mask0 = row < HEAD_DIM
        mask1 = row >= HEAD_DIM
    qt = qt_ref[0, 0].astype(F32)
    qst = jnp.concatenate([jnp.where(mask0, qt, 0.0), jnp.where(mask1, qt, 0.0)], axis=1).astype(BF16)

    m_sc[...] = jnp.full(m_sc.shape, NEG_INF, F32)
    acc_sc[...] = jnp.zeros(acc_sc.shape, F32)
    for slot in range(PIPE_SETS):
        p_sc[slot] = jnp.zeros(p_sc.shape[1:], BF16)
        a_sc[slot] = jnp.ones(a_sc.shape[1:], F32)
    last = nkb - 1

    def scores(j, slot):
        k = k_ref[0, 0, pl.ds(pl.multiple_of(j * KEY_BLOCK, KEY_BLOCK), KEY_BLOCK), :]
        s = jnp.dot(k, qst, preferred_element_type=F32)
        s_sc[slot] = s
        b_sc[slot] = jnp.max(s, axis=0, keepdims=True)

    def softmax(slot):
        m_prev = m_sc[...]
        m_new = jnp.maximum(m_prev, b_sc[slot])
        a_sc[slot] = jnp.exp2(m_prev - m_new)
        p_sc[slot] = jnp.exp2((s_sc[slot] - m_new).astype(BF16))
        m_sc[...] = m_new

    def values(j, slot):
        pv = jnp.dot(vt_ref[0, 0, j], p_sc[slot], preferred_element_type=F32)
        acc_sc[...] = acc_sc[...] * a_sc[slot] + pv

    def step(j, slot, prefetch=True):
        values(jnp.maximum(j - PIPE_SETS, 0), slot)
        softmax(slot)
        if prefetch:
            scores(jnp.minimum(j + PIPE_SETS, last), slot)

    scores(0, 0)
    scores(jnp.minimum(1, last), 1)

    per_trip = _steps_per_trip(nkb, FALLBACK_STEPS_PER_TRIP)

    def trip(i, carry):
        for r in range(per_trip):
            step(per_trip * i + r, r % PIPE_SETS)
        return carry

    lax.fori_loop(0, last // per_trip, trip, 0)
    step(last, 0, prefetch=False)
    if last >= 1:
        values(last - 1, 1)
    values(last, 0)

    acc = acc_sc[...]
    o = acc[0:HEAD_DIM] / acc[HEAD_DIM:HEAD_DIM + 1]
    o0 = o[:, :tq]
    o1 = o[:, tq:]
    if diff:
        lq = lamqk_ref[...]
        lam = (jnp.exp(jnp.sum(lq[0:1] * lq[1:2], axis=1, keepdims=True))
               - jnp.exp(jnp.sum(lq[2:3] * lq[3:4], axis=1, keepdims=True)) + lam_init)
        dlt = o0 - lam * o1
        ms = jnp.mean(dlt * dlt, axis=0, keepdims=True)
        y = dlt * lax.rsqrt(ms + 1e-6) * gain_ref[...] * (1.0 - lam_init)
        out_t = jnp.concatenate([y, jnp.zeros_like(y)], axis=0)
    else:
        out_t = jnp.concatenate([o0, o1], axis=0)
    o_ref[0] = out_t.T.astype(o_ref.dtype)


def _flash(qt, k, vt, *, diff, tq, lam_qk=None, gain=None, lam_init=0.0):
    b, _, _, s = qt.shape
    units, nkb = vt.shape[1:3]
    nk = k.shape[2]
    assert nkb * KEY_BLOCK == nk and nkb % PIPE_SETS == 1
    n = 2 * tq
    ku = (lambda u: u // 2) if diff else (lambda u: u)
    in_specs = [pl.BlockSpec((1, 1, LANES, tq), lambda bi, u, i: (bi, ku(u), 0, i)),
                pl.BlockSpec((1, 1, nk, LANES), lambda bi, u, i: (bi, ku(u), 0, 0)),
                pl.BlockSpec((1, 1, nkb, V_ROWS, KEY_BLOCK), lambda bi, u, i: (bi, u, 0, 0, 0))]
    args = [qt, k, vt]
    if diff:
        in_specs += [pl.BlockSpec(lam_qk.shape, lambda bi, u, i: (0, 0)),
                     pl.BlockSpec((HEAD_DIM, 1), lambda bi, u, i: (0, 0))]
        args += [lam_qk, gain]
    return pl.pallas_call(
        functools.partial(_flash_body, diff=diff, tq=tq, nkb=nkb, lam_init=lam_init),
        grid=(b, units, s // tq),
        in_specs=in_specs,
        out_specs=pl.BlockSpec((1, tq, LANES), lambda bi, u, i: (bi, i, u)),
        out_shape=jax.ShapeDtypeStruct((b, s, units * LANES), BF16),
        scratch_shapes=[pltpu.VMEM((PIPE_SETS, KEY_BLOCK, n), F32), pltpu.VMEM((PIPE_SETS, KEY_BLOCK, n), BF16),
                        pltpu.VMEM((PIPE_SETS, 1, n), F32), pltpu.VMEM((PIPE_SETS, 1, n), F32),
                        pltpu.VMEM((1, n), F32), pltpu.VMEM((V_ROWS, n), F32)],
        compiler_params=_cparams("arbitrary", "arbitrary", "arbitrary"),
        name="flash_diff" if diff else "flash_gqa",
    )(*args)


SHIFT_MARGIN = 1.01
SHIFT_DENOM_FLOOR = 2.0 ** -90


def _shift_flash_body(qt_ref, k_ref, vt_ref, kmax_ref, *rest, diff, tq, nkb, lam_init):
    if diff:
        lamqk_ref, gain_ref, o_ref, den_ref, p_sc, acc_sc = rest
    else:
        o_ref, den_ref, p_sc, acc_sc = rest
    row = lax.broadcasted_iota(jnp.int32, (LANES, tq), 0)
    if diff:
        base = (pl.program_id(1) % 2) * HEAD_DIM
        mask0 = (row >= base) & (row < base + C_QK_DIM)
        mask1 = (row >= base + C_QK_DIM) & (row < base + 2 * C_QK_DIM)
    else:
        mask0 = row < HEAD_DIM
        mask1 = row >= HEAD_DIM
    qt = qt_ref[0, 0].astype(F32)
    heads = [jnp.where(mask0, qt, 0.0), jnp.where(mask1, qt, 0.0)]
    qst = jnp.concatenate(heads, axis=1).astype(BF16)
    kmax = kmax_ref[0, 0]
    shift = jnp.concatenate(
        [jnp.sqrt(jnp.sum(hq * hq, axis=0, keepdims=True)) * kmax[i:i + 1, 0:1] for i, hq in enumerate(heads)],
        axis=1) * SHIFT_MARGIN

    acc_sc[...] = jnp.zeros(acc_sc.shape, F32)
    for slot in range(PIPE_SETS):
        p_sc[slot] = jnp.zeros(p_sc.shape[1:], BF16)
    last = nkb - 1

    def values(j, slot):
        acc_sc[...] += jnp.dot(vt_ref[0, 0, j], p_sc[slot], preferred_element_type=F32)

    def probs(j, slot):
        k = k_ref[0, 0, pl.ds(pl.multiple_of(j * KEY_BLOCK, KEY_BLOCK), KEY_BLOCK), :]
        s = jnp.dot(k, qst, preferred_element_type=F32)
        p_sc[slot] = jnp.exp2(s - shift).astype(BF16)

    def step(j, slot):
        values(jnp.maximum(j - PIPE_SETS, 0), slot)
        probs(j, slot)

    per_trip = _steps_per_trip(nkb, STEPS_PER_TRIP)

    def trip(i, carry):
        for r in range(per_trip):
            step(per_trip * i + r, r % PIPE_SETS)
        return carry

    lax.fori_loop(0, last // per_trip, trip, 0)
    step(last, 0)
    if last >= 1:
        values(last - 1, 1)
    values(last, 0)

    acc = acc_sc[...]
    den = acc[HEAD_DIM:HEAD_DIM + 1]
    den_ref[0, 0] = jnp.concatenate([den[:, :tq], den[:, tq:]], axis=0)
    o = acc[0:HEAD_DIM] / den
    o0 = o[:, :tq]
    o1 = o[:, tq:]
    if diff:
        lq = lamqk_ref[...]
        lam = (jnp.exp(jnp.sum(lq[0:1] * lq[1:2], axis=1, keepdims=True))
               - jnp.exp(jnp.sum(lq[2:3] * lq[3:4], axis=1, keepdims=True)) + lam_init)
        dlt = o0 - lam * o1
        ms = jnp.mean(dlt * dlt, axis=0, keepdims=True)
        y = dlt * lax.rsqrt(ms + 1e-6) * gain_ref[...] * (1.0 - lam_init)
        out_t = jnp.concatenate([y, jnp.zeros_like(y)], axis=0)
    else:
        out_t = jnp.concatenate([o0, o1], axis=0)
    o_ref[0] = out_t.T.astype(o_ref.dtype)


def _shift_flash(qt, k, vt, kmax, *, diff, tq, lam_qk=None, gain=None, lam_init=0.0):
    b, _, _, s = qt.shape
    units, nkb = vt.shape[1:3]
    nk = k.shape[2]
    assert nkb * KEY_BLOCK == nk and nkb % PIPE_SETS == 1
    n = 2 * tq
    ku = (lambda u: u // 2) if diff else (lambda u: u)
    in_specs = [pl.BlockSpec((1, 1, LANES, tq), lambda bi, u, i: (bi, ku(u), 0, i)),
                pl.BlockSpec((1, 1, nk, LANES), lambda bi, u, i: (bi, ku(u), 0, 0)),
                pl.BlockSpec((1, 1, nkb, V_ROWS, KEY_BLOCK), lambda bi, u, i: (bi, u, 0, 0, 0)),
                pl.BlockSpec((1, 1, 2, LANES), lambda bi, u, i: (bi, u, 0, 0))]
    args = [qt, k, vt, kmax]
    if diff:
        in_specs += [pl.BlockSpec(lam_qk.shape, lambda bi, u, i: (0, 0)),
                     pl.BlockSpec((HEAD_DIM, 1), lambda bi, u, i: (0, 0))]
        args += [lam_qk, gain]
    return pl.pallas_call(
        functools.partial(_shift_flash_body, diff=diff, tq=tq, nkb=nkb, lam_init=lam_init),
        grid=(b, units, s // tq),
        in_specs=in_specs,
        out_specs=[pl.BlockSpec((1, tq, LANES), lambda bi, u, i: (bi, i, u)),
                   pl.BlockSpec((1, 1, 2, tq), lambda bi, u, i: (bi, u, 0, i))],
        out_shape=[jax.ShapeDtypeStruct((b, s, units * LANES), BF16),
                   jax.ShapeDtypeStruct((b, units, 2, s), F32)],
        scratch_shapes=[pltpu.VMEM((PIPE_SETS, KEY_BLOCK, n), BF16), pltpu.VMEM((V_ROWS, n), F32)],
        compiler_params=_cparams("arbitrary", "arbitrary", "arbitrary"),
        name="shift_flash_diff" if diff else "shift_flash_gqa",
    )(*args)


def _key_norm_max(k, width):
    kf = k.astype(F32)
    sq = jnp.sum((kf * kf).reshape(k.shape[:3] + (LANES // width, width)), axis=-1)
    return jnp.sqrt(jnp.max(sq, axis=2))


def _attend(qt, k, vt, kmax, *, diff, tq, **kw):
    y, den = _shift_flash(qt, k, vt, kmax, diff=diff, tq=tq, **kw)
    ok = jnp.all(den >= SHIFT_DENOM_FLOOR)
    return lax.cond(ok, lambda: y, lambda: _flash(qt, k, vt, diff=diff, tq=tq, **kw))


def _mixout_body(x_ref, mod_ref, ya_ref, yc_ref, pdu_ref, prev_ref, next_ref, poolw_ref, pscale_ref,
                 sguw_ref, sgub_ref, wout_ref, gpost_ref, gpre_ref, xo_ref, tok_ref, *, tm, n):
    i = pl.program_id(1)
    nt = pl.num_programs(1)
    mod = mod_ref[0]
    pdu = pdu_ref[0]
    pb = pdu[:, 0:256]
    du = pdu[:, 256:512]
    dv = pdu[:, 512:768]

    prev = jnp.where(i > 0, prev_ref[0], 0.0)
    nxt = jnp.where(i < nt - 1, next_ref[0], 0.0)
    ext = jnp.concatenate([prev, pb, nxt], axis=0)
    rows = tm + 16
    up = lambda a, k: pltpu.roll(a, rows - k, 0)
    s2 = ext + up(ext, 1)
    s4 = s2 + up(s2, 2)
    s8 = s4 + up(s4, 4)
    s16 = s8 + up(s8, 8)
    lane = lax.broadcasted_iota(jnp.int32, (tm, 256), 1)
    grp = lane // 64
    win = jnp.where(grp == 0, up(s2, 7)[0:tm],
                    jnp.where(grp == 1, up(s4, 6)[0:tm],
                              jnp.where(grp == 2, up(s8, 4)[0:tm], s16[0:tm])))
    tok_idx = i * tm + lax.broadcasted_iota(jnp.int32, (tm, 256), 0)
    half = jnp.left_shift(1, grp)
    cnt = jnp.minimum(tok_idx + half, n) - jnp.maximum(tok_idx - half, 0)
    pooled = win / cnt.astype(F32) - pb
    pool = jnp.dot(pooled.astype(BF16), poolw_ref[...], preferred_element_type=F32) * pscale_ref[...]

    mu = jnp.mean(dv, axis=1, keepdims=True)
    dc = dv - mu
    vln = (dc * lax.rsqrt(jnp.mean(dc * dc, axis=1, keepdims=True) + 1e-5)).astype(BF16)
    head = lax.broadcasted_iota(jnp.int32, (CHUNK, 256), 1) // 64
    svs = []
    for c in range(tm // CHUNK):
        vch = vln[c * CHUNK:(c + 1) * CHUNK]
        sv = sgub_ref[...]
        for hd in range(4):
            r = jnp.dot(sguw_ref[hd], vch, preferred_element_type=F32)
            sv = sv + jnp.where(head == hd, r, 0.0)
        svs.append(sv)
    sgu = du * jnp.concatenate(svs, axis=0)

    ycat = jnp.concatenate([ya_ref[0], pool.astype(BF16), yc_ref[0], sgu.astype(BF16)], axis=1)
    o = jnp.dot(ycat, wout_ref[...], preferred_element_type=F32)
    xn = x_ref[0] + mod[2:3] * (_rms(o) * gpost_ref[...])
    xo_ref[0] = xn
    tok_ref[0] = _rms(xn) * gpre_ref[...] * (1.0 + mod[4:5]) + mod[3:4]


def _mixout(x, mod, ya, yc, pdu, poolw, pscale, sguw, sgub, wout, gpost, gpre, *, tm):
    b, s, d = x.shape
    t8 = tm // 8
    last8 = s // 8 - 1
    full = lambda shape: pl.BlockSpec(shape, lambda bi, i: (0,) * len(shape))
    row = lambda w: pl.BlockSpec((1, tm, w), lambda bi, i: (bi, i, 0))
    return pl.pallas_call(
        functools.partial(_mixout_body, tm=tm, n=s),
        grid=(b, s // tm),
        in_specs=[row(d), pl.BlockSpec((1, 6, d), lambda bi, i: (bi, 0, 0)),
                  row(256), row(512), row(768),
                  pl.BlockSpec((1, 8, 256), lambda bi, i: (bi, jnp.maximum(i * t8 - 1, 0), 0)),
                  pl.BlockSpec((1, 8, 256), lambda bi, i: (bi, jnp.minimum((i + 1) * t8, last8), 0)),
                  full((256, 256)), full((1, 256)), full(sguw.shape), full((CHUNK, 256)),
                  full(wout.shape), full((1, d)), full((1, d))],
        out_specs=[row(d), row(d)],
        out_shape=[jax.ShapeDtypeStruct((b, s, d), F32), jax.ShapeDtypeStruct((b, s, d), F32)],
        compiler_params=_cparams("arbitrary", "arbitrary"),
        name="mixout",
    )(x, mod, ya, yc, pdu, pdu, pdu, poolw, pscale, sguw, sgub, wout, gpost, gpre)


def _route(sel, scores, tm):
    per = N_EXPERTS // N_EXPERT_GROUPS
    i8 = lax.broadcasted_iota(jnp.int32, (per, tm), 0)
    gsc = []
    for g in range(N_EXPERT_GROUPS):
        blk = sel[g * per:(g + 1) * per]
        m1 = jnp.max(blk, axis=0, keepdims=True)
        i1 = jnp.min(jnp.where(blk == m1, i8, per), axis=0, keepdims=True)
        m2 = jnp.max(jnp.where(i8 == i1, NEG_INF, blk), axis=0, keepdims=True)
        gsc.append(m1 + m2)
    gs = jnp.concatenate(gsc, axis=0)
    g8 = lax.broadcasted_iota(jnp.int32, (N_EXPERT_GROUPS, tm), 0)
    gmask = jnp.zeros((N_EXPERT_GROUPS, tm), F32)
    for _ in range(TOPK_GROUPS):
        gm = jnp.max(gs, axis=0, keepdims=True)
        gi = jnp.min(jnp.where(gs == gm, g8, N_EXPERT_GROUPS), axis=0, keepdims=True)
        hit = g8 == gi
        gmask = jnp.where(hit, 1.0, gmask)
        gs = jnp.where(hit, NEG_INF, gs)
    ms = jnp.concatenate(
        [jnp.where(gmask[g:g + 1] > 0.0, sel[g * per:(g + 1) * per], NEG_INF) for g in range(N_EXPERT_GROUPS)],
        axis=0)
    e64 = lax.broadcasted_iota(jnp.int32, (N_EXPERTS, tm), 0)
    chosen = jnp.zeros((N_EXPERTS, tm), F32)
    for _ in range(TOP_K):
        m = jnp.max(ms, axis=0, keepdims=True)
        ii = jnp.min(jnp.where(ms == m, e64, N_EXPERTS), axis=0, keepdims=True)
        hit = e64 == ii
        chosen = jnp.where(hit, 1.0, chosen)
        ms = jnp.where(hit, NEG_INF, ms)
    w = chosen * scores
    return w / jnp.sum(w, axis=0, keepdims=True) * ROUTED_SCALE


def _moe_body(tok_ref, x_ref, mod_ref, rw_ref, rb_ref, w1_ref, w3_ref, w2_ref, s1_ref, s3_ref, s2_ref,
              gpost_ref, o_ref, hb_sc, gates_sc, acc_sc, *, tm):
    e = pl.program_id(2)

    @pl.when(e == 0)
    def _():
        h = tok_ref[0]
        hb = h.astype(BF16)
        hb_sc[...] = hb
        logits = jnp.dot(h, rw_ref[...], precision=lax.Precision.HIGHEST, preferred_element_type=F32)
        lt = logits.T[0:N_EXPERTS]
        scores = jax.nn.sigmoid(lt)
        gates_t = _route(scores + rb_ref[...], scores, tm)
        gates_sc[...] = jnp.concatenate([gates_t, jnp.zeros_like(gates_t)], axis=0).T
        a = jnp.dot(hb, s1_ref[...], preferred_element_type=F32)
        g = jnp.dot(hb, s3_ref[...], preferred_element_type=F32)
        acc_sc[...] = jnp.dot((_silu(a) * g).astype(BF16), s2_ref[...], preferred_element_type=F32)

    hb = hb_sc[...]
    lane = lax.broadcasted_iota(jnp.int32, (tm, LANES), 1)
    gates = gates_sc[...]
    acts = []
    for i in range(EXPERTS_PER_STEP):
        gate = jnp.sum(jnp.where(lane == e * EXPERTS_PER_STEP + i, gates, 0.0), axis=1, keepdims=True)
        a = jnp.dot(hb, w1_ref[0, i].astype(BF16), preferred_element_type=F32)
        g = jnp.dot(hb, w3_ref[0, i].astype(BF16), preferred_element_type=F32)
        acts.append((_silu(a) * g * gate).astype(BF16))
    hid = w2_ref.shape[2]
    w2 = w2_ref[0].reshape(EXPERTS_PER_STEP * hid, w2_ref.shape[3]).astype(BF16)
    acc_sc[...] += jnp.dot(jnp.concatenate(acts, axis=1), w2, preferred_element_type=F32)

    @pl.when(e == pl.num_programs(2) - 1)
    def _():
        mod = mod_ref[0]
        o_ref[0] = x_ref[0] + mod[5:6] * (_rms(acc_sc[...]) * gpost_ref[...])


def _moe(tok, x, mod, rw, rb, w1, w3, w2, s1, s3, s2, gpost, *, layer, tm):
    b, s, d = x.shape
    ne, _, hid = w1.shape[1:]
    full = lambda shape: pl.BlockSpec(shape, lambda bi, i, e: (0,) * len(shape))
    row = pl.BlockSpec((1, tm, d), lambda bi, i, e: (bi, i, 0))
    return pl.pallas_call(
        functools.partial(_moe_body, tm=tm),
        grid=(b, s // tm, ne // EXPERTS_PER_STEP),
        in_specs=[row, row, pl.BlockSpec((1, 6, d), lambda bi, i, e: (bi, 0, 0)),
                  full((d, LANES)), full((N_EXPERTS, 1)),
                  pl.BlockSpec((1, EXPERTS_PER_STEP, d, hid), lambda bi, i, e: (layer, e, 0, 0)),
                  pl.BlockSpec((1, EXPERTS_PER_STEP, d, hid), lambda bi, i, e: (layer, e, 0, 0)),
                  pl.BlockSpec((1, EXPERTS_PER_STEP, hid, d), lambda bi, i, e: (layer, e, 0, 0)),
                  full(s1.shape), full(s3.shape), full(s2.shape), full((1, d))],
        out_specs=row,
        out_shape=jax.ShapeDtypeStruct((b, s, d), F32),
        scratch_shapes=[pltpu.VMEM((tm, d), BF16), pltpu.VMEM((tm, LANES), F32), pltpu.VMEM((tm, d), F32)],
        compiler_params=_cparams("arbitrary", "arbitrary", "arbitrary"),
        name="moe",
    )(tok, x, mod, rw, rb, w1, w3, w2, s1, s3, s2, gpost)


SPARSE_TILE = 1024
SC_WINDOW = 128
PIECE = 256
HI16 = 0xFFFF0000


def _pack_pair(lo, hi):
    def bits(x):
        return lax.bitcast_convert_type(x.astype(BF16).astype(F32), jnp.uint32)
    return lax.bitcast_convert_type(bits(hi) | (bits(lo) >> 16), jnp.int32)


def _unpack_pair(word):
    u = lax.bitcast_convert_type(word, jnp.uint32)
    return (lax.bitcast_convert_type(u << 16, F32), lax.bitcast_convert_type(u & jnp.uint32(HI16), F32))


def _pack_row(x):
    return [_pack_pair(x[:, 2 * p * PIECE:(2 * p + 1) * PIECE], x[:, (2 * p + 1) * PIECE:(2 * p + 2) * PIECE])
            for p in range(2)]


def _unpack_row(p0, p1):
    return jnp.concatenate(_unpack_pair(p0) + _unpack_pair(p1), axis=1)


def _sc_mesh():
    from jax.experimental.pallas import tpu_sc as plsc
    return plsc.VectorSubcoreMesh(core_axis_name="core", subcore_axis_name="subcore")


def _sc_gather(table, idx):
    nb = idx.shape[0]
    d = table.shape[1]
    assert nb % SC_WINDOW == 0

    @functools.partial(pl.kernel, out_type=jax.ShapeDtypeStruct((nb, d), table.dtype), mesh=_sc_mesh())
    def gather_kernel(x_hbm, i_hbm, o_hbm):
        def body(i_vmem, o_vmem):
            pltpu.sync_copy(x_hbm.at[i_vmem.at[0]], o_vmem)

        pltpu.emit_pipeline(
            body,
            grid=(nb // SC_WINDOW,),
            in_specs=[pl.BlockSpec((1, SC_WINDOW), lambda i: (0, i))],
            out_specs=[pl.BlockSpec((SC_WINDOW, d), lambda i: (i, 0))],
            core_axis_name=("core", "subcore"),
            dimension_semantics=(pltpu.PARALLEL,),
        )(i_hbm, o_hbm)

    return gather_kernel(table, idx.reshape(1, nb))


def _sc_scatter(x, idx, out_rows):
    rounds, m = idx.shape
    d = x.shape[1]
    nblk = m // SC_WINDOW
    assert m % SC_WINDOW == 0

    @functools.partial(pl.kernel, out_type=jax.ShapeDtypeStruct((out_rows, d), x.dtype), mesh=_sc_mesh())
    def scatter_kernel(x_hbm, i_hbm, o_hbm):
        def body(x_vmem, i_vmem):
            for r in range(rounds):
                pltpu.sync_copy(x_vmem, o_hbm.at[i_vmem.at[r]])

        pltpu.emit_pipeline(
            body,
            grid=(nblk,),
            in_specs=[pl.BlockSpec((SC_WINDOW, d), lambda i: (i, 0)),
                      pl.BlockSpec((rounds, SC_WINDOW), lambda i: (0, i))],
            out_specs=[],
            core_axis_name=("core", "subcore"),
            dimension_semantics=(pltpu.PARALLEL,),
        )(x_hbm, i_hbm)

    return scatter_kernel(x, idx)


def _topk_route(sel, scores, tm):
    per = N_EXPERTS // N_EXPERT_GROUPS
    i8 = lax.broadcasted_iota(jnp.int32, (per, tm), 0)
    gsc = []
    for g in range(N_EXPERT_GROUPS):
        blk = sel[g * per:(g + 1) * per]
        m1 = jnp.max(blk, axis=0, keepdims=True)
        i1 = jnp.min(jnp.where(blk == m1, i8, per), axis=0, keepdims=True)
        m2 = jnp.max(jnp.where(i8 == i1, NEG_INF, blk), axis=0, keepdims=True)
        gsc.append(m1 + m2)
    gs = jnp.concatenate(gsc, axis=0)
    g8 = lax.broadcasted_iota(jnp.int32, (N_EXPERT_GROUPS, tm), 0)
    gmask = jnp.zeros((N_EXPERT_GROUPS, tm), F32)
    for _ in range(TOPK_GROUPS):
        gm = jnp.max(gs, axis=0, keepdims=True)
        gi = jnp.min(jnp.where(gs == gm, g8, N_EXPERT_GROUPS), axis=0, keepdims=True)
        hit = g8 == gi
        gmask = jnp.where(hit, 1.0, gmask)
        gs = jnp.where(hit, NEG_INF, gs)
    ms = jnp.concatenate(
        [jnp.where(gmask[g:g + 1] > 0.0, sel[g * per:(g + 1) * per], NEG_INF) for g in range(N_EXPERT_GROUPS)],
        axis=0)
    e64 = lax.broadcasted_iota(jnp.int32, (N_EXPERTS, tm), 0)
    hits, ids = [], []
    for _ in range(TOP_K):
        m = jnp.max(ms, axis=0, keepdims=True)
        ii = jnp.min(jnp.where(ms == m, e64, N_EXPERTS), axis=0, keepdims=True)
        hit = e64 == ii
        hits.append(hit)
        ids.append(ii)
        ms = jnp.where(hit, NEG_INF, ms)
    return hits, ids


def _route_body(tok_ref, rw_ref, rb_ref, tri_ref, tokp_ref, eidx_ref, posk_ref, wts_ref, cnt_ref, run_sc, *, tm):
    first = (pl.program_id(0) == 0) & (pl.program_id(1) == 0)

    @pl.when(first)
    def _():
        run_sc[...] = jnp.zeros(run_sc.shape, F32)

    h = tok_ref[0]
    pieces = _pack_row(h)
    tokp_ref[0] = pieces[0]
    tokp_ref[1] = pieces[1]

    rw = rw_ref[...]
    h_hi = h.astype(BF16)
    h_lo = (h - h_hi.astype(F32)).astype(BF16)
    w_hi = rw.astype(BF16)
    w_lo = (rw - w_hi.astype(F32)).astype(BF16)
    logits = (jnp.dot(h_hi, w_hi, preferred_element_type=F32) + jnp.dot(h_hi, w_lo, preferred_element_type=F32)
              + jnp.dot(h_lo, w_hi, preferred_element_type=F32))
    scores = jax.nn.sigmoid(logits.T[0:N_EXPERTS])
    hits, ids = _topk_route(scores + rb_ref[...], scores, tm)
    raw = [jnp.sum(jnp.where(hit, scores, 0.0), axis=0, keepdims=True) for hit in hits]
    denom = raw[0]
    for r in raw[1:]:
        denom = denom + r
    wts = jnp.concatenate([r / denom * ROUTED_SCALE for r in raw], axis=0)
    wts_ref[...] = jnp.concatenate([wts, jnp.zeros((LANES - TOP_K, tm), F32)], axis=0).T

    chosen = jnp.zeros((N_EXPERTS, tm), F32)
    for hit in hits:
        chosen = jnp.where(hit, 1.0, chosen)
    incl = jnp.dot(chosen.astype(BF16), tri_ref[...], preferred_element_type=F32)
    before = run_sc[...] + incl - chosen
    posk_ref[...] = jnp.concatenate(
        [jnp.sum(jnp.where(hit, before, 0.0), axis=0, keepdims=True) for hit in hits], axis=0).astype(jnp.int32)
    eidx_ref[...] = jnp.concatenate(ids, axis=0)
    run_sc[...] = run_sc[...] + jnp.sum(chosen, axis=1, keepdims=True)
    cnt_ref[...] = jnp.broadcast_to(run_sc[...], cnt_ref.shape)


def _route_tokens(tok, rw, rb, tri, *, tm):
    b, s, d = tok.shape
    n = b * s
    nt = s // tm
    full = lambda shape: pl.BlockSpec(shape, lambda bi, i: (0,) * len(shape))
    col = lambda rows: pl.BlockSpec((rows, tm), lambda bi, i: (0, bi * nt + i))
    return pl.pallas_call(
        functools.partial(_route_body, tm=tm),
        grid=(b, nt),
        in_specs=[pl.BlockSpec((1, tm, d), lambda bi, i: (bi, i, 0)),
                  full((d, LANES)), full((N_EXPERTS, 1)), full((tm, tm))],
        out_specs=[pl.BlockSpec((2, tm, PIECE), lambda bi, i: (0, bi * nt + i, 0)),
                   col(TOP_K), col(TOP_K),
                   pl.BlockSpec((tm, LANES), lambda bi, i: (bi * nt + i, 0)),
                   full((N_EXPERTS, LANES))],
        out_shape=[jax.ShapeDtypeStruct((2, n, PIECE), jnp.int32),
                   jax.ShapeDtypeStruct((TOP_K, n), jnp.int32),
                   jax.ShapeDtypeStruct((TOP_K, n), jnp.int32),
                   jax.ShapeDtypeStruct((n, LANES), F32),
                   jax.ShapeDtypeStruct((N_EXPERTS, LANES), F32)],
        scratch_shapes=[pltpu.VMEM((N_EXPERTS, 1), F32)],
        compiler_params=_cparams("arbitrary", "arbitrary"),
        name="route",
    )(tok, rw, rb, tri)


def _ffn_body(te_ref, tv_ref, x_ref, w1_ref, w3_ref, w2_ref, y_ref):
    valid = tv_ref[pl.program_id(0)]

    @pl.when(valid > 0)
    def _():
        x = _unpack_row(x_ref[0], x_ref[1])
        rows = lax.broadcasted_iota(jnp.int32, x.shape, 0)
        xb = jnp.where(rows < valid, x, 0.0).astype(BF16)
        a = jnp.dot(xb, w1_ref[0, 0].astype(BF16), preferred_element_type=F32)
        g = jnp.dot(xb, w3_ref[0, 0].astype(BF16), preferred_element_type=F32)
        y = jnp.dot((_silu(a) * g).astype(BF16), w2_ref[0, 0].astype(BF16), preferred_element_type=F32)
        pieces = _pack_row(y)
        y_ref[0] = pieces[0]
        y_ref[1] = pieces[1]


def _expert_ffn(tile_expert, tile_valid, xs, w1, w3, w2, *, layer):
    _, rows, _ = xs.shape
    d, hid = w1.shape[2:]
    blk = pl.BlockSpec((2, SPARSE_TILE, PIECE), lambda i, te, tv: (0, i, 0))
    return pl.pallas_call(
        _ffn_body,
        grid_spec=pltpu.PrefetchScalarGridSpec(
            num_scalar_prefetch=2,
            grid=(rows // SPARSE_TILE,),
            in_specs=[blk,
                      pl.BlockSpec((1, 1, d, hid), lambda i, te, tv: (layer, te[i], 0, 0)),
                      pl.BlockSpec((1, 1, d, hid), lambda i, te, tv: (layer, te[i], 0, 0)),
                      pl.BlockSpec((1, 1, hid, d), lambda i, te, tv: (layer, te[i], 0, 0))],
            out_specs=blk),
        out_shape=jax.ShapeDtypeStruct(xs.shape, jnp.int32),
        compiler_params=_cparams("arbitrary"),
        name="expert_ffn",
    )(tile_expert, tile_valid, xs, w1, w3, w2)


def _combine_body(tok_ref, x_ref, mod_ref, yg_ref, wts_ref, s1_ref, s3_ref, s2_ref, gpost_ref, o_ref):
    hb = tok_ref[0].astype(BF16)
    a = jnp.dot(hb, s1_ref[...], preferred_element_type=F32)
    g = jnp.dot(hb, s3_ref[...], preferred_element_type=F32)
    f = jnp.dot((_silu(a) * g).astype(BF16), s2_ref[...], preferred_element_type=F32)
    wts = wts_ref[...]
    for k in range(TOP_K):
        f = f + wts[:, k:k + 1] * _unpack_row(yg_ref[0, k], yg_ref[1, k])
    o_ref[0] = x_ref[0] + mod_ref[0][5:6] * (_rms(f) * gpost_ref[...])


def _combine(tok, x, mod, yg, wts, s1, s3, s2, gpost, *, tm):
    b, s, d = x.shape
    nt = s // tm
    full = lambda shape: pl.BlockSpec(shape, lambda bi, i: (0,) * len(shape))
    row = pl.BlockSpec((1, tm, d), lambda bi, i: (bi, i, 0))
    return pl.pallas_call(
        _combine_body,
        grid=(b, nt),
        in_specs=[row, row, pl.BlockSpec((1, 6, d), lambda bi, i: (bi, 0, 0)),
                  pl.BlockSpec((2, TOP_K, tm, PIECE), lambda bi, i: (0, 0, bi * nt + i, 0)),
                  pl.BlockSpec((tm, LANES), lambda bi, i: (bi * nt + i, 0)),
                  full(s1.shape), full(s3.shape), full(s2.shape), full((1, d))],
        out_specs=row,
        out_shape=jax.ShapeDtypeStruct((b, s, d), F32),
        compiler_params=_cparams("arbitrary", "arbitrary"),
        name="combine",
    )(tok, x, mod, yg, wts, s1, s3, s2, gpost)


def _dest_body(offs_ref, eidx_ref, posk_ref, dest_ref, *, rows):
    eidx = eidx_ref[...]
    off = jnp.zeros(eidx.shape, jnp.int32)
    for e in range(N_EXPERTS):
        off = jnp.where(eidx == e, offs_ref[e], off)
    dest = posk_ref[...] + off
    dest_ref[0] = dest
    dest_ref[1] = dest + rows


def _dest_rows(offs, eidx, posk, *, rows, tm):
    k, n = eidx.shape
    blk = pl.BlockSpec((k, tm), lambda i, offs: (0, i))
    return pl.pallas_call(
        functools.partial(_dest_body, rows=rows),
        grid_spec=pltpu.PrefetchScalarGridSpec(
            num_scalar_prefetch=1, grid=(n // tm,), in_specs=[blk, blk],
            out_specs=pl.BlockSpec((2, k, tm), lambda i, offs: (0, 0, i))),
        out_shape=jax.ShapeDtypeStruct((2, k, n), jnp.int32),
        compiler_params=_cparams("arbitrary"),
        name="dest_rows",
    )(offs, eidx, posk)


def _sparse_moe(tok, x, mod, rw, rb, w1, w3, w2, s1, s3, s2, gpost, *, layer, tm_route, tm_combine):
    b, s, d = x.shape
    n = b * s
    rows = n * TOP_K + N_EXPERTS * SPARSE_TILE
    ntiles = rows // SPARSE_TILE
    tri = jnp.triu(jnp.ones((tm_route, tm_route), BF16))
    tokp, eidx, posk, wts, cnt = _route_tokens(tok, rw, rb, tri, tm=tm_route)

    cnt = cnt[:, 0].astype(jnp.int32)
    padded = (cnt + SPARSE_TILE - 1) // SPARSE_TILE * SPARSE_TILE
    ar = jnp.arange(N_EXPERTS)
    ends = jnp.sum(jnp.where(ar[None, :] <= ar[:, None], padded[None, :], 0), axis=1)
    offs = ends - padded
    dest = _dest_rows(offs.astype(jnp.int32), eidx, posk, rows=rows, tm=4096)
    tile_start = jnp.arange(ntiles, dtype=jnp.int32) * SPARSE_TILE
    tile_expert = jnp.minimum(jnp.sum(tile_start[:, None] >= ends[None, :], axis=1), N_EXPERTS - 1).astype(jnp.int32)
    lo = jnp.maximum(offs[None, :], tile_start[:, None])
    hi = jnp.minimum((offs + cnt)[None, :], tile_start[:, None] + SPARSE_TILE)
    tile_valid = jnp.sum(jnp.maximum(hi - lo, 0), axis=1).astype(jnp.int32)

    sidx = jnp.concatenate([dest[0], dest[1]], axis=1)
    xs = _sc_scatter(tokp.reshape(2 * n, PIECE), sidx, 2 * rows).reshape(2, rows, PIECE)
    ys = _expert_ffn(tile_expert, tile_valid, xs, w1, w3, w2, layer=layer)
    yg = _sc_gather(ys.reshape(2 * rows, PIECE), dest.reshape(2 * TOP_K * n)).reshape(2, TOP_K, n, PIECE)
    return _combine(tok, x, mod, yg, wts, s1, s3, s2, gpost, tm=tm_combine)


def _rope_tables(s, dim):
    rows = s // GRID_W
    row = jnp.repeat(jnp.arange(rows, dtype=F32), GRID_W)
    col = jnp.tile(jnp.arange(GRID_W, dtype=F32), rows)
    half = dim // 2
    inv = ROPE_THETA ** (-jnp.arange(0, half, 2, dtype=F32) / half)
    ar = row[:, None] * inv[None, :]
    ac = col[:, None] * inv[None, :]
    ang = jnp.concatenate([ar, ar, ac, ac], axis=-1)
    sign = jnp.where((jnp.arange(dim) & (dim // 4)) == 0, -1.0, 1.0).astype(F32)
    reps = LANES // dim
    return jnp.tile(jnp.cos(ang), (1, reps)), jnp.tile(jnp.sin(ang) * sign, (1, reps))


def _block_diag(blocks):
    n = len(blocks)
    r, c = blocks[0].shape
    out = jnp.zeros((n * r, n * c), blocks[0].dtype)
    for i, blk in enumerate(blocks):
        out = out.at[i * r:(i + 1) * r, i * c:(i + 1) * c].set(blk)
    return out


def kernel(x, c, ctx, c_ctx, ada_w, ada_b, g_pre_mix, g_post_mix, g_pre_ffn, g_post_ffn, w_in, w_out, a_q_gain, a_k_gain, pool_w, pool_scale, lam_qk, c_subln_gain, sgu_w, sgu_b, router_w, router_bias, exp_w1, exp_w3, exp_w2, sh_w1, sh_w3, sh_w2):
    b, s, d = x.shape

    cvec = jnp.zeros((8, d), F32).at[0:b].set(c).at[b].set(c_ctx)
    mods = _ada(cvec, ada_w, ada_b)

    tabs = _rope_tables(s, HEAD_DIM) + _rope_tables(s, C_QK_DIM)
    bd = _block_diag([jnp.ones((HEAD_DIM, HEAD_DIM), BF16)] * 4)
    return _layers(x, ctx, mods, 0, b, tabs, bd, g_pre_mix, g_post_mix, g_pre_ffn, g_post_ffn, w_in, w_out,
                   a_q_gain, a_k_gain, pool_w, pool_scale, lam_qk, c_subln_gain, sgu_w, sgu_b, router_w,
                   router_bias, exp_w1, exp_w3, exp_w2, sh_w1, sh_w3, sh_w2)


def _layers(xl, xc, mods, lo, ctx_row, tabs, bd, g_pre_mix, g_post_mix, g_pre_ffn, g_post_ffn, w_in, w_out,
            a_q_gain, a_k_gain, pool_w, pool_scale, lam_qk, c_subln_gain, sgu_w, sgu_b, router_w, router_bias,
            exp_w1, exp_w3, exp_w2, sh_w1, sh_w3, sh_w2):
    b, s, d = xl.shape
    nctx = xc.shape[1]
    depth = w_in.shape[0]
    tm_lat = 512
    tq = 512
    tm_moe = 1024
    row2 = lambda v: v.reshape(1, -1)
    for l in range(depth):
        need_ctx = l < depth - 1
        lam_init = 0.8 - 0.6 * math.exp(-0.3 * l)
        m6 = mods[l].reshape(8, 6, d)
        mod_l = m6[lo:lo + b]
        mod_c = jnp.broadcast_to(m6[ctx_row:ctx_row + 1], (b, 6, d))

        w_in_l = w_in[l].astype(BF16)
        qg = jnp.tile(a_q_gain[l], 4).reshape(1, 256)
        kg = jnp.tile(a_k_gain[l], 2).reshape(1, LANES)
        inproj = functools.partial(_inproj, g_pre=row2(g_pre_mix[l]), w_in=w_in_l, q_gain=qg, k_gain=kg,
                                   tabs=tabs, bd=bd)
        qat_l, ka_l, vat_l, qct_l, kc_l, vct_l, pdu_l = inproj(xl, mod_l, rope=True, tm=tm_lat)
        qat_c, ka_c, vat_c, qct_c, kc_c, vct_c, pdu_c = inproj(xc, mod_c, rope=False, tm=nctx)

        ka = jnp.concatenate([ka_c, ka_l], axis=2)
        vat = jnp.concatenate([vat_c, vat_l], axis=2)
        kc = jnp.concatenate([kc_c, kc_l], axis=2)
        vct = jnp.concatenate([vct_c, vct_l], axis=2)
        sub_gain = c_subln_gain[l].reshape(HEAD_DIM, 1)

        lanes = lambda v: jnp.broadcast_to(v[..., None], v.shape + (LANES,))
        kmax_a = lanes(_key_norm_max(ka, HEAD_DIM))
        kmax_c = lanes(_key_norm_max(kc, C_QK_DIM).reshape(b, 4, 2))
        ya_l = _attend(qat_l, ka, vat, kmax_a, diff=False, tq=tq)
        yc_l = _attend(qct_l, kc, vct, kmax_c, diff=True, tq=tq, lam_qk=lam_qk[l], gain=sub_gain,
                       lam_init=lam_init)

        poolw = _block_diag([pool_w[l, g] for g in range(len(POOL_WINDOWS))]).astype(BF16)
        sgub = jnp.repeat(jnp.transpose(sgu_b[l]), d // 16, axis=1)
        wo = w_out[l]
        wo_c = jnp.pad(wo[512:768].reshape(4, HEAD_DIM, d), ((0, 0), (0, LANES - HEAD_DIM), (0, 0)))
        wout = jnp.concatenate([wo[0:512], wo_c.reshape(4 * LANES, d), wo[768:1024]], axis=0).astype(BF16)
        mixout = functools.partial(_mixout, poolw=poolw, pscale=row2(pool_scale[l]), sguw=sgu_w[l].astype(BF16),
                                   sgub=sgub, wout=wout, gpost=row2(g_post_mix[l]), gpre=row2(g_pre_ffn[l]))
        rw = jnp.pad(router_w[l], ((0, 0), (0, LANES - N_EXPERTS)))
        moe_args = dict(rw=rw, rb=router_bias[l].reshape(N_EXPERTS, 1), w1=exp_w1, w3=exp_w3,
                        w2=exp_w2, s1=sh_w1[l].astype(BF16), s3=sh_w3[l].astype(BF16),
                        s2=sh_w2[l].astype(BF16), gpost=row2(g_post_ffn[l]), layer=l)
        moe = functools.partial(_moe, **moe_args)

        xl_mid, tok_l = mixout(xl, mod_l, ya_l, yc_l, pdu_l, tm=tm_lat)
        xl = _sparse_moe(tok_l, xl_mid, mod_l, tm_route=tm_moe, tm_combine=tm_lat, **moe_args)
        if need_ctx:
            ya_c = _flash(qat_c, ka_c, vat_c, diff=False, tq=nctx)
            yc_c = _flash(qct_c, kc_c, vct_c, diff=True, tq=nctx, lam_qk=lam_qk[l], gain=sub_gain,
                          lam_init=lam_init)
            xc_mid, tok_c = mixout(xc, mod_c, ya_c, yc_c, pdu_c, tm=nctx)
            xc = moe(tok_c, xc_mid, mod_c, tm=nctx)
    return xl
```

```python
import functools
import math

import jax
import jax.numpy as jnp
from jax import lax
from jax.experimental import pallas as pl
from jax.experimental.pallas import tpu as pltpu

F32 = jnp.float32
BF16 = jnp.bfloat16

GRID_W = 64
ROPE_THETA = 10000.0
HEAD_DIM = 64
C_QK_DIM = 32
POOL_WINDOWS = (2, 4, 8, 16)
CHUNK = 128
N_EXPERTS = 64
TOP_K = 8
N_EXPERT_GROUPS = 8
TOPK_GROUPS = 4
ROUTED_SCALE = 2.5
EXPERTS_PER_STEP = 2

LANES = 128
KEY_BLOCK = 256
PIPE_SETS = 2
STEPS_PER_TRIP = 64
FALLBACK_STEPS_PER_TRIP = 16
V_ROWS = 80
VMEM_LIMIT = 56 * 1024 * 1024

NEG_INF = float("-inf")
LOG2E = math.log2(math.e)


def _cparams(*sem, flags=None):
    return pltpu.CompilerParams(dimension_semantics=sem, vmem_limit_bytes=VMEM_LIMIT, flags=flags)


def _rms(x, eps=1e-6):
    return x * lax.rsqrt(jnp.mean(x * x, axis=-1, keepdims=True) + eps)


def _segsum(sq, bd):
    hi = sq.astype(BF16)
    lo = (sq - hi.astype(F32)).astype(BF16)
    return (jnp.dot(hi, bd, preferred_element_type=F32)
            + jnp.dot(lo, bd, preferred_element_type=F32))


def _rope(x, cos, sin_signed, quarter):
    w = x.shape[1]
    lane = lax.broadcasted_iota(jnp.int32, x.shape, 1)
    first = (lane & quarter) == 0
    rot = jnp.where(first, pltpu.roll(x, w - quarter, 1), pltpu.roll(x, quarter, 1))
    return x * cos + rot * sin_signed


def _steps_per_trip(nkb, most):
    looped = max(nkb - 1, PIPE_SETS)
    return max([t for t in range(PIPE_SETS, most + 1, PIPE_SETS) if looped % t == 0], default=PIPE_SETS)


def _silu(x):
    return x * jax.nn.sigmoid(x)


def _ada_body(c_ref, w_ref, b_ref, o_ref):
    sc = _silu(c_ref[...])
    o_ref[0] = jnp.dot(sc, w_ref[0], precision=lax.Precision.HIGHEST,
                       preferred_element_type=F32) + b_ref[0]


def _ada(cvec, ada_w, ada_b):
    nl, d, d6 = ada_w.shape
    return pl.pallas_call(
        _ada_body,
        grid=(nl, d6 // d),
        in_specs=[pl.BlockSpec((8, d), lambda l, j: (0, 0)),
                  pl.BlockSpec((1, d, d), lambda l, j: (l, 0, j)),
                  pl.BlockSpec((1, 1, d), lambda l, j: (l, 0, j))],
        out_specs=pl.BlockSpec((1, 8, d), lambda l, j: (l, 0, j)),
        out_shape=jax.ShapeDtypeStruct((nl, 8, d6), F32),
        compiler_params=_cparams("arbitrary", "arbitrary"),
        name="ada",
    )(cvec, ada_w, ada_b.reshape(nl, 1, d6))


def _inproj_body(x_ref, mod_ref, g_ref, w_ref, qg_ref, kg_ref, ca_ref, sa_ref, cc_ref, sc_ref, bd_ref,
                 qat_ref, ka_ref, vat_ref, qct_ref, kc_ref, vct_ref, pdu_ref, *, rope, tm):
    x = x_ref[0]
    mod = mod_ref[0]
    h = _rms(x) * g_ref[...] * (1.0 + mod[1:2]) + mod[0:1]
    p = jnp.dot(h.astype(BF16), w_ref[...], preferred_element_type=F32)

    lane = lax.broadcasted_iota(jnp.int32, (tm, LANES), 1)
    low = lane < HEAD_DIM
    ones_col = (lane == HEAD_DIM).astype(F32)
    nkb = tm // KEY_BLOCK

    aq = p[:, 0:256]
    qn = aq * lax.rsqrt(_segsum(aq * aq, bd_ref[...]) * (1.0 / HEAD_DIM) + 1e-6) * qg_ref[...]
    if rope:
        ca = ca_ref[...]
        sa = sa_ref[...]
        qn = _rope(qn, jnp.concatenate([ca, ca], axis=1), jnp.concatenate([sa, sa], axis=1), HEAD_DIM // 4)
    qn = qn * (HEAD_DIM ** -0.5 * LOG2E)
    for kv in range(2):
        qat_ref[0, kv] = qn[:, kv * LANES:(kv + 1) * LANES].T.astype(BF16)

    ak = p[:, 256:384]
    kn = ak * lax.rsqrt(_segsum(ak * ak, bd_ref[0:LANES, 0:LANES]) * (1.0 / HEAD_DIM) + 1e-6) * kg_ref[...]
    if rope:
        kn = _rope(kn, ca_ref[...], sa_ref[...], HEAD_DIM // 4)
    ksw = pltpu.roll(kn, HEAD_DIM, 1)
    ka_ref[0, 0] = jnp.where(low, kn, ones_col).astype(BF16)
    ka_ref[0, 1] = jnp.where(low, ksw, ones_col).astype(BF16)

    def store_vt(ref, unit, vext):
        for j in range(nkb):
            ref[0, unit, j] = vext[j * KEY_BLOCK:(j + 1) * KEY_BLOCK].T[0:V_ROWS].astype(BF16)

    av = p[:, 384:512]
    store_vt(vat_ref, 0, jnp.where(low, av, ones_col))
    store_vt(vat_ref, 1, jnp.where(low, pltpu.roll(av, HEAD_DIM, 1), ones_col))

    cq = p[:, 768:1024]
    ck = p[:, 1024:1280]
    if rope:
        cc = cc_ref[...]
        sc = sc_ref[...]
        cc2 = jnp.concatenate([cc, cc], axis=1)
        sc2 = jnp.concatenate([sc, sc], axis=1)
        cq = _rope(cq, cc2, sc2, C_QK_DIM // 4)
        ck = _rope(ck, cc2, sc2, C_QK_DIM // 4)
    cq = cq * (C_QK_DIM ** -0.5 * LOG2E)
    for pr in range(2):
        qct_ref[0, pr] = cq[:, pr * LANES:(pr + 1) * LANES].T.astype(BF16)
        kc_ref[0, pr] = ck[:, pr * LANES:(pr + 1) * LANES].astype(BF16)
    cv = p[:, 1280:1536]
    for hd in range(4):
        seg = cv[:, (hd // 2) * LANES:(hd // 2 + 1) * LANES]
        if hd % 2:
            seg = pltpu.roll(seg, HEAD_DIM, 1)
        store_vt(vct_ref, hd, jnp.where(low, seg, ones_col))

    pdu_ref[0, :, 0:256] = p[:, 512:768]
    pdu_ref[0, :, 256:768] = p[:, 1536:2048]


def _inproj(x, mod, g_pre, w_in, q_gain, k_gain, tabs, bd, *, rope, tm):
    b, s, d = x.shape
    nkb = s // KEY_BLOCK
    tkb = tm // KEY_BLOCK
    full = lambda shape: pl.BlockSpec(shape, lambda bi, i: (0,) * len(shape))
    tab = pl.BlockSpec((tm, LANES), lambda bi, i: (i, 0))
    return pl.pallas_call(
        functools.partial(_inproj_body, rope=rope, tm=tm),
        grid=(b, s // tm),
        in_specs=[pl.BlockSpec((1, tm, d), lambda bi, i: (bi, i, 0)),
                  pl.BlockSpec((1, 6, d), lambda bi, i: (bi, 0, 0)),
                  full((1, d)), full(w_in.shape), full((1, 256)), full((1, LANES)),
                  tab, tab, tab, tab, full((256, 256))],
        out_specs=[pl.BlockSpec((1, 2, LANES, tm), lambda bi, i: (bi, 0, 0, i)),
                   pl.BlockSpec((1, 2, tm, LANES), lambda bi, i: (bi, 0, i, 0)),
                   pl.BlockSpec((1, 2, tkb, V_ROWS, KEY_BLOCK), lambda bi, i: (bi, 0, i, 0, 0)),
                   pl.BlockSpec((1, 2, LANES, tm), lambda bi, i: (bi, 0, 0, i)),
                   pl.BlockSpec((1, 2, tm, LANES), lambda bi, i: (bi, 0, i, 0)),
                   pl.BlockSpec((1, 4, tkb, V_ROWS, KEY_BLOCK), lambda bi, i: (bi, 0, i, 0, 0)),
                   pl.BlockSpec((1, tm, 768), lambda bi, i: (bi, i, 0))],
        out_shape=[jax.ShapeDtypeStruct((b, 2, LANES, s), BF16),
                   jax.ShapeDtypeStruct((b, 2, s, LANES), BF16),
                   jax.ShapeDtypeStruct((b, 2, nkb, V_ROWS, KEY_BLOCK), BF16),
                   jax.ShapeDtypeStruct((b, 2, LANES, s), BF16),
                   jax.ShapeDtypeStruct((b, 2, s, LANES), BF16),
                   jax.ShapeDtypeStruct((b, 4, nkb, V_ROWS, KEY_BLOCK), BF16),
                   jax.ShapeDtypeStruct((b, s, 768), F32)],
        compiler_params=_cparams("arbitrary", "arbitrary"),
        name="inproj",
    )(x, mod, g_pre, w_in, q_gain, k_gain, *tabs, bd)


def _query_heads(qt, masks):
    if masks is not None:
        return [jnp.where(m, qt, 0.0) for m in masks]
    pad = jnp.zeros((LANES - HEAD_DIM, qt.shape[1]), qt.dtype)
    return [jnp.concatenate([qt[0:HEAD_DIM], pad], axis=0), jnp.concatenate([qt[HEAD_DIM:], pad], axis=0)]


def _flash_body(qt_ref, k_ref, vt_ref, *rest, diff, tq, nkb, lam_init):
    if diff:
        lamqk_ref, gain_ref, o_ref, s_sc, p_sc, a_sc, b_sc, m_sc, acc_sc = rest
    else:
        o_ref, s_sc, p_sc, a_sc, b_sc, m_sc, acc_sc = rest
    row =lax.broadcasted_iota(jnp.int32, (LANES, tq), 0)
    if diff:
        base = (pl.program_id(1) % 2) * HEAD_DIM
        mask0 = (row >= base) & (row < base + C_QK_DIM)
        mask1 = (row >= base + C_QK_DIM) & (row < base + 2 * C_QK_DIM)
    qt = qt_ref[0, 0].astype(F32)
    qst = jnp.concatenate(_query_heads(qt, None if not diff else (mask0, mask1)), axis=1).astype(BF16)

    m_sc[...] = jnp.full(m_sc.shape, NEG_INF, F32)
    acc_sc[...] = jnp.zeros(acc_sc.shape, F32)
    for slot in range(PIPE_SETS):
        p_sc[slot] = jnp.zeros(p_sc.shape[1:], BF16)
        a_sc[slot] = jnp.ones(a_sc.shape[1:], F32)
    last = nkb - 1

    def scores(j, slot):
        k = k_ref[0, 0, pl.ds(pl.multiple_of(j * KEY_BLOCK, KEY_BLOCK), KEY_BLOCK), :]
        s = jnp.dot(k, qst, preferred_element_type=F32)
        s_sc[slot] = s
        b_sc[slot] = jnp.max(s, axis=0, keepdims=True)

    def softmax(slot):
        m_prev = m_sc[...]
        m_new = jnp.maximum(m_prev, b_sc[slot])
        a_sc[slot] = jnp.exp2(m_prev - m_new)
        p_sc[slot] = jnp.exp2((s_sc[slot] - m_new).astype(BF16))
        m_sc[...] = m_new

    def values(j, slot):
        pv = jnp.dot(vt_ref[0, 0, j], p_sc[slot], preferred_element_type=F32)
        acc_sc[...] = acc_sc[...] * a_sc[slot] + pv

    def step(j, slot, prefetch=True):
        values(jnp.maximum(j - PIPE_SETS, 0), slot)
        softmax(slot)
        if prefetch:
            scores(jnp.minimum(j + PIPE_SETS, last), slot)

    scores(0, 0)
    scores(jnp.minimum(1, last), 1)

    per_trip = _steps_per_trip(nkb, FALLBACK_STEPS_PER_TRIP)

    def trip(i, carry):
        for r in range(per_trip):
            step(per_trip * i + r, r % PIPE_SETS)
        return carry

    lax.fori_loop(0, last // per_trip, trip, 0)
    step(last, 0, prefetch=False)
    if last >= 1:
        values(last - 1, 1)
    values(last, 0)

    acc = acc_sc[...]
    o = acc[0:HEAD_DIM] / acc[HEAD_DIM:HEAD_DIM + 1]
    o0 = o[:, :tq]
    o1 = o[:, tq:]
    if diff:
        lq = lamqk_ref[...]
        lam = (jnp.exp(jnp.sum(lq[0:1] * lq[1:2], axis=1, keepdims=True))
               - jnp.exp(jnp.sum(lq[2:3] * lq[3:4], axis=1, keepdims=True)) + lam_init)
        dlt = o0 - lam * o1
        ms = jnp.mean(dlt * dlt, axis=0, keepdims=True)
        y = dlt * lax.rsqrt(ms + 1e-6) * gain_ref[...] * (1.0 - lam_init)
        out_t = jnp.concatenate([y, jnp.zeros_like(y)], axis=0)
    else:
        out_t = jnp.concatenate([o0, o1], axis=0)
    o_ref[0] = out_t.T.astype(o_ref.dtype)


def _flash(qt, k, vt, *, diff, tq, lam_qk=None, gain=None, lam_init=0.0):
    b, _, _, s = qt.shape
    units, nkb = vt.shape[1:3]
    nk = k.shape[2]
    assert nkb * KEY_BLOCK == nk and nkb % PIPE_SETS == 1
    n = 2 * tq
    ku = (lambda u: u // 2) if diff else (lambda u: u)
    in_specs = [pl.BlockSpec((1, 1, LANES, tq), lambda bi, u, i: (bi, ku(u), 0, i)),
                pl.BlockSpec((1, 1, nk, LANES), lambda bi, u, i: (bi, ku(u), 0, 0)),
                pl.BlockSpec((1, 1, nkb, V_ROWS, KEY_BLOCK), lambda bi, u, i: (bi, u, 0, 0, 0))]
    args = [qt, k, vt]
    if diff:
        in_specs += [pl.BlockSpec(lam_qk.shape, lambda bi, u, i: (0, 0)),
                     pl.BlockSpec((HEAD_DIM, 1), lambda bi, u, i: (0, 0))]
        args += [lam_qk, gain]
    return pl.pallas_call(
        functools.partial(_flash_body, diff=diff, tq=tq, nkb=nkb, lam_init=lam_init),
        grid=(b, units, s // tq),
        in_specs=in_specs,
        out_specs=pl.BlockSpec((1, tq, LANES), lambda bi, u, i: (bi, i, u)),
        out_shape=jax.ShapeDtypeStruct((b, s, units * LANES), BF16),
        scratch_shapes=[pltpu.VMEM((PIPE_SETS, KEY_BLOCK, n), F32), pltpu.VMEM((PIPE_SETS, KEY_BLOCK, n), BF16),
                        pltpu.VMEM((PIPE_SETS, 1, n), F32), pltpu.VMEM((PIPE_SETS, 1, n), F32),
                        pltpu.VMEM((1, n), F32), pltpu.VMEM((V_ROWS, n), F32)],
        compiler_params=_cparams("arbitrary", "arbitrary", "arbitrary"),
        name="flash_diff" if diff else "flash_gqa",
    )(*args)


SHIFT_MARGIN = 1.01
SHIFT_DENOM_FLOOR = 2.0 ** -90


def _shift_flash_body(qt_ref, k_ref, vt_ref, kmax_ref, *rest, diff, tq, nkb, lam_init):
    if diff:
        lamqk_ref, gain_ref, o_ref, den_ref, p_sc, acc_sc = rest
    else:
        o_ref, den_ref, p_sc, acc_sc = rest
    row = lax.broadcasted_iota(jnp.int32, (LANES, tq), 0)
    if diff:
        base = (pl.program_id(1) % 2) * HEAD_DIM
        mask0 = (row >= base) & (row < base + C_QK_DIM)
        mask1 = (row >= base + C_QK_DIM) & (row < base + 2 * C_QK_DIM)
    qt = qt_ref[0, 0].astype(F32)
    heads = _query_heads(qt, None if not diff else (mask0, mask1))
    kmax = kmax_ref[0, 0]
    shifts = [jnp.sqrt(jnp.sum(hq * hq, axis=0, keepdims=True)) * kmax[i:i + 1, 0:1] * SHIFT_MARGIN
              for i, hq in enumerate(heads)]
    if not diff:
        heads = [jnp.where(row == HEAD_DIM, -sh, hq) for sh, hq in zip(shifts, heads)]
    shift = jnp.concatenate(shifts, axis=1)
    qst = jnp.concatenate(heads, axis=1).astype(BF16)

    acc_sc[...] = jnp.zeros(acc_sc.shape, F32)
    for slot in range(PIPE_SETS):
        p_sc[slot] = jnp.zeros(p_sc.shape[1:], BF16)
    last = nkb - 1

    def values(j, slot):
        acc_sc[...] += jnp.dot(vt_ref[0, 0, j], p_sc[slot], preferred_element_type=F32)

    def probs(j, slot):
        k = k_ref[0, 0, pl.ds(pl.multiple_of(j * KEY_BLOCK, KEY_BLOCK), KEY_BLOCK), :]
        s = jnp.dot(k, qst, preferred_element_type=F32)
        p_sc[slot] = jnp.exp2(s - shift if diff else s).astype(BF16)

    def step(j, slot):
        values(jnp.maximum(j - PIPE_SETS, 0), slot)
        probs(j, slot)

    per_trip = _steps_per_trip(nkb, STEPS_PER_TRIP)

    def trip(i, carry):
        for r in range(per_trip):
            step(per_trip * i + r, r % PIPE_SETS)
        return carry

    lax.fori_loop(0, last // per_trip, trip, 0)
    step(last, 0)
    if last >= 1:
        values(last - 1, 1)
    values(last, 0)

    acc = acc_sc[...]
    den = acc[HEAD_DIM:HEAD_DIM + 1]
    den_ref[0, 0] = jnp.concatenate([den[:, :tq], den[:, tq:]], axis=0)
    o = acc[0:HEAD_DIM] / den
    o0 = o[:, :tq]
    o1 = o[:, tq:]
    if diff:
        lq = lamqk_ref[...]
        lam = (jnp.exp(jnp.sum(lq[0:1] * lq[1:2], axis=1, keepdims=True))
               - jnp.exp(jnp.sum(lq[2:3] * lq[3:4], axis=1, keepdims=True)) + lam_init)
        dlt = o0 - lam * o1
        ms = jnp.mean(dlt * dlt, axis=0, keepdims=True)
        y = dlt * lax.rsqrt(ms + 1e-6) * gain_ref[...] * (1.0 - lam_init)
        out_t = jnp.concatenate([y, jnp.zeros_like(y)], axis=0)
    else:
        out_t = jnp.concatenate([o0, o1], axis=0)
    o_ref[0] = out_t.T.astype(o_ref.dtype)


def _shift_flash(qt, k, vt, kmax, *, diff, tq, lam_qk=None, gain=None, lam_init=0.0):
    b, _, _, s = qt.shape
    units, nkb = vt.shape[1:3]
    nk = k.shape[2]
    assert nkb * KEY_BLOCK == nk and nkb % PIPE_SETS == 1
    n = 2 * tq
    ku = (lambda u: u // 2) if diff else (lambda u: u)
    in_specs = [pl.BlockSpec((1, 1, LANES, tq), lambda bi, u, i: (bi, ku(u), 0, i)),
                pl.BlockSpec((1, 1, nk, LANES), lambda bi, u, i: (bi, ku(u), 0, 0)),
                pl.BlockSpec((1, 1, nkb, V_ROWS, KEY_BLOCK), lambda bi, u, i: (bi, u, 0, 0, 0)),
                pl.BlockSpec((1, 1, 2, LANES), lambda bi, u, i: (bi, u, 0, 0))]
    args = [qt, k, vt, kmax]
    if diff:
        in_specs += [pl.BlockSpec(lam_qk.shape, lambda bi, u, i: (0, 0)),
                     pl.BlockSpec((HEAD_DIM, 1), lambda bi, u, i: (0, 0))]
        args += [lam_qk, gain]
    return pl.pallas_call(
        functools.partial(_shift_flash_body, diff=diff, tq=tq, nkb=nkb, lam_init=lam_init),
        grid=(b, units, s // tq),
        in_specs=in_specs,
        out_specs=[pl.BlockSpec((1, tq, LANES), lambda bi, u, i: (bi, i, u)),
                   pl.BlockSpec((1, 1, 2, tq), lambda bi, u, i: (bi, u, 0, i))],
        out_shape=[jax.ShapeDtypeStruct((b, s, units * LANES), BF16),
                   jax.ShapeDtypeStruct((b, units, 2, s), F32)],
        scratch_shapes=[pltpu.VMEM((PIPE_SETS, KEY_BLOCK, n), BF16), pltpu.VMEM((V_ROWS, n), F32)],
        compiler_params=_cparams("arbitrary", "arbitrary", "arbitrary"),
        name="shift_flash_diff" if diff else "shift_flash_gqa",
    )(*args)


def _key_norm_max(k, width):
    kf = k.astype(F32)
    sq = jnp.sum((kf * kf).reshape(k.shape[:3] + (LANES // width, width)), axis=-1)
    return jnp.sqrt(jnp.max(sq, axis=2))


def _attend(qt, k, vt, kmax, *, diff, tq, **kw):
    y, den = _shift_flash(qt, k, vt, kmax, diff=diff, tq=tq, **kw)
    ok = jnp.all(den >= SHIFT_DENOM_FLOOR)
    return lax.cond(ok, lambda: y, lambda: _flash(qt, k, vt, diff=diff, tq=tq, **kw))


def _mixout_body(x_ref, mod_ref, ya_ref, yc_ref, pdu_ref, prev_ref, next_ref, poolw_ref, pscale_ref,
                 sguw_ref, sgub_ref, wout_ref, gpost_ref, gpre_ref, xo_ref, tok_ref, *, tm, n):
    i = pl.program_id(1)
    nt = pl.num_programs(1)
    mod = mod_ref[0]
    pdu = pdu_ref[0]
    pb = pdu[:, 0:256]
    du = pdu[:, 256:512]
    dv = pdu[:, 512:768]

    prev = jnp.where(i > 0, prev_ref[0], 0.0)
    nxt = jnp.where(i < nt - 1, next_ref[0], 0.0)
    ext = jnp.concatenate([prev, pb, nxt], axis=0)
    rows = tm + 16
    up = lambda a, k: pltpu.roll(a, rows - k, 0)
    s2 = ext + up(ext, 1)
    s4 = s2 + up(s2, 2)
    s8 = s4 + up(s4, 4)
    s16 = s8 + up(s8, 8)
    lane = lax.broadcasted_iota(jnp.int32, (tm, 256), 1)
    grp = lane // 64
    win = jnp.where(grp == 0, up(s2, 7)[0:tm],
                    jnp.where(grp == 1, up(s4, 6)[0:tm],
                              jnp.where(grp == 2, up(s8, 4)[0:tm], s16[0:tm])))
    tok_idx = i * tm + lax.broadcasted_iota(jnp.int32, (tm, 256), 0)
    half = jnp.left_shift(1, grp)
    cnt = jnp.minimum(tok_idx + half, n) - jnp.maximum(tok_idx - half, 0)
    pooled = win / cnt.astype(F32) - pb
    pool = jnp.dot(pooled.astype(BF16), poolw_ref[...], preferred_element_type=F32) * pscale_ref[...]

    mu = jnp.mean(dv, axis=1, keepdims=True)
    dc = dv - mu
    vln = (dc * lax.rsqrt(jnp.mean(dc * dc, axis=1, keepdims=True) + 1e-5)).astype(BF16)
    head = lax.broadcasted_iota(jnp.int32, (CHUNK, 256), 1) // 64
    svs = []
    for c in range(tm // CHUNK):
        vch = vln[c * CHUNK:(c + 1) * CHUNK]
        sv = sgub_ref[...]
        for hd in range(4):
            r = jnp.dot(sguw_ref[hd], vch, preferred_element_type=F32)
            sv = sv + jnp.where(head == hd, r, 0.0)
        svs.append(sv)
    sgu = du * jnp.concatenate(svs, axis=0)

    ycat = jnp.concatenate([ya_ref[0], pool.astype(BF16), yc_ref[0], sgu.astype(BF16)], axis=1)
    o = jnp.dot(ycat, wout_ref[...], preferred_element_type=F32)
    xn = x_ref[0] + mod[2:3] * (_rms(o) * gpost_ref[...])
    xo_ref[0] = xn
    tok_ref[0] = _rms(xn) * gpre_ref[...] * (1.0 + mod[4:5]) + mod[3:4]


def _mixout(x, mod, ya, yc, pdu, poolw, pscale, sguw, sgub, wout, gpost, gpre, *, tm):
    b, s, d = x.shape
    t8 = tm // 8
    last8 = s // 8 - 1
    full = lambda shape: pl.BlockSpec(shape, lambda bi, i: (0,) * len(shape))
    row = lambda w: pl.BlockSpec((1, tm, w), lambda bi, i: (bi, i, 0))
    return pl.pallas_call(
        functools.partial(_mixout_body, tm=tm, n=s),
        grid=(b, s // tm),
        in_specs=[row(d), pl.BlockSpec((1, 6, d), lambda bi, i: (bi, 0, 0)),
                  row(256), row(512), row(768),
                  pl.BlockSpec((1, 8, 256), lambda bi, i: (bi, jnp.maximum(i * t8 - 1, 0), 0)),
                  pl.BlockSpec((1, 8, 256), lambda bi, i: (bi, jnp.minimum((i + 1) * t8, last8), 0)),
                  full((256, 256)), full((1, 256)), full(sguw.shape), full((CHUNK, 256)),
                  full(wout.shape), full((1, d)), full((1, d))],
        out_specs=[row(d), row(d)],
        out_shape=[jax.ShapeDtypeStruct((b, s, d), F32), jax.ShapeDtypeStruct((b, s, d), F32)],
        compiler_params=_cparams("arbitrary", "arbitrary"),
        name="mixout",
    )(x, mod, ya, yc, pdu, pdu, pdu, poolw, pscale, sguw, sgub, wout, gpost, gpre)


def _route(sel, scores, tm):
    per = N_EXPERTS // N_EXPERT_GROUPS
    i8 = lax.broadcasted_iota(jnp.int32, (per, tm), 0)
    gsc = []
    for g in range(N_EXPERT_GROUPS):
        blk = sel[g * per:(g + 1) * per]
        m1 = jnp.max(blk, axis=0, keepdims=True)
        i1 = jnp.min(jnp.where(blk == m1, i8, per), axis=0, keepdims=True)
        m2 = jnp.max(jnp.where(i8 == i1, NEG_INF, blk), axis=0, keepdims=True)
        gsc.append(m1 + m2)
    gs = jnp.concatenate(gsc, axis=0)
    g8 = lax.broadcasted_iota(jnp.int32, (N_EXPERT_GROUPS, tm), 0)
    gmask = jnp.zeros((N_EXPERT_GROUPS, tm), F32)
    for _ in range(TOPK_GROUPS):
        gm = jnp.max(gs, axis=0, keepdims=True)
        gi = jnp.min(jnp.where(gs == gm, g8, N_EXPERT_GROUPS), axis=0, keepdims=True)
        hit = g8 == gi
        gmask = jnp.where(hit, 1.0, gmask)
        gs = jnp.where(hit, NEG_INF, gs)
    ms = jnp.concatenate(
        [jnp.where(gmask[g:g + 1] > 0.0, sel[g * per:(g + 1) * per], NEG_INF) for g in range(N_EXPERT_GROUPS)],
        axis=0)
    e64 = lax.broadcasted_iota(jnp.int32, (N_EXPERTS, tm), 0)
    chosen = jnp.zeros((N_EXPERTS, tm), F32)
    for _ in range(TOP_K):
        m = jnp.max(ms, axis=0, keepdims=True)
        ii = jnp.min(jnp.where(ms == m, e64, N_EXPERTS), axis=0, keepdims=True)
        hit = e64 == ii
        chosen = jnp.where(hit, 1.0, chosen)
        ms = jnp.where(hit, NEG_INF, ms)
    w = chosen * scores
    return w / jnp.sum(w, axis=0, keepdims=True) * ROUTED_SCALE


def _moe_body(tok_ref, x_ref, mod_ref, rw_ref, rb_ref, w1_ref, w3_ref, w2_ref, s1_ref, s3_ref, s2_ref,
              gpost_ref, o_ref, hb_sc, gates_sc, acc_sc, *, tm):
    e = pl.program_id(2)

    @pl.when(e == 0)
    def _():
        h = tok_ref[0]
        hb = h.astype(BF16)
        hb_sc[...] = hb
        logits = jnp.dot(h, rw_ref[...], precision=lax.Precision.HIGHEST, preferred_element_type=F32)
        lt = logits.T[0:N_EXPERTS]
        scores = jax.nn.sigmoid(lt)
        gates_t = _route(scores + rb_ref[...], scores, tm)
        gates_sc[...] = jnp.concatenate([gates_t, jnp.zeros_like(gates_t)], axis=0).T
        a = jnp.dot(hb, s1_ref[...], preferred_element_type=F32)
        g = jnp.dot(hb, s3_ref[...], preferred_element_type=F32)
        acc_sc[...] = jnp.dot((_silu(a) * g).astype(BF16), s2_ref[...], preferred_element_type=F32)

    hb = hb_sc[...]
    lane = lax.broadcasted_iota(jnp.int32, (tm, LANES), 1)
    gates = gates_sc[...]
    acts = []
    for i in range(EXPERTS_PER_STEP):
        gate = jnp.sum(jnp.where(lane == e * EXPERTS_PER_STEP + i, gates, 0.0), axis=1, keepdims=True)
        a = jnp.dot(hb, w1_ref[0, i].astype(BF16), preferred_element_type=F32)
        g = jnp.dot(hb, w3_ref[0, i].astype(BF16), preferred_element_type=F32)
        acts.append((_silu(a) * g * gate).astype(BF16))
    hid = w2_ref.shape[2]
    w2 = w2_ref[0].reshape(EXPERTS_PER_STEP * hid, w2_ref.shape[3]).astype(BF16)
    acc_sc[...] += jnp.dot(jnp.concatenate(acts, axis=1), w2, preferred_element_type=F32)

    @pl.when(e == pl.num_programs(2) - 1)
    def _():
        mod = mod_ref[0]
        o_ref[0] = x_ref[0] + mod[5:6] * (_rms(acc_sc[...]) * gpost_ref[...])


def _moe(tok, x, mod, rw, rb, w1, w3, w2, s1, s3, s2, gpost, *, layer, tm):
    b, s, d = x.shape
    ne, _, hid = w1.shape[1:]
    full = lambda shape: pl.BlockSpec(shape, lambda bi, i, e: (0,) * len(shape))
    row = pl.BlockSpec((1, tm, d), lambda bi, i, e: (bi, i, 0))
    return pl.pallas_call(
        functools.partial(_moe_body, tm=tm),
        grid=(b, s // tm, ne // EXPERTS_PER_STEP),
        in_specs=[row, row, pl.BlockSpec((1, 6, d), lambda bi, i, e: (bi, 0, 0)),
                  full((d, LANES)), full((N_EXPERTS, 1)),
                  pl.BlockSpec((1, EXPERTS_PER_STEP, d, hid), lambda bi, i, e: (layer, e, 0, 0)),
                  pl.BlockSpec((1, EXPERTS_PER_STEP, d, hid), lambda bi, i, e: (layer, e, 0, 0)),
                  pl.BlockSpec((1, EXPERTS_PER_STEP, hid, d), lambda bi, i, e: (layer, e, 0, 0)),
                  full(s1.shape), full(s3.shape), full(s2.shape), full((1, d))],
        out_specs=row,
        out_shape=jax.ShapeDtypeStruct((b, s, d), F32),
        scratch_shapes=[pltpu.VMEM((tm, d), BF16), pltpu.VMEM((tm, LANES), F32), pltpu.VMEM((tm, d), F32)],
        compiler_params=_cparams("arbitrary", "arbitrary", "arbitrary"),
        name="moe",
    )(tok, x, mod, rw, rb, w1, w3, w2, s1, s3, s2, gpost)


SPARSE_TILE = 1024
SC_WINDOW = 128
PIECE = 256
HI16 = 0xFFFF0000


def _pack_pair(lo, hi):
    def rne(x):
        u = lax.bitcast_convert_type(x, jnp.uint32)
        return u + jnp.uint32(0x7FFF) + ((u >> 16) & jnp.uint32(1))
    word = (rne(hi) & jnp.uint32(HI16)) | (rne(lo) >> 16)
    return lax.bitcast_convert_type(word, jnp.int32)


def _unpack_pair(word):
    u = lax.bitcast_convert_type(word, jnp.uint32)
    return (lax.bitcast_convert_type(u << 16, F32), lax.bitcast_convert_type(u & jnp.uint32(HI16), F32))


def _pack_row(x):
    return [_pack_pair(x[:, 2 * p * PIECE:(2 * p + 1) * PIECE], x[:, (2 * p + 1) * PIECE:(2 * p + 2) * PIECE])
            for p in range(2)]


def _unpack_row(p0, p1):
    return jnp.concatenate(_unpack_pair(p0) + _unpack_pair(p1), axis=1)


def _sc_mesh():
    from jax.experimental.pallas import tpu_sc as plsc
    return plsc.VectorSubcoreMesh(core_axis_name="core", subcore_axis_name="subcore")


def _sc_gather(table, idx):
    nb = idx.shape[0]
    d = table.shape[1]
    assert nb % SC_WINDOW == 0

    @functools.partial(pl.kernel, out_type=jax.ShapeDtypeStruct((nb, d), table.dtype), mesh=_sc_mesh())
    def gather_kernel(x_hbm, i_hbm, o_hbm):
        def body(i_vmem, o_vmem):
            pltpu.sync_copy(x_hbm.at[i_vmem.at[0]], o_vmem)

        pltpu.emit_pipeline(
            body,
            grid=(nb // SC_WINDOW,),
            in_specs=[pl.BlockSpec((1, SC_WINDOW), lambda i: (0, i))],
            out_specs=[pl.BlockSpec((SC_WINDOW, d), lambda i: (i, 0))],
            core_axis_name=("core", "subcore"),
            dimension_semantics=(pltpu.PARALLEL,),
        )(i_hbm, o_hbm)

    return gather_kernel(table, idx.reshape(1, nb))


def _sc_scatter(x, idx, out_rows):
    rounds, m = idx.shape
    d = x.shape[1]
    nblk = m // SC_WINDOW
    assert m % SC_WINDOW == 0

    @functools.partial(pl.kernel, out_type=jax.ShapeDtypeStruct((out_rows, d), x.dtype), mesh=_sc_mesh())
    def scatter_kernel(x_hbm, i_hbm, o_hbm):
        def body(x_vmem, i_vmem):
            for r in range(rounds):
                pltpu.sync_copy(x_vmem, o_hbm.at[i_vmem.at[r]])

        pltpu.emit_pipeline(
            body,
            grid=(nblk,),
            in_specs=[pl.BlockSpec((SC_WINDOW, d), lambda i: (i, 0)),
                      pl.BlockSpec((rounds, SC_WINDOW), lambda i: (0, i))],
            out_specs=[],
            core_axis_name=("core", "subcore"),
            dimension_semantics=(pltpu.PARALLEL,),
        )(x_hbm, i_hbm)

    return scatter_kernel(x, idx)


def _topk_route(sel, scores, tm):
    per = N_EXPERTS // N_EXPERT_GROUPS
    i8 = lax.broadcasted_iota(jnp.int32, (per, tm), 0)
    gsc = []
    for g in range(N_EXPERT_GROUPS):
        blk = sel[g * per:(g + 1) * per]
        m1 = jnp.max(blk, axis=0, keepdims=True)
        i1 = jnp.min(jnp.where(blk == m1, i8, per), axis=0, keepdims=True)
        m2 = jnp.max(jnp.where(i8 == i1, NEG_INF, blk), axis=0, keepdims=True)
        gsc.append(m1 + m2)
    gs = jnp.concatenate(gsc, axis=0)
    g8 = lax.broadcasted_iota(jnp.int32, (N_EXPERT_GROUPS, tm), 0)
    gmask = jnp.zeros((N_EXPERT_GROUPS, tm), F32)
    for _ in range(TOPK_GROUPS):
        gm = jnp.max(gs, axis=0, keepdims=True)
        gi = jnp.min(jnp.where(gs == gm, g8, N_EXPERT_GROUPS), axis=0, keepdims=True)
        hit = g8 == gi
        gmask = jnp.where(hit, 1.0, gmask)
        gs = jnp.where(hit, NEG_INF, gs)
    ms = jnp.concatenate(
        [jnp.where(gmask[g:g + 1] > 0.0, sel[g * per:(g + 1) * per], NEG_INF) for g in range(N_EXPERT_GROUPS)],
        axis=0)
    e64 = lax.broadcasted_iota(jnp.int32, (N_EXPERTS, tm), 0)
    hits, ids = [], []
    for _ in range(TOP_K):
        m = jnp.max(ms, axis=0, keepdims=True)
        ii = jnp.min(jnp.where(ms == m, e64, N_EXPERTS), axis=0, keepdims=True)
        hit = e64 == ii
        hits.append(hit)
        ids.append(ii)
        ms = jnp.where(hit, NEG_INF, ms)
    return hits, ids


def _route_body(tok_ref, rw_ref, rb_ref, tri_ref, tokp_ref, eidx_ref, posk_ref, wts_ref, cnt_ref, run_sc, *, tm):
    first = (pl.program_id(0) == 0) & (pl.program_id(1) == 0)

    @pl.when(first)
    def _():
        run_sc[...] = jnp.zeros(run_sc.shape, F32)

    h = tok_ref[0]
    pieces = _pack_row(h)
    tokp_ref[0] = pieces[0]
    tokp_ref[1] = pieces[1]

    rw = rw_ref[...]
    h_hi = h.astype(BF16)
    h_lo = (h - h_hi.astype(F32)).astype(BF16)
    w_hi = rw.astype(BF16)
    w_lo = (rw - w_hi.astype(F32)).astype(BF16)
    logits = (jnp.dot(h_hi, w_hi, preferred_element_type=F32) + jnp.dot(h_hi, w_lo, preferred_element_type=F32)
              + jnp.dot(h_lo, w_hi, preferred_element_type=F32))
    scores = jax.nn.sigmoid(logits.T[0:N_EXPERTS])
    hits, ids = _topk_route(scores + rb_ref[...], scores, tm)
    raw = [jnp.sum(jnp.where(hit, scores, 0.0), axis=0, keepdims=True) for hit in hits]
    denom = raw[0]
    for r in raw[1:]:
        denom = denom + r
    wts = jnp.concatenate([r / denom * ROUTED_SCALE for r in raw], axis=0)
    wts_ref[...] = jnp.concatenate([wts, jnp.zeros((LANES - TOP_K, tm), F32)], axis=0).T

    chosen = jnp.zeros((N_EXPERTS, tm), F32)
    for hit in hits:
        chosen = jnp.where(hit, 1.0, chosen)
    incl = jnp.dot(chosen.astype(BF16), tri_ref[...], preferred_element_type=F32)
    before = run_sc[...] + incl - chosen
    posk_ref[...] = jnp.concatenate(
        [jnp.sum(jnp.where(hit, before, 0.0), axis=0, keepdims=True) for hit in hits], axis=0).astype(jnp.int32)
    eidx_ref[...] = jnp.concatenate(ids, axis=0)
    run_sc[...] = run_sc[...] + jnp.sum(chosen, axis=1, keepdims=True)
    cnt_ref[...] = jnp.broadcast_to(run_sc[...], cnt_ref.shape)


def _route_tokens(tok, rw, rb, tri, *, tm):
    b, s, d = tok.shape
    n = b * s
    nt = s // tm
    full = lambda shape: pl.BlockSpec(shape, lambda bi, i: (0,) * len(shape))
    col = lambda rows: pl.BlockSpec((rows, tm), lambda bi, i: (0, bi * nt + i))
    return pl.pallas_call(
        functools.partial(_route_body, tm=tm),
        grid=(b, nt),
        in_specs=[pl.BlockSpec((1, tm, d), lambda bi, i: (bi, i, 0)),
                  full((d, LANES)), full((N_EXPERTS, 1)), full((tm, tm))],
        out_specs=[pl.BlockSpec((2, tm, PIECE), lambda bi, i: (0, bi * nt + i, 0)),
                   col(TOP_K), col(TOP_K),
                   pl.BlockSpec((tm, LANES), lambda bi, i: (bi * nt + i, 0)),
                   full((N_EXPERTS, LANES))],
        out_shape=[jax.ShapeDtypeStruct((2, n, PIECE), jnp.int32),
                   jax.ShapeDtypeStruct((TOP_K, n), jnp.int32),
                   jax.ShapeDtypeStruct((TOP_K, n), jnp.int32),
                   jax.ShapeDtypeStruct((n, LANES), F32),
                   jax.ShapeDtypeStruct((N_EXPERTS, LANES), F32)],
        scratch_shapes=[pltpu.VMEM((N_EXPERTS, 1), F32)],
        compiler_params=_cparams("arbitrary", "arbitrary"),
        name="route",
    )(tok, rw, rb, tri)


def _ffn_body(te_ref, tv_ref, x_ref, w1_ref, w3_ref, w2_ref, y_ref):
    valid = tv_ref[pl.program_id(0)]

    @pl.when(valid > 0)
    def _():
        x = _unpack_row(x_ref[0], x_ref[1])
        rows = lax.broadcasted_iota(jnp.int32, x.shape, 0)
        xb = jnp.where(rows < valid, x, 0.0).astype(BF16)
        a = jnp.dot(xb, w1_ref[0, 0].astype(BF16), preferred_element_type=F32)
        g = jnp.dot(xb, w3_ref[0, 0].astype(BF16), preferred_element_type=F32)
        y = jnp.dot((_silu(a) * g).astype(BF16), w2_ref[0, 0].astype(BF16), preferred_element_type=F32)
        pieces = _pack_row(y)
        y_ref[0] = pieces[0]
        y_ref[1] = pieces[1]


def _expert_ffn(tile_expert, tile_valid, xs, w1, w3, w2, *, layer):
    _, rows, _ = xs.shape
    d, hid = w1.shape[2:]
    blk = pl.BlockSpec((2, SPARSE_TILE, PIECE), lambda i, te, tv: (0, i, 0))
    return pl.pallas_call(
        _ffn_body,
        grid_spec=pltpu.PrefetchScalarGridSpec(
            num_scalar_prefetch=2,
            grid=(rows // SPARSE_TILE,),
            in_specs=[blk,
                      pl.BlockSpec((1, 1, d, hid), lambda i, te, tv: (layer, te[i], 0, 0)),
                      pl.BlockSpec((1, 1, d, hid), lambda i, te, tv: (layer, te[i], 0, 0)),
                      pl.BlockSpec((1, 1, hid, d), lambda i, te, tv: (layer, te[i], 0, 0))],
            out_specs=blk),
        out_shape=jax.ShapeDtypeStruct(xs.shape, jnp.int32),
        compiler_params=_cparams("arbitrary"),
        name="expert_ffn",
    )(tile_expert, tile_valid, xs, w1, w3, w2)


def _combine_body(tok_ref, x_ref, mod_ref, yg_ref, wts_ref, s1_ref, s3_ref, s2_ref, gpost_ref, o_ref):
    hb = tok_ref[0].astype(BF16)
    a = jnp.dot(hb, s1_ref[...], preferred_element_type=F32)
    g = jnp.dot(hb, s3_ref[...], preferred_element_type=F32)
    f = jnp.dot((_silu(a) * g).astype(BF16), s2_ref[...], preferred_element_type=F32)
    wts = wts_ref[...]
    for k in range(TOP_K):
        f = f + wts[:, k:k + 1] * _unpack_row(yg_ref[0, k], yg_ref[1, k])
    o_ref[0] = x_ref[0] + mod_ref[0][5:6] * (_rms(f) * gpost_ref[...])


def _combine(tok, x, mod, yg, wts, s1, s3, s2, gpost, *, tm):
    b, s, d = x.shape
    nt = s // tm
    full = lambda shape: pl.BlockSpec(shape, lambda bi, i: (0,) * len(shape))
    row = pl.BlockSpec((1, tm, d), lambda bi, i: (bi, i, 0))
    return pl.pallas_call(
        _combine_body,
        grid=(b, nt),
        in_specs=[row, row, pl.BlockSpec((1, 6, d), lambda bi, i: (bi, 0, 0)),
                  pl.BlockSpec((2, TOP_K, tm, PIECE), lambda bi, i: (0, 0, bi * nt + i, 0)),
                  pl.BlockSpec((tm, LANES), lambda bi, i: (bi * nt + i, 0)),
                  full(s1.shape), full(s3.shape), full(s2.shape), full((1, d))],
        out_specs=row,
        out_shape=jax.ShapeDtypeStruct((b, s, d), F32),
        compiler_params=_cparams("arbitrary", "arbitrary"),
        name="combine",
    )(tok, x, mod, yg, wts, s1, s3, s2, gpost)


def _dest_body(offs_ref, eidx_ref, posk_ref, dest_ref):
    eidx = eidx_ref[...]
    off = jnp.zeros(eidx.shape, jnp.int32)
    for e in range(N_EXPERTS):
        off = jnp.where(eidx == e, offs_ref[e], off)
    dest_ref[...] = posk_ref[...] + off


def _dest_rows(offs, eidx, posk, *, tm):
    k, n = eidx.shape
    blk = pl.BlockSpec((k, tm), lambda i, offs: (0, i))
    return pl.pallas_call(
        _dest_body,
        grid_spec=pltpu.PrefetchScalarGridSpec(num_scalar_prefetch=1, grid=(n // tm,), in_specs=[blk, blk],
                                               out_specs=blk),
        out_shape=jax.ShapeDtypeStruct((k, n), jnp.int32),
        compiler_params=_cparams("arbitrary"),
        name="dest_rows",
    )(offs, eidx, posk)


def _sparse_moe(tok, x, mod, rw, rb, w1, w3, w2, s1, s3, s2, gpost, *, layer, tm_route, tm_combine):
    b, s, d = x.shape
    n = b * s
    rows = n * TOP_K + N_EXPERTS * SPARSE_TILE
    ntiles = rows // SPARSE_TILE
    tri = jnp.triu(jnp.ones((tm_route, tm_route), BF16))
    tokp, eidx, posk, wts, cnt = _route_tokens(tok, rw, rb, tri, tm=tm_route)

    cnt = cnt[:, 0].astype(jnp.int32)
    padded = (cnt + SPARSE_TILE - 1) // SPARSE_TILE * SPARSE_TILE
    ends = jnp.cumsum(padded)
    offs = ends - padded
    dest = _dest_rows(offs.astype(jnp.int32), eidx, posk, tm=4096)
    tile_start = jnp.arange(ntiles, dtype=jnp.int32) * SPARSE_TILE
    tile_expert = jnp.minimum(jnp.sum(tile_start[:, None] >= ends[None, :], axis=1), N_EXPERTS - 1).astype(jnp.int32)
    tile_valid = jnp.clip(cnt[tile_expert] - (tile_start - offs[tile_expert]), 0, SPARSE_TILE).astype(jnp.int32)

    piece_base = jnp.arange(2, dtype=jnp.int32) * rows
    sidx = (dest[:, None, :] + piece_base[None, :, None]).reshape(TOP_K, 2 * n)
    xs = _sc_scatter(tokp.reshape(2 * n, PIECE), sidx, 2 * rows).reshape(2, rows, PIECE)
    ys = _expert_ffn(tile_expert, tile_valid, xs, w1, w3, w2, layer=layer)
    gidx = (piece_base[:, None, None] + dest[None, :, :]).reshape(2 * TOP_K * n)
    yg = _sc_gather(ys.reshape(2 * rows, PIECE), gidx).reshape(2, TOP_K, n, PIECE)
    return _combine(tok, x, mod, yg, wts, s1, s3, s2, gpost, tm=tm_combine)


def _rope_tables(s, dim):
    rows = s // GRID_W
    row = jnp.repeat(jnp.arange(rows, dtype=F32), GRID_W)
    col = jnp.tile(jnp.arange(GRID_W, dtype=F32), rows)
    half = dim // 2
    inv = ROPE_THETA ** (-jnp.arange(0, half, 2, dtype=F32) / half)
    ar = row[:, None] * inv[None, :]
    ac = col[:, None] * inv[None, :]
    ang = jnp.concatenate([ar, ar, ac, ac], axis=-1)
    sign = jnp.where((jnp.arange(dim) & (dim // 4)) == 0, -1.0, 1.0).astype(F32)
    reps = LANES // dim
    return jnp.tile(jnp.cos(ang), (1, reps)), jnp.tile(jnp.sin(ang) * sign, (1, reps))


def _block_diag(blocks):
    n = len(blocks)
    r, c = blocks[0].shape
    out = jnp.zeros((n * r, n * c), blocks[0].dtype)
    for i, blk in enumerate(blocks):
        out = out.at[i * r:(i + 1) * r, i * c:(i + 1) * c].set(blk)
    return out


def kernel(x, c, ctx, c_ctx, ada_w, ada_b, g_pre_mix, g_post_mix, g_pre_ffn, g_post_ffn, w_in, w_out, a_q_gain, a_k_gain, pool_w, pool_scale, lam_qk, c_subln_gain, sgu_w, sgu_b, router_w, router_bias, exp_w1, exp_w3, exp_w2, sh_w1, sh_w3, sh_w2):
    b, s, d = x.shape

    cvec = jnp.zeros((8, d), F32).at[0:b].set(c).at[b].set(c_ctx)
    mods = _ada(cvec, ada_w, ada_b)

    tabs = _rope_tables(s, HEAD_DIM) + _rope_tables(s, C_QK_DIM)
    bd = _block_diag([jnp.ones((HEAD_DIM, HEAD_DIM), BF16)] * 4)
    return _layers(x, ctx, mods, 0, b, tabs, bd, g_pre_mix, g_post_mix, g_pre_ffn, g_post_ffn, w_in, w_out,
                   a_q_gain, a_k_gain, pool_w, pool_scale, lam_qk, c_subln_gain, sgu_w, sgu_b, router_w,
                   router_bias, exp_w1, exp_w3, exp_w2, sh_w1, sh_w3, sh_w2)


def _layers(xl, xc, mods, lo, ctx_row, tabs, bd, g_pre_mix, g_post_mix, g_pre_ffn, g_post_ffn, w_in, w_out,
            a_q_gain, a_k_gain, pool_w, pool_scale, lam_qk, c_subln_gain, sgu_w, sgu_b, router_w, router_bias,
            exp_w1, exp_w3, exp_w2, sh_w1, sh_w3, sh_w2):
    b, s, d = xl.shape
    nctx = xc.shape[1]
    depth = w_in.shape[0]
    tm_lat = 512
    tq = 512
    tm_moe = 1024
    row2 = lambda v: v.reshape(1, -1)
    for l in range(depth):
        need_ctx = l < depth - 1
        lam_init = 0.8 - 0.6 * math.exp(-0.3 * l)
        m6 = mods[l].reshape(8, 6, d)
        mod_l = m6[lo:lo + b]
        mod_c = jnp.broadcast_to(m6[ctx_row:ctx_row + 1], (b, 6, d))

        w_in_l = w_in[l].astype(BF16)
        qg = jnp.tile(a_q_gain[l], 4).reshape(1, 256)
        kg = jnp.tile(a_k_gain[l], 2).reshape(1, LANES)
        inproj = functools.partial(_inproj, g_pre=row2(g_pre_mix[l]), w_in=w_in_l, q_gain=qg, k_gain=kg,
                                   tabs=tabs, bd=bd)
        qat_l, ka_l, vat_l, qct_l, kc_l, vct_l, pdu_l = inproj(xl, mod_l, rope=True, tm=tm_lat)
        qat_c, ka_c, vat_c, qct_c, kc_c, vct_c, pdu_c = inproj(xc, mod_c, rope=False, tm=nctx)

        ka = jnp.concatenate([ka_c, ka_l], axis=2)
        vat = jnp.concatenate([vat_c, vat_l], axis=2)
        kc = jnp.concatenate([kc_c, kc_l], axis=2)
        vct = jnp.concatenate([vct_c, vct_l], axis=2)
        sub_gain = c_subln_gain[l].reshape(HEAD_DIM, 1)

        lanes = lambda v: jnp.broadcast_to(v[..., None], v.shape + (LANES,))
        kmax_a = lanes(jnp.repeat(_key_norm_max(ka, HEAD_DIM)[..., 0:1], 2, axis=-1))
        kmax_c = lanes(_key_norm_max(kc, C_QK_DIM).reshape(b, 4, 2))
        ya_l = _attend(qat_l, ka, vat, kmax_a, diff=False, tq=tq)
        yc_l = _attend(qct_l, kc, vct, kmax_c, diff=True, tq=tq, lam_qk=lam_qk[l], gain=sub_gain,
                       lam_init=lam_init)

        poolw = _block_diag([pool_w[l, g] for g in range(len(POOL_WINDOWS))]).astype(BF16)
        sgub = jnp.repeat(jnp.transpose(sgu_b[l]), d // 16, axis=1)
        wo = w_out[l]
        wo_c = jnp.pad(wo[512:768].reshape(4, HEAD_DIM, d), ((0, 0), (0, LANES - HEAD_DIM), (0, 0)))
        wout = jnp.concatenate([wo[0:512], wo_c.reshape(4 * LANES, d), wo[768:1024]], axis=0).astype(BF16)
        mixout = functools.partial(_mixout, poolw=poolw, pscale=row2(pool_scale[l]), sguw=sgu_w[l].astype(BF16),
                                   sgub=sgub, wout=wout, gpost=row2(g_post_mix[l]), gpre=row2(g_pre_ffn[l]))
        rw = jnp.pad(router_w[l], ((0, 0), (0, LANES - N_EXPERTS)))
        moe_args = dict(rw=rw, rb=router_bias[l].reshape(N_EXPERTS, 1), w1=exp_w1, w3=exp_w3,
                        w2=exp_w2, s1=sh_w1[l].astype(BF16), s3=sh_w3[l].astype(BF16),
                        s2=sh_w2[l].astype(BF16), gpost=row2(g_post_ffn[l]), layer=l)
        moe = functools.partial(_moe, **moe_args)

        xl_mid, tok_l = mixout(xl, mod_l, ya_l, yc_l, pdu_l, tm=tm_lat)
        xl = _sparse_moe(tok_l, xl_mid, mod_l, tm_route=tm_moe, tm_combine=tm_lat, **moe_args)
        if need_ctx:
            ya_c = _flash(qat_c, ka_c, vat_c, diff=False, tq=nctx)
            yc_c = _flash(qct_c, kc_c, vct_c, diff=True, tq=nctx, lam_qk=lam_qk[l], gain=sub_gain,
                          lam_init=lam_init)
            xc_mid, tok_c = mixout(xc, mod_c, ya_c, yc_c, pdu_c, tm=nctx)
            xc = moe(tok_c, xc_mid, mod_c, tm=nctx)
    return xl
```
